```python
import jax, jax.numpy as jnp
from jax import lax
import numpy as np

D_MODEL = 1024
BATCH = 32
SEQ = 2048
DEPTH = 1
DEC_BATCH = 8
DEC_SEQ = 64
PAST_LEN = 4096

CHUNK = 64
SB_HEADS = 16
SB_HEAD_DIM = 64
SB_WIDTH = SB_HEADS * SB_HEAD_DIM
SB_BLOCK = 128
GM_GROUPS = 8
GM_GROUP_DIM = 128
GM_WIDTH = GM_GROUPS * GM_GROUP_DIM
GM_CHUNK = 128
D_FF = 2816
CONV_WIDTH = 3
EPS = 1e-6
IN_WIDTH = 2 * GM_WIDTH + 3 * SB_WIDTH

kernel_name = "hybrid_stream_gmlp_stickbreak_convffn"


def rms_norm(x, g):
    xf = x.astype(jnp.float32)
    xf = xf * lax.rsqrt(jnp.mean(xf * xf, axis=-1, keepdims=True) + EPS)
    return xf.astype(x.dtype) * g


def layer_norm(x, g, b):
    xf = x.astype(jnp.float32)
    mu = jnp.mean(xf, axis=-1, keepdims=True)
    var = jnp.mean(jnp.square(xf - mu), axis=-1, keepdims=True)
    return ((xf - mu) * lax.rsqrt(var + EPS)).astype(x.dtype) * g + b


def spatial_gating(u, v, w_s, b_s):
    bsz, t, _ = v.shape
    n_pos = min(t, GM_CHUNK)
    n_chunks = t // n_pos
    tri = jnp.tril(jnp.ones((n_pos, n_pos), dtype=bool))
    w = jnp.where(tri[None], w_s[:, :n_pos, :n_pos], 0.0)
    vc = v.reshape(bsz, n_chunks, n_pos, GM_GROUPS, GM_GROUP_DIM)
    mixed = jnp.einsum('gts,bcsgd->bctgd', w, vc) + b_s[:, :n_pos].T[None, None, :, :, None]
    return u * mixed.reshape(bsz, t, GM_WIDTH)


def stick_breaking(q, k, v, q_offset):
    tq = q.shape[1]
    scale = SB_HEAD_DIM ** -0.5
    outs = []
    for start in range(0, tq, SB_BLOCK):
        stop = min(start + SB_BLOCK, tq)
        n_keys = q_offset + stop
        qb = q[:, start:stop].astype(jnp.float32)
        kb = k[:, :n_keys].astype(jnp.float32)
        vb = v[:, :n_keys]
        z = jnp.einsum('bqhd,bkhd->bhqk', qb, kb) * scale
        qpos = q_offset + start + jnp.arange(stop - start)
        kpos = jnp.arange(n_keys)
        mask = kpos[None, :] < qpos[:, None]
        log_one_minus = jnp.where(mask, jax.nn.log_sigmoid(-z), 0.0)
        after = lax.cumsum(log_one_minus, axis=3, reverse=True) - log_one_minus
        a = jnp.where(mask, jnp.exp(jax.nn.log_sigmoid(z) + after), 0.0)
        outs.append(jnp.einsum('bhqk,bkhd->bqhd', a.astype(v.dtype), vb))
    return jnp.concatenate(outs, axis=1)


def layer(x, past_k, past_v, conv_prev, q_offset,
          g_pre_mix, w_in, ln_v_g, ln_v_b, w_s, b_s, w_gate, b_gate, w_branch_a, w_branch_b, w_o,
          g_post_mix, g_pre_ffn, w_up, w_conv, b_conv, w_down, g_post_ffn):
    bsz, t, _ = x.shape
    h = rms_norm(x, g_pre_mix)
    proj = h @ w_in
    u_a, v_a, q, k, v_b = jnp.split(
        proj, [GM_WIDTH, 2 * GM_WIDTH, 2 * GM_WIDTH + SB_WIDTH, 2 * GM_WIDTH + 2 * SB_WIDTH], axis=-1)
    u_a = jax.nn.gelu(u_a)
    v_a = layer_norm(jax.nn.gelu(v_a), ln_v_g, ln_v_b)
    y_a = spatial_gating(u_a, v_a, w_s, b_s)
    q = q.reshape(bsz, t, SB_HEADS, SB_HEAD_DIM)
    k = k.reshape(bsz, t, SB_HEADS, SB_HEAD_DIM)
    v_b = v_b.reshape(bsz, t, SB_HEADS, SB_HEAD_DIM)
    if past_k is None:
        k_all, v_all = k, v_b
    else:
        k_all = jnp.concatenate([past_k, k], axis=1)
        v_all = jnp.concatenate([past_v, v_b], axis=1)
    y_b = stick_breaking(q, k_all, v_all, q_offset).reshape(bsz, t, SB_WIDTH)
    gates = jax.nn.sigmoid(h @ w_gate + b_gate)
    g_a, g_b = jnp.split(gates, 2, axis=-1)
    merged = g_a * (y_a @ w_branch_a) + g_b * (y_b @ w_branch_b)
    x = x + rms_norm(merged @ w_o, g_post_mix)
    up = rms_norm(x, g_pre_ffn) @ w_up
    buf = jnp.concatenate([conv_prev.astype(up.dtype), up], axis=1)
    conv = b_conv + sum(w_conv[i] * buf[:, i:i + t] for i in range(CONV_WIDTH))
    gate, val = jnp.split(conv, 2, axis=-1)
    ff = (jax.nn.gelu(gate) * val) @ w_down
    x = x + rms_norm(ff, g_post_ffn)
    return x, k, v_b, buf[:, -(CONV_WIDTH - 1):], v_a


def setup_inputs(seed: int = 0) -> dict:
    key = jax.random.key(seed)
    ks = jax.random.split(key, 32)
    f32 = jnp.float32

    def nrm(k, shape, scale):
        return jax.random.normal(k, shape, f32) * scale

    def gain(k, shape):
        return 1.0 + 0.02 * jax.random.normal(k, shape, f32)

    L = DEPTH
    return {
        "x_prompt": nrm(ks[0], (BATCH, SEQ, D_MODEL), 1.0),
        "x_sample": nrm(ks[1], (DEC_BATCH, DEC_SEQ, D_MODEL), 1.0),
        "cache_sb_k": nrm(ks[2], (L, DEC_BATCH, PAST_LEN, SB_HEADS, SB_HEAD_DIM), 1.0),
        "cache_sb_v": nrm(ks[3], (L, DEC_BATCH, PAST_LEN, SB_HEADS, SB_HEAD_DIM), 1.0),
        "state_ffn_conv": nrm(ks[4], (L, DEC_BATCH, CONV_WIDTH - 1, 2 * D_FF), 1.0),
        "g_pre_mix": gain(ks[5], (L, D_MODEL)),
        "w_in": nrm(ks[6], (L, D_MODEL, IN_WIDTH), D_MODEL ** -0.5),
        "ln_v_g": gain(ks[7], (L, GM_WIDTH)),
        "ln_v_b": nrm(ks[8], (L, GM_WIDTH), 0.02),
        "w_s": nrm(ks[9], (L, GM_GROUPS, GM_CHUNK, GM_CHUNK), 0.5 * GM_CHUNK ** -0.5),
        "b_s": gain(ks[10], (L, GM_GROUPS, GM_CHUNK)),
        "w_gate": nrm(ks[11], (L, D_MODEL, 2 * D_MODEL), D_MODEL ** -0.5),
        "b_gate": nrm(ks[12], (L, 2 * D_MODEL), 0.01),
        "w_branch_a": nrm(ks[13], (L, GM_WIDTH, D_MODEL), GM_WIDTH ** -0.5),
        "w_branch_b": nrm(ks[14], (L, SB_WIDTH, D_MODEL), SB_WIDTH ** -0.5),
        "w_o": nrm(ks[15], (L, D_MODEL, D_MODEL), D_MODEL ** -0.5),
        "g_post_mix": gain(ks[16], (L, D_MODEL)),
        "g_pre_ffn": gain(ks[17], (L, D_MODEL)),
        "w_up": nrm(ks[18], (L, D_MODEL, 2 * D_FF), D_MODEL ** -0.5),
        "w_conv": nrm(ks[19], (L, CONV_WIDTH, 2 * D_FF), CONV_WIDTH ** -0.5),
        "b_conv": nrm(ks[20], (L, 2 * D_FF), 0.02),
        "w_down": nrm(ks[21], (L, D_FF, D_MODEL), D_FF ** -0.5),
        "g_post_ffn": gain(ks[22], (L, D_MODEL)),
    }


def reference(x_prompt, x_sample, cache_sb_k, cache_sb_v, state_ffn_conv,
              g_pre_mix, w_in, ln_v_g, ln_v_b, w_s, b_s, w_gate, b_gate, w_branch_a, w_branch_b, w_o,
              g_post_mix, g_pre_ffn, w_up, w_conv, b_conv, w_down, g_post_ffn):
    y_p, y_s = x_prompt, x_sample
    k_p, v_p, c_p, k_s, v_s, c_s, gv_s = [], [], [], [], [], [], []
    for l in range(DEPTH):
        params = (g_pre_mix[l], w_in[l], ln_v_g[l], ln_v_b[l], w_s[l], b_s[l], w_gate[l], b_gate[l],
                  w_branch_a[l], w_branch_b[l], w_o[l], g_post_mix[l], g_pre_ffn[l], w_up[l],
                  w_conv[l], b_conv[l], w_down[l], g_post_ffn[l])
        zero_conv = jnp.zeros((y_p.shape[0], CONV_WIDTH - 1, 2 * D_FF), y_p.dtype)
        y_p, kp, vp, cp, _ = layer(y_p, None, None, zero_conv, 0, *params)
        past_len = cache_sb_k.shape[2]
        y_s, ks_, vs_, cs_, gvs = layer(y_s, cache_sb_k[l], cache_sb_v[l], state_ffn_conv[l], past_len, *params)
        k_p.append(kp); v_p.append(vp); c_p.append(cp)
        k_s.append(ks_); v_s.append(vs_); c_s.append(cs_); gv_s.append(gvs)
    new_sb_k_prompt = jnp.stack(k_p)
    new_sb_v_prompt = jnp.stack(v_p)
    new_conv_prompt = jnp.stack(c_p)
    new_sb_k_sample = jnp.stack(k_s)
    new_sb_v_sample = jnp.stack(v_s)
    new_conv_sample = jnp.stack(c_s)
    new_gmlp_v_sample = jnp.stack(gv_s)
    return (y_p, y_s, new_sb_k_prompt, new_sb_v_prompt, new_conv_prompt,
            new_sb_k_sample, new_sb_v_sample, new_conv_sample, new_gmlp_v_sample)
```

```python
import functools

import jax
import jax.numpy as jnp
from jax import lax
from jax.experimental import pallas as pl
from jax.experimental.pallas import tpu as pltpu

F32 = jnp.float32
BF16 = jnp.bfloat16

EPS = 1e-6
SB_HEADS = 16
SB_HEAD_DIM = 64
GM_GROUPS = 8
GM_GROUP_DIM = 128
GM_CHUNK = 128
CONV_WIDTH = 3

LANES = 128
SUBLANES = 8
MXU_COLS = 256
HEADS_PER_LANE_TILE = LANES // SB_HEAD_DIM
V7X_SCOPED_VMEM_BYTES = 60000 * 1024

_ARB = "arbitrary"


def _rms(x, g):
    return x * lax.rsqrt(jnp.mean(x * x, axis=-1, keepdims=True) + EPS) * g


def _dot(a, b):
    return jnp.dot(a, b, preferred_element_type=F32)


def _params(n_grid, vmem_bytes):
    return pltpu.CompilerParams(
        dimension_semantics=(_ARB,) * n_grid,
        vmem_limit_bytes=int(min(vmem_bytes, V7X_SCOPED_VMEM_BYTES)))


def _full(shape):
    nd = len(shape)
    return pl.BlockSpec(shape, lambda *_: (0,) * nd)


def _proj_kernel(x_ref, gpre_ref, win_ref, lng_ref, lnb_ref, ws_ref, bias_ref, wg_ref, bg_ref,
                 ya_ref, q_ref, k_ref, v_ref, gate_ref, va_ref, hb_s, u_s, gv_s, vab_s, *, tm, chunk):
    gm = GM_GROUPS * GM_GROUP_DIM
    sb = SB_HEADS * SB_HEAD_DIM
    hb_s[...] = _rms(x_ref[...], gpre_ref[...]).astype(BF16)

    def proj(col0, n):
        cols = slice(col0 + n * MXU_COLS, col0 + (n + 1) * MXU_COLS)
        return _dot(hb_s[...], win_ref[:, cols])

    for n in range(gm // MXU_COLS):
        oc = slice(n * MXU_COLS, (n + 1) * MXU_COLS)
        u_s[:, oc] = jax.nn.gelu(proj(0, n))
        gv_s[:, oc] = jax.nn.gelu(proj(gm, n))

    gv = gv_s[...]
    mu = jnp.mean(gv, axis=-1, keepdims=True)
    dev = gv - mu
    var = jnp.mean(dev * dev, axis=-1, keepdims=True)
    va = dev * lax.rsqrt(var + EPS) * lng_ref[...] + lnb_ref[...]
    va_ref[...] = va
    vab_s[...] = va.astype(BF16)

    nch = tm // chunk
    for g in range(GM_GROUPS):
        cols = slice(g * GM_GROUP_DIM, (g + 1) * GM_GROUP_DIM)
        pieces = [vab_s[c * chunk:(c + 1) * chunk, cols] for c in range(nch)]
        rhs = pieces[0] if nch == 1 else jnp.concatenate(pieces, axis=1)
        mixed = _dot(ws_ref[g], rhs)
        for c in range(nch):
            rows = slice(c * chunk, (c + 1) * chunk)
            m = mixed[:, c * GM_GROUP_DIM:(c + 1) * GM_GROUP_DIM] + bias_ref[:, cols]
            ya_ref[rows, cols] = (u_s[rows, cols] * m).astype(BF16)

    scale = SB_HEAD_DIM ** -0.5
    for n in range(sb // MXU_COLS):
        oc = slice(n * MXU_COLS, (n + 1) * MXU_COLS)
        q_ref[:, oc] = (proj(2 * gm, n) * scale).astype(BF16)
        k_ref[:, oc] = proj(2 * gm + sb, n)
        v_ref[:, oc] = proj(2 * gm + 2 * sb, n)

    for n in range(gate_ref.shape[1] // MXU_COLS):
        oc = slice(n * MXU_COLS, (n + 1) * MXU_COLS)
        gate_ref[:, oc] = jax.nn.sigmoid(_dot(hb_s[...], wg_ref[:, oc]) + bg_ref[:, oc]).astype(BF16)


def _proj_call(x2d, gpre, win, lng, lnb, ws, bias, wg, bg, *, tm, chunk):
    n_tok, d = x2d.shape
    gm = GM_GROUPS * GM_GROUP_DIM
    sb = SB_HEADS * SB_HEAD_DIM
    n_gate = wg.shape[1]

    def tok(width):
        return pl.BlockSpec((tm, width), lambda i: (i, 0))

    weight_bytes = 2 * (win.size + wg.size + ws.size) * 2
    block_bytes = 2 * tm * (d * 4 + gm * 2 + sb * 2 + sb * 4 + sb * 4 + n_gate * 2 + gm * 4)
    scratch_bytes = tm * (d * 2 + gm * 4 + gm * 4 + gm * 2)
    return pl.pallas_call(
        functools.partial(_proj_kernel, tm=tm, chunk=chunk),
        grid=(n_tok // tm,),
        in_specs=[tok(d), _full(gpre.shape), _full(win.shape), _full(lng.shape), _full(lnb.shape),
                  _full(ws.shape), _full(bias.shape), _full(wg.shape), _full(bg.shape)],
        out_specs=[tok(gm), tok(sb), tok(sb), tok(sb), tok(n_gate), tok(gm)],
        out_shape=[jax.ShapeDtypeStruct((n_tok, gm), BF16),
                   jax.ShapeDtypeStruct((n_tok, sb), BF16),
                   jax.ShapeDtypeStruct((n_tok, sb), F32),
                   jax.ShapeDtypeStruct((n_tok, sb), F32),
                   jax.ShapeDtypeStruct((n_tok, n_gate), BF16),
                   jax.ShapeDtypeStruct((n_tok, gm), F32)],
        scratch_shapes=[pltpu.VMEM((tm, d), BF16), pltpu.VMEM((tm, gm), F32),
                        pltpu.VMEM((tm, gm), F32), pltpu.VMEM((tm, gm), BF16)],
        compiler_params=_params(1, weight_bytes + block_bytes + 2 * scratch_bytes),
        name="proj",
    )(x2d, gpre, win, lng, lnb, ws, bias, wg, bg)


def _sb_block(qm, kb, vm, u_tri, c, mask):
    z = lax.dot_general(qm, kb, (((1,), (1,)), ((), ())), preferred_element_type=F32)
    soft = jnp.log1p(jnp.exp(-jnp.abs(z)))
    log_one_minus = jnp.minimum(-z, 0.0) - soft
    log_beta = jnp.minimum(z, 0.0) - soft
    if mask is not None:
        log_one_minus = jnp.where(mask, log_one_minus, 0.0)
    hi = log_one_minus.astype(BF16)
    lo = (log_one_minus - hi.astype(F32)).astype(BF16)
    cum = _dot(hi, u_tri) + _dot(lo, u_tri)
    a = jnp.exp(log_beta + (cum + c))
    if mask is not None:
        a = jnp.where(mask, a, 0.0)
    contrib = _dot(a.astype(BF16), vm)
    c_new = c + (cum[:, 0:1] + log_one_minus[:, 0:1])
    return contrib, c_new


def _tri_tables(n):
    r = lax.broadcasted_iota(jnp.int32, (n, n), 0)
    s = lax.broadcasted_iota(jnp.int32, (n, n), 1)
    return r, s


def _head_masks():
    lane = lax.broadcasted_iota(jnp.int32, (1, LANES), 1)
    return [(lane >= h * SB_HEAD_DIM) & (lane < (h + 1) * SB_HEAD_DIM) for h in range(HEADS_PER_LANE_TILE)]


def _sb_prompt_kernel(q_ref, k_ref, v_ref, o_ref, acc_s, c_s, u_s, *, tq):
    i = pl.program_id(2)
    r, s = _tri_tables(tq)
    u_s[...] = jnp.where(r > s, 1.0, 0.0).astype(BF16)
    mask = s < r
    hms = _head_masks()
    q = q_ref[...]
    qms = [jnp.where(hm, q, jnp.zeros_like(q)) for hm in hms]
    acc_s[...] = jnp.zeros_like(acc_s)
    c_s[...] = jnp.zeros_like(c_s)

    def tile(j, m):
        start = pl.multiple_of(j * tq, tq)
        kb = k_ref[pl.ds(start, tq), :].astype(BF16)
        vb = v_ref[pl.ds(start, tq), :].astype(BF16)
        total = acc_s[...]
        for h, hm in enumerate(hms):
            vm = jnp.where(hm, vb, jnp.zeros_like(vb))
            contrib, c_new = _sb_block(qms[h], kb, vm, u_s[...], c_s[h], m)
            c_s[h] = c_new
            total = total + contrib
        acc_s[...] = total

    tile(i, mask)

    def body(n, carry):
        tile(i - 1 - n, None)
        return carry

    lax.fori_loop(0, i, body, 0)
    o_ref[...] = acc_s[...].astype(BF16)


def _sb_prompt_call(q, k, v, *, tq):
    b, t, w = q.shape
    n_tiles = w // LANES
    qspec = pl.BlockSpec((None, tq, LANES), lambda bi, hp, i: (bi, i, hp))
    kvspec = pl.BlockSpec((None, t, LANES), lambda bi, hp, i: (bi, 0, hp))
    vmem = (2 * 2 * t * LANES * 4 + 2 * 2 * tq * LANES * 2
            + (1 + HEADS_PER_LANE_TILE) * tq * LANES * 4 + tq * tq * 2 + 16 * tq * tq * 4)
    return pl.pallas_call(
        functools.partial(_sb_prompt_kernel, tq=tq),
        grid=(b, n_tiles, t // tq),
        in_specs=[qspec, kvspec, kvspec],
        out_specs=qspec,
        out_shape=jax.ShapeDtypeStruct((b, t, w), BF16),
        scratch_shapes=[pltpu.VMEM((tq, LANES), F32),
                        pltpu.VMEM((HEADS_PER_LANE_TILE, tq, 1), F32),
                        pltpu.VMEM((tq, tq), BF16)],
        compiler_params=_params(3, 2 * vmem),
        name="sb_prompt",
    )(q, k, v)


def _sb_sample_kernel(q_ref, kn_ref, vn_ref, kp_ref, vp_ref, o_ref, acc_s, c_s, u_s, *, tq, tk, n_past):
    r, s = _tri_tables(tk)
    u_s[...] = jnp.where(r > s, 1.0, 0.0).astype(BF16)
    rq, sq = _tri_tables(tq)
    mask = sq < rq
    hms = _head_masks()
    q = q_ref[...]
    qms = [jnp.where(hm, q, jnp.zeros_like(q)) for hm in hms]
    acc_s[...] = jnp.zeros_like(acc_s)
    c_s[...] = jnp.zeros_like(c_s)

    def tile(kb, vb, u_tri, m):
        total = acc_s[...]
        for h, hm in enumerate(hms):
            vm = jnp.where(hm, vb, jnp.zeros_like(vb))
            contrib, c_new = _sb_block(qms[h], kb, vm, u_tri, c_s[h], m)
            c_s[h] = c_new
            total = total + contrib
        acc_s[...] = total

    tile(kn_ref[...].astype(BF16), vn_ref[...].astype(BF16), u_s[0:tq, 0:tq], mask)

    def body(n, carry):
        start = pl.multiple_of((n_past - 1 - n) * tk, tk)
        tile(kp_ref[pl.ds(start, tk), :].astype(BF16), vp_ref[pl.ds(start, tk), :].astype(BF16), u_s[...], None)
        return carry

    lax.fori_loop(0, n_past, body, 0)
    o_ref[...] = acc_s[...].astype(BF16)


def _sb_sample_call(q, k_new, v_new, k_past, v_past, *, tk):
    b, tq, w = q.shape
    p = k_past.shape[1]
    n_tiles = w // LANES
    newspec = pl.BlockSpec((None, tq, LANES), lambda bi, hp: (bi, 0, hp))
    pastspec = pl.BlockSpec((None, p, LANES), lambda bi, hp: (bi, 0, hp))
    vmem = (2 * 2 * p * LANES * 4 + 2 * 4 * tq * LANES * 4
            + (1 + HEADS_PER_LANE_TILE) * tq * LANES * 4 + tk * tk * 2 + 16 * tq * tk * 4)
    return pl.pallas_call(
        functools.partial(_sb_sample_kernel, tq=tq, tk=tk, n_past=p // tk),
        grid=(b, n_tiles),
        in_specs=[newspec, newspec, newspec, pastspec, pastspec],
        out_specs=newspec,
        out_shape=jax.ShapeDtypeStruct((b, tq, w), BF16),
        scratch_shapes=[pltpu.VMEM((tq, LANES), F32),
                        pltpu.VMEM((HEADS_PER_LANE_TILE, tq, 1), F32),
                        pltpu.VMEM((tk, tk), BF16)],
        compiler_params=_params(2, 2 * vmem),
        name="sb_sample",
    )(q, k_new, v_new, k_past, v_past)


def _merge_kernel(x_ref, ya_ref, yb_ref, gate_ref, wa_ref, wb_ref, wo_ref, gpost_ref, o_ref, m_s, p_s):
    d = x_ref.shape[1]
    for n in range(d // MXU_COLS):
        cols = slice(n * MXU_COLS, (n + 1) * MXU_COLS)
        gcols = slice(d + n * MXU_COLS, d + (n + 1) * MXU_COLS)
        merged = (gate_ref[:, cols].astype(F32) * _dot(ya_ref[...], wa_ref[:, cols])
                  + gate_ref[:, gcols].astype(F32) * _dot(yb_ref[...], wb_ref[:, cols]))
        m_s[:, cols] = merged.astype(BF16)
    for n in range(d // MXU_COLS):
        cols = slice(n * MXU_COLS, (n + 1) * MXU_COLS)
        p_s[:, cols] = _dot(m_s[...], wo_ref[:, cols])
    o_ref[...] = x_ref[...] + _rms(p_s[...], gpost_ref[...])


def _merge_call(x2d, ya, yb, gates, wa, wb, wo, gpost, *, tm):
    n_tok, d = x2d.shape

    def tok(width):
        return pl.BlockSpec((tm, width), lambda i: (i, 0))

    weight_bytes = 2 * (wa.size + wb.size + wo.size) * 2
    block_bytes = 2 * tm * (d * 4 + ya.shape[1] * 2 + yb.shape[1] * 2 + gates.shape[1] * 2 + d * 4)
    scratch_bytes = tm * d * 6
    return pl.pallas_call(
        _merge_kernel,
        grid=(n_tok // tm,),
        in_specs=[tok(d), tok(ya.shape[1]), tok(yb.shape[1]), tok(gates.shape[1]),
                  _full(wa.shape), _full(wb.shape), _full(wo.shape), _full(gpost.shape)],
        out_specs=tok(d),
        out_shape=jax.ShapeDtypeStruct((n_tok, d), F32),
        scratch_shapes=[pltpu.VMEM((tm, d), BF16), pltpu.VMEM((tm, d), F32)],
        compiler_params=_params(1, weight_bytes + block_bytes + 2 * scratch_bytes),
        name="merge",
    )(x2d, ya, yb, gates, wa, wb, wo, gpost)


def _ffn_kernel(x_ref, prev_ref, gpre_ref, wup_ref, wc_ref, bc_ref, wdown_ref, gpost_ref,
                y_ref, newconv_ref, hn_s, carry_s, ug_s, uv_s, act_s, f_s, *, tm):
    dff = wdown_ref.shape[0]
    d = x_ref.shape[1]
    lead = SUBLANES - (CONV_WIDTH - 1)
    t = pl.program_id(1)

    @pl.when(t == 0)
    def _():
        carry_s[lead:SUBLANES, :] = prev_ref[...]

    hn_s[...] = _rms(x_ref[...], gpre_ref[...]).astype(BF16)
    for n in range(dff // MXU_COLS):
        halves = []
        for off, buf in ((0, ug_s), (dff, uv_s)):
            cols = slice(off + n * MXU_COLS, off + (n + 1) * MXU_COLS)
            buf[lead:SUBLANES, :] = carry_s[lead:SUBLANES, cols]
            buf[SUBLANES:SUBLANES + tm, :] = _dot(hn_s[...], wup_ref[:, cols])
            conv = bc_ref[:, cols]
            for w in range(CONV_WIDTH):
                conv = conv + wc_ref[w:w + 1, cols] * buf[lead + w:lead + w + tm, :]
            carry_s[lead:SUBLANES, cols] = buf[lead + tm:SUBLANES + tm, :]
            halves.append(conv)
        act_s[:, n * MXU_COLS:(n + 1) * MXU_COLS] = (jax.nn.gelu(halves[0]) * halves[1]).astype(BF16)
    newconv_ref[...] = carry_s[lead:SUBLANES, :]
    for n in range(d // MXU_COLS):
        cols = slice(n * MXU_COLS, (n + 1) * MXU_COLS)
        f_s[:, cols] = _dot(act_s[...], wdown_ref[:, cols])
    y_ref[...] = x_ref[...] + _rms(f_s[...], gpost_ref[...])


def _ffn_call(x, conv_prev, gpre, wup, wc, bc, wdown, gpost, *, tm):
    b, t, d = x.shape
    dff = wdown.shape[0]
    xspec = pl.BlockSpec((None, tm, d), lambda bi, ti: (bi, ti, 0))
    cspec = pl.BlockSpec((None, CONV_WIDTH - 1, 2 * dff), lambda bi, ti: (bi, 0, 0))
    weight_bytes = 2 * (wup.size + wdown.size) * 2
    block_bytes = 2 * 2 * tm * d * 4
    scratch_bytes = (tm * d * 2 + SUBLANES * 2 * dff * 4 + 2 * (tm + SUBLANES) * MXU_COLS * 4
                     + tm * dff * 2 + tm * d * 4)
    return pl.pallas_call(
        functools.partial(_ffn_kernel, tm=tm),
        grid=(b, t // tm),
        in_specs=[xspec, cspec, _full(gpre.shape), _full(wup.shape), _full(wc.shape), _full(bc.shape),
                  _full(wdown.shape), _full(gpost.shape)],
        out_specs=[xspec, cspec],
        out_shape=[jax.ShapeDtypeStruct((b, t, d), F32),
                   jax.ShapeDtypeStruct((b, CONV_WIDTH - 1, 2 * dff), F32)],
        scratch_shapes=[pltpu.VMEM((tm, d), BF16),
                        pltpu.VMEM((SUBLANES, 2 * dff), F32),
                        pltpu.VMEM((tm + SUBLANES, MXU_COLS), F32),
                        pltpu.VMEM((tm + SUBLANES, MXU_COLS), F32),
                        pltpu.VMEM((tm, dff), BF16),
                        pltpu.VMEM((tm, d), F32)],
        compiler_params=_params(2, weight_bytes + block_bytes + 2 * scratch_bytes),
        name="ffn",
    )(x, conv_prev, gpre, wup, wc, bc, wdown, gpost)


def _token_block(n_tok, preferred=256):
    tm = min(preferred, n_tok)
    assert n_tok % tm == 0, (n_tok, tm)
    return tm


def _layer(x, past_k, past_v, conv_prev, w):
    b, t, d = x.shape
    sbw = SB_HEADS * SB_HEAD_DIM
    chunk = min(t, GM_CHUNK)
    assert t % chunk == 0
    n_tok = b * t
    tm = _token_block(n_tok)
    assert tm % chunk == 0

    tri = jnp.tril(jnp.ones((chunk, chunk), dtype=bool))
    ws = jnp.where(tri[None], w["w_s"][:, :chunk, :chunk], 0.0).astype(BF16)
    bias = jnp.repeat(w["b_s"][:, :chunk].T, GM_GROUP_DIM, axis=1)

    ya, q, k, v, gates, va = _proj_call(
        x.reshape(n_tok, d), w["g_pre_mix"], w["w_in"], w["ln_v_g"], w["ln_v_b"], ws, bias,
        w["w_gate"], w["b_gate"], tm=tm, chunk=chunk)

    q3, k3, v3 = (a.reshape(b, t, sbw) for a in (q, k, v))
    if past_k is None:
        yb = _sb_prompt_call(q3, k3, v3, tq=min(t, 256))
    else:
        yb = _sb_sample_call(q3, k3, v3, past_k, past_v, tk=min(past_k.shape[1], 256))

    x1 = _merge_call(x.reshape(n_tok, d), ya, yb.reshape(n_tok, sbw), gates,
                     w["w_branch_a"], w["w_branch_b"], w["w_o"], w["g_post_mix"], tm=tm)

    y, conv_new = _ffn_call(x1.reshape(b, t, d), conv_prev, w["g_pre_ffn"], w["w_up"], w["w_conv"],
                            w["b_conv"], w["w_down"], w["g_post_ffn"], tm=min(t, 256))
    k4 = k.reshape(b, t, SB_HEADS, SB_HEAD_DIM)
    v4 = v.reshape(b, t, SB_HEADS, SB_HEAD_DIM)
    return y, k4, v4, conv_new, va.reshape(b, t, -1)


def kernel(x_prompt, x_sample, cache_sb_k, cache_sb_v, state_ffn_conv, g_pre_mix, w_in, ln_v_g, ln_v_b, w_s, b_s,
           w_gate, b_gate, w_branch_a, w_branch_b, w_o, g_post_mix, g_pre_ffn, w_up, w_conv, b_conv, w_down,
           g_post_ffn):
    depth = w_in.shape[0]
    sbw = SB_HEADS * SB_HEAD_DIM
    y_p, y_s = x_prompt, x_sample
    outs = [[] for _ in range(7)]
    for l in range(depth):
        row = lambda a: a[l][None, :]
        w = dict(
            g_pre_mix=row(g_pre_mix), w_in=w_in[l].astype(BF16), ln_v_g=row(ln_v_g), ln_v_b=row(ln_v_b),
            w_s=w_s[l], b_s=b_s[l], w_gate=w_gate[l].astype(BF16), b_gate=row(b_gate),
            w_branch_a=w_branch_a[l].astype(BF16), w_branch_b=w_branch_b[l].astype(BF16),
            w_o=w_o[l].astype(BF16), g_post_mix=row(g_post_mix), g_pre_ffn=row(g_pre_ffn),
            w_up=w_up[l].astype(BF16), w_conv=w_conv[l], b_conv=row(b_conv),
            w_down=w_down[l].astype(BF16), g_post_ffn=row(g_post_ffn))
        zero_conv = jnp.zeros((y_p.shape[0], CONV_WIDTH - 1, w_up.shape[2]), y_p.dtype)
        y_p, kp, vp, cp, _ = _layer(y_p, None, None, zero_conv, w)
        bs, past = cache_sb_k.shape[1], cache_sb_k.shape[2]
        y_s, ks, vs, cs, gvs = _layer(y_s, cache_sb_k[l].reshape(bs, past, sbw),
                                      cache_sb_v[l].reshape(bs, past, sbw), state_ffn_conv[l], w)
        for lst, a in zip(outs, (kp, vp, cp, ks, vs, cs, gvs)):
            lst.append(a)
    return (y_p, y_s) + tuple(jnp.stack(o) for o in outs)
```

```python
import functools
import math

import jax
import jax.numpy as jnp
from jax import lax
from jax.experimental import pallas as pl
from jax.experimental.pallas import tpu as pltpu

F32 = jnp.float32
BF16 = jnp.bfloat16

EPS = 1e-6
SB_HEADS = 16
SB_HEAD_DIM = 64
GM_GROUPS = 8
GM_GROUP_DIM = 128
GM_CHUNK = 128
CONV_WIDTH = 3

LANES = 128
SUBLANES = 8
MXU_COLS = 256
HEADS_PER_LANE_TILE = LANES // SB_HEAD_DIM
HEADS_PER_MXU_TILE = MXU_COLS // SB_HEAD_DIM
V7X_SCOPED_VMEM_BYTES = 60000 * 1024

LOG2E = math.log2(math.e)
F32_EXP2_UNDERFLOW = 150.0
SIGN_BIT = 0x80000000

_ARB = "arbitrary"
_NT = (((1,), (1,)), ((), ()))


def _rms(x, g):
    return x * lax.rsqrt(jnp.mean(x * x, axis=-1, keepdims=True) + EPS) * g


def _dot(a, b):
    return jnp.dot(a, b, preferred_element_type=F32)


def _dot_nt(a, b):
    return lax.dot_general(a, b, _NT, preferred_element_type=F32)


def _params(n_grid, vmem_bytes):
    return pltpu.CompilerParams(
        dimension_semantics=(_ARB,) * n_grid,
        vmem_limit_bytes=int(min(vmem_bytes, V7X_SCOPED_VMEM_BYTES)))


def _full(shape):
    nd = len(shape)
    return pl.BlockSpec(shape, lambda *_: (0,) * nd)


def _proj_kernel(x_ref, gpre_ref, win_ref, wk_ref, wv_ref, lng_ref, lnb_ref, ws_ref, bias_ref, wg_ref, bg_ref,
                 ya_ref, q_ref, k_ref, v_ref, kb_ref, vb_ref, gate_ref, *rest, tm, chunk, transposed_kv, emit_va):
    if emit_va:
        va_ref, hb_s, u_s, gv_s, vab_s = rest
    else:
        hb_s, u_s, gv_s, vab_s = rest
    gm = GM_GROUPS * GM_GROUP_DIM
    sb = SB_HEADS * SB_HEAD_DIM
    hb_s[...] = _rms(x_ref[...], gpre_ref[...]).astype(BF16)

    def proj(col0, n):
        cols = slice(col0 + n * MXU_COLS, col0 + (n + 1) * MXU_COLS)
        return _dot(hb_s[...], win_ref[:, cols])

    for n in range(gm // MXU_COLS):
        oc = slice(n * MXU_COLS, (n + 1) * MXU_COLS)
        u_s[:, oc] = jax.nn.gelu(proj(0, n))
        gv_s[:, oc] = jax.nn.gelu(proj(gm, n))

    gv = gv_s[...]
    mu = jnp.mean(gv, axis=-1, keepdims=True)
    dev = gv - mu
    var = jnp.mean(dev * dev, axis=-1, keepdims=True)
    va = dev * lax.rsqrt(var + EPS) * lng_ref[...] + lnb_ref[...]
    if emit_va:
        va_ref[...] = va
    vab_s[...] = va.astype(BF16)

    nch = tm // chunk
    for g in range(GM_GROUPS):
        cols = slice(g * GM_GROUP_DIM, (g + 1) * GM_GROUP_DIM)
        pieces = [vab_s[c * chunk:(c + 1) * chunk, cols] for c in range(nch)]
        rhs = pieces[0] if nch == 1 else jnp.concatenate(pieces, axis=1)
        mixed = _dot(ws_ref[g], rhs)
        for c in range(nch):
            rows = slice(c * chunk, (c + 1) * chunk)
            m = mixed[:, c * GM_GROUP_DIM:(c + 1) * GM_GROUP_DIM] + bias_ref[:, cols]
            ya_ref[rows, cols] = (u_s[rows, cols] * m).astype(BF16)

    qscale = SB_HEAD_DIM ** -0.5 * LOG2E
    for n in range(sb // MXU_COLS):
        oc = slice(n * MXU_COLS, (n + 1) * MXU_COLS)
        q_ref[:, oc] = (proj(2 * gm, n) * qscale).astype(BF16)

    for w_ref, f32_out, bf16_out in ((wk_ref, k_ref, kb_ref), (wv_ref, v_ref, vb_ref)):
        for n in range(sb // MXU_COLS):
            oc = slice(n * MXU_COLS, (n + 1) * MXU_COLS)
            if transposed_kv:
                pt = _dot_nt(w_ref[oc, :], hb_s[...])
                f32_out[oc, :] = pt
                bf16_out[oc, :] = pt.astype(BF16)
            else:
                p = _dot(hb_s[...], w_ref[:, oc])
                bf16_out[:, oc] = p.astype(BF16)
                for hs in range(HEADS_PER_MXU_TILE):
                    h = n * HEADS_PER_MXU_TILE + hs
                    f32_out[pl.ds(h, tm, stride=SB_HEADS), :] = p[:, hs * SB_HEAD_DIM:(hs + 1) * SB_HEAD_DIM]

    for n in range(gate_ref.shape[1] // MXU_COLS):
        oc = slice(n * MXU_COLS, (n + 1) * MXU_COLS)
        gate_ref[:, oc] = jax.nn.sigmoid(_dot(hb_s[...], wg_ref[:, oc]) + bg_ref[:, oc]).astype(BF16)


def _proj_call(x2d, gpre, win, wk, wv, lng, lnb, ws, bias, wg, bg, *, tm, chunk, seq_len, transposed_kv, emit_va):
    n_tok, d = x2d.shape
    gm = GM_GROUPS * GM_GROUP_DIM
    sb = SB_HEADS * SB_HEAD_DIM
    n_gate = wg.shape[1]

    def tok(width):
        return pl.BlockSpec((tm, width), lambda i: (i, 0))

    if transposed_kv:
        nt = seq_len // tm
        kv_spec = pl.BlockSpec((None, sb, tm), lambda i: (i // nt, 0, i % nt))
        kv_f32 = jax.ShapeDtypeStruct((n_tok // seq_len, sb, seq_len), F32)
        kv_bf16 = jax.ShapeDtypeStruct((n_tok // seq_len, sb, seq_len), BF16)
        kv_specs = [kv_spec] * 4
        kv_bytes = 2 * 2 * sb * tm * (4 + 2)
    else:
        hm_spec = pl.BlockSpec((tm * SB_HEADS, SB_HEAD_DIM), lambda i: (i, 0))
        kv_f32 = jax.ShapeDtypeStruct((n_tok * SB_HEADS, SB_HEAD_DIM), F32)
        kv_bf16 = jax.ShapeDtypeStruct((n_tok, sb), BF16)
        kv_specs = [hm_spec, hm_spec, tok(sb), tok(sb)]
        kv_bytes = 2 * 2 * tm * (SB_HEADS * LANES * 4 + sb * 2)
    out_specs = [tok(gm), tok(sb)] + kv_specs + [tok(n_gate)]
    out_shape = [jax.ShapeDtypeStruct((n_tok, gm), BF16), jax.ShapeDtypeStruct((n_tok, sb), BF16),
                 kv_f32, kv_f32, kv_bf16, kv_bf16, jax.ShapeDtypeStruct((n_tok, n_gate), BF16)]
    if emit_va:
        out_specs.append(tok(gm))
        out_shape.append(jax.ShapeDtypeStruct((n_tok, gm), F32))

    weight_bytes = 2 * (win.size + wk.size + wv.size + wg.size + ws.size) * 2
    block_bytes = 2 * tm * (d * 4 + gm * 2 + sb * 2 + n_gate * 2 + gm * 4) + kv_bytes
    scratch_bytes = tm * (d * 2 + gm * 4 + gm * 4 + gm * 2)
    return pl.pallas_call(
        functools.partial(_proj_kernel, tm=tm, chunk=chunk, transposed_kv=transposed_kv, emit_va=emit_va),
        grid=(n_tok // tm,),
        in_specs=[tok(d), _full(gpre.shape), _full(win.shape), _full(wk.shape), _full(wv.shape),
                  _full(lng.shape), _full(lnb.shape), _full(ws.shape), _full(bias.shape), _full(wg.shape),
                  _full(bg.shape)],
        out_specs=out_specs,
        out_shape=out_shape,
        scratch_shapes=[pltpu.VMEM((tm, d), BF16), pltpu.VMEM((tm, gm), F32),
                        pltpu.VMEM((tm, gm), F32), pltpu.VMEM((tm, gm), BF16)],
        compiler_params=_params(1, weight_bytes + block_bytes + 2 * scratch_bytes),
        name="proj",
    )(x2d, gpre, win, wk, wv, lng, lnb, ws, bias, wg, bg)


def _sb_tiles(zs, vs, v_transposed, u2, cs, mask):
    softplus, split = [], []
    for z in zs:
        neg_abs = pltpu.bitcast(pltpu.bitcast(z, jnp.uint32) | jnp.uint32(SIGN_BIT), F32)
        sp = jnp.maximum(z, 0.0) + jnp.log(1.0 + jnp.exp2(neg_abs)) * LOG2E
        if mask is not None:
            sp = jnp.where(mask, sp, 0.0)
        hi = sp.astype(BF16)
        lo = (sp - hi.astype(F32)).astype(BF16)
        softplus.append(sp)
        split.append(jnp.concatenate([hi, lo], axis=1))
    cums = [_dot(s, u2) for s in split]
    weights, c_new = [], []
    for z, sp, cum, c in zip(zs, softplus, cums, cs):
        total = sp + cum + c
        a = jnp.exp2(z - total)
        if mask is not None:
            a = jnp.where(mask, a, 0.0)
        weights.append(a.astype(BF16))
        c_new.append(total[:, 0:1])
    contribs = [(_dot_nt(a, v) if v_transposed else _dot(a, v)) for a, v in zip(weights, vs)]
    return contribs, c_new


def _stacked_tri(n):
    r = lax.broadcasted_iota(jnp.int32, (2 * n, n), 0)
    s = lax.broadcasted_iota(jnp.int32, (2 * n, n), 1)
    r = jnp.where(r >= n, r - n, r)
    return jnp.where(r > s, 1.0, 0.0).astype(BF16)


def _causal_mask(n):
    r = lax.broadcasted_iota(jnp.int32, (n, n), 0)
    s = lax.broadcasted_iota(jnp.int32, (n, n), 1)
    return s < r


def _head_lane_masks():
    lane = lax.broadcasted_iota(jnp.int32, (1, LANES), 1)
    return [(lane >= hh * SB_HEAD_DIM) & (lane < (hh + 1) * SB_HEAD_DIM) for hh in range(HEADS_PER_LANE_TILE)]


def _head_row_masks():
    row = lax.broadcasted_iota(jnp.int32, (LANES, 1), 0)
    return [(row >= hh * SB_HEAD_DIM) & (row < (hh + 1) * SB_HEAD_DIM) for hh in range(HEADS_PER_LANE_TILE)]


def _masked_queries(q_ref, qm_s, heads):
    lane_masks = _head_lane_masks()
    for h in range(heads):
        lt = h // HEADS_PER_LANE_TILE
        qt = q_ref[:, lt * LANES:(lt + 1) * LANES]
        qm_s[h] = jnp.where(lane_masks[h % HEADS_PER_LANE_TILE], qt, jnp.zeros_like(qt))


def _sb_prompt_kernel(q_ref, kt_ref, vt_ref, o_ref, acc_s, c_s, u_s, qm_s, *, tq, heads):
    i = pl.program_id(2)
    u_s[...] = _stacked_tri(tq)
    _masked_queries(q_ref, qm_s, heads)
    row_masks = _head_row_masks()
    acc_s[...] = jnp.zeros_like(acc_s)
    c_s[...] = jnp.zeros_like(c_s)

    def tile(j, mask):
        keys = pl.ds(pl.multiple_of(j * tq, tq), tq)
        zs, vms = [], []
        for h in range(heads):
            pair = slice(h // HEADS_PER_LANE_TILE * LANES, (h // HEADS_PER_LANE_TILE + 1) * LANES)
            zs.append(_dot(qm_s[h], kt_ref[pair, keys]))
            vt = vt_ref[pair, keys]
            vms.append(jnp.where(row_masks[h % HEADS_PER_LANE_TILE], vt, jnp.zeros_like(vt)))
        contribs, c_new = _sb_tiles(zs, vms, True, u_s[...], [c_s[h] for h in range(heads)], mask)
        for h in range(heads):
            c_s[h] = c_new[h]
        for lt in range(heads // HEADS_PER_LANE_TILE):
            pair = slice(lt * LANES, (lt + 1) * LANES)
            acc_s[:, pair] = acc_s[:, pair] + sum(contribs[lt * HEADS_PER_LANE_TILE:(lt + 1) * HEADS_PER_LANE_TILE])

    def live():
        return (jnp.min(c_s[...]) < F32_EXP2_UNDERFLOW).astype(jnp.int32)

    tile(i, _causal_mask(tq))

    def cond(carry):
        n, go = carry
        return jnp.logical_and(n < i, go > 0)

    def body(carry):
        n, _ = carry
        tile(i - 1 - n, None)
        return n + 1, live()

    lax.while_loop(cond, body, (jnp.int32(0), live()))
    o_ref[...] = acc_s[...].astype(BF16)


def _sb_prompt_call(q, kt, vt, *, tq, heads):
    b, t, w = q.shape
    width = heads * SB_HEAD_DIM
    qspec = pl.BlockSpec((None, tq, width), lambda bi, hg, i: (bi, i, hg))
    kvspec = pl.BlockSpec((None, width, t), lambda bi, hg, i: (bi, hg, 0))
    vmem = (2 * 2 * t * width * 2 + 2 * 2 * tq * width * 2 + tq * width * 4 + heads * tq * LANES * 4
            + 2 * tq * tq * 2 + heads * tq * LANES * 2 + heads * 12 * tq * tq * 4)
    return pl.pallas_call(
        functools.partial(_sb_prompt_kernel, tq=tq, heads=heads),
        grid=(b, w // width, t // tq),
        in_specs=[qspec, kvspec, kvspec],
        out_specs=qspec,
        out_shape=jax.ShapeDtypeStruct((b, t, w), BF16),
        scratch_shapes=[pltpu.VMEM((tq, width), F32),
                        pltpu.VMEM((heads, tq, 1), F32),
                        pltpu.VMEM((2 * tq, tq), BF16),
                        pltpu.VMEM((heads, tq, LANES), BF16)],
        compiler_params=_params(3, vmem),
        name="sb_prompt",
    )(q, kt, vt)


def _sb_sample_kernel(q_ref, kn_ref, vn_ref, kp_hbm, vp_hbm, o_ref, kbuf, vbuf, sem, acc_s, c_s, u_s, ud_s, qm_s,
                      *, tq, tk, n_past):
    b = pl.program_id(0)

    def copies(j, slot):
        keys = pl.ds(pl.multiple_of(j * tk, tk), tk)
        return (pltpu.make_async_copy(kp_hbm.at[b, :, keys], kbuf.at[slot], sem.at[0, slot]),
                pltpu.make_async_copy(vp_hbm.at[b, :, keys], vbuf.at[slot], sem.at[1, slot]))

    def start(j, slot):
        for cp in copies(j, slot):
            cp.start()

    def wait(j, slot):
        for cp in copies(j, slot):
            cp.wait()

    start(n_past - 1, 0)
    u_s[...] = _stacked_tri(tk)
    ud_s[...] = _stacked_tri(tq)
    _masked_queries(q_ref, qm_s, SB_HEADS)
    lane_masks = _head_lane_masks()
    row_masks = _head_row_masks()
    acc_s[...] = jnp.zeros_like(acc_s)
    c_s[...] = jnp.zeros_like(c_s)

    def live():
        return (jnp.min(c_s[...]) < F32_EXP2_UNDERFLOW).astype(jnp.int32)

    def accumulate(zs, vms, v_transposed, u2, mask):
        contribs, c_new = _sb_tiles(zs, vms, v_transposed, u2, [c_s[h] for h in range(SB_HEADS)], mask)
        for h in range(SB_HEADS):
            c_s[h] = c_new[h]
        for lt in range(SB_HEADS // HEADS_PER_LANE_TILE):
            pair = slice(lt * LANES, (lt + 1) * LANES)
            acc_s[:, pair] = acc_s[:, pair] + sum(contribs[lt * HEADS_PER_LANE_TILE:(lt + 1) * HEADS_PER_LANE_TILE])

    zs, vms = [], []
    for h in range(SB_HEADS):
        pair = slice(h // HEADS_PER_LANE_TILE * LANES, (h // HEADS_PER_LANE_TILE + 1) * LANES)
        zs.append(_dot_nt(qm_s[h], kn_ref[:, pair]))
        vn = vn_ref[:, pair]
        vms.append(jnp.where(lane_masks[h % HEADS_PER_LANE_TILE], vn, jnp.zeros_like(vn)))
    accumulate(zs, vms, False, ud_s[...], _causal_mask(tq))

    def cond(carry):
        n, go = carry
        return jnp.logical_and(n < n_past, go > 0)

    def body(carry):
        n, _ = carry
        j = n_past - 1 - n
        slot = lax.rem(n, 2)
        wait(j, slot)

        @pl.when(n + 1 < n_past)
        def _():
            start(j - 1, 1 - slot)

        zs, vms = [], []
        for h in range(SB_HEADS):
            pair = slice(h // HEADS_PER_LANE_TILE * LANES, (h // HEADS_PER_LANE_TILE + 1) * LANES)
            zs.append(_dot(qm_s[h], kbuf[slot, pair, :].astype(BF16)))
            vt = vbuf[slot, pair, :].astype(BF16)
            vms.append(jnp.where(row_masks[h % HEADS_PER_LANE_TILE], vt, jnp.zeros_like(vt)))
        accumulate(zs, vms, True, u_s[...], None)
        return n + 1, live()

    n_done, _ = lax.while_loop(cond, body, (jnp.int32(0), live()))

    @pl.when(n_done < n_past)
    def _():
        wait(n_past - 1 - n_done, lax.rem(n_done, 2))

    o_ref[...] = acc_s[...].astype(BF16)


def _sb_sample_call(q, k_new, v_new, kt_past, vt_past, *, tk):
    b, tq, w = q.shape
    p = kt_past.shape[2]
    newspec = pl.BlockSpec((None, tq, w), lambda bi: (bi, 0, 0))
    anyspec = pl.BlockSpec(memory_space=pl.ANY)
    vmem = (2 * 2 * w * tk * 4 + 2 * 4 * tq * w * 2 + tq * w * 4 + SB_HEADS * tq * LANES * 4
            + 2 * tk * tk * 2 + SB_HEADS * tq * LANES * 2 + SB_HEADS * 12 * tq * tk * 4)
    return pl.pallas_call(
        functools.partial(_sb_sample_kernel, tq=tq, tk=tk, n_past=p // tk),
        grid=(b,),
        in_specs=[newspec, newspec, newspec, anyspec, anyspec],
        out_specs=newspec,
        out_shape=jax.ShapeDtypeStruct((b, tq, w), BF16),
        scratch_shapes=[pltpu.VMEM((2, w, tk), F32),
                        pltpu.VMEM((2, w, tk), F32),
                        pltpu.SemaphoreType.DMA((2, 2)),
                        pltpu.VMEM((tq, w), F32),
                        pltpu.VMEM((SB_HEADS, tq, 1), F32),
                        pltpu.VMEM((2 * tk, tk), BF16),
                        pltpu.VMEM((2 * tq, tq), BF16),
                        pltpu.VMEM((SB_HEADS, tq, LANES), BF16)],
        compiler_params=_params(1, vmem),
        name="sb_sample",
    )(q, k_new, v_new, kt_past, vt_past)


def _merge_kernel(x_ref, ya_ref, yb_ref, gate_ref, wa_ref, wb_ref, wo_ref, gpost_ref, o_ref, m_s, p_s):
    d = x_ref.shape[1]
    for n in range(d // MXU_COLS):
        cols = slice(n * MXU_COLS, (n + 1) * MXU_COLS)
        gcols = slice(d + n * MXU_COLS, d + (n + 1) * MXU_COLS)
        merged = (gate_ref[:, cols].astype(F32) * _dot(ya_ref[...], wa_ref[:, cols])
                  + gate_ref[:, gcols].astype(F32) * _dot(yb_ref[...], wb_ref[:, cols]))
        m_s[:, cols] = merged.astype(BF16)
    for n in range(d // MXU_COLS):
        cols = slice(n * MXU_COLS, (n + 1) * MXU_COLS)
        p_s[:, cols] = _dot(m_s[...], wo_ref[:, cols])
    o_ref[...] = x_ref[...] + _rms(p_s[...], gpost_ref[...])


def _merge_call(x2d, ya, yb, gates, wa, wb, wo, gpost, *, tm):
    n_tok, d = x2d.shape

    def tok(width):
        return pl.BlockSpec((tm, width), lambda i: (i, 0))

    weight_bytes = 2 * (wa.size + wb.size + wo.size) * 2
    block_bytes = 2 * tm * (d * 4 + ya.shape[1] * 2 + yb.shape[1] * 2 + gates.shape[1] * 2 + d * 4)
    scratch_bytes = tm * d * 6
    return pl.pallas_call(
        _merge_kernel,
        grid=(n_tok // tm,),
        in_specs=[tok(d), tok(ya.shape[1]), tok(yb.shape[1]), tok(gates.shape[1]),
                  _full(wa.shape), _full(wb.shape), _full(wo.shape), _full(gpost.shape)],
        out_specs=tok(d),
        out_shape=jax.ShapeDtypeStruct((n_tok, d), F32),
        scratch_shapes=[pltpu.VMEM((tm, d), BF16), pltpu.VMEM((tm, d), F32)],
        compiler_params=_params(1, weight_bytes + block_bytes + 2 * scratch_bytes),
        name="merge",
    )(x2d, ya, yb, gates, wa, wb, wo, gpost)


def _ffn_kernel(x_ref, prev_ref, gpre_ref, wup_ref, wc_ref, bc_ref, wdown_ref, gpost_ref,
                y_ref, newconv_ref, hn_s, carry_s, ug_s, uv_s, act_s, f_s, *, tm):
    dff = wdown_ref.shape[0]
    d = x_ref.shape[1]
    lead = SUBLANES - (CONV_WIDTH - 1)
    t = pl.program_id(1)

    @pl.when(t == 0)
    def _():
        carry_s[lead:SUBLANES, :] = prev_ref[...]

    hn_s[...] = _rms(x_ref[...], gpre_ref[...]).astype(BF16)
    for n in range(dff // MXU_COLS):
        halves = []
        for off, buf in ((0, ug_s), (dff, uv_s)):
            cols = slice(off + n * MXU_COLS, off + (n + 1) * MXU_COLS)
            buf[lead:SUBLANES, :] = carry_s[lead:SUBLANES, cols]
            buf[SUBLANES:SUBLANES + tm, :] = _dot(hn_s[...], wup_ref[:, cols])
            conv = bc_ref[:, cols]
            for w in range(CONV_WIDTH):
                conv = conv + wc_ref[w:w + 1, cols] * buf[lead + w:lead + w + tm, :]
            carry_s[lead:SUBLANES, cols] = buf[lead + tm:SUBLANES + tm, :]
            halves.append(conv)
        act_s[:, n * MXU_COLS:(n + 1) * MXU_COLS] = (jax.nn.gelu(halves[0]) * halves[1]).astype(BF16)
    newconv_ref[...] = carry_s[lead:SUBLANES, :]
    for n in range(d // MXU_COLS):
        cols = slice(n * MXU_COLS, (n + 1) * MXU_COLS)
        f_s[:, cols] = _dot(act_s[...], wdown_ref[:, cols])
    y_ref[...] = x_ref[...] + _rms(f_s[...], gpost_ref[...])


def _ffn_call(x, conv_prev, gpre, wup, wc, bc, wdown, gpost, *, tm):
    b, t, d = x.shape
    dff = wdown.shape[0]
    xspec = pl.BlockSpec((None, tm, d), lambda bi, ti: (bi, ti, 0))
    cspec = pl.BlockSpec((None, CONV_WIDTH - 1, 2 * dff), lambda bi, ti: (bi, 0, 0))
    weight_bytes = 2 * (wup.size + wdown.size) * 2
    block_bytes = 2 * 2 * tm * d * 4
    scratch_bytes = (tm * d * 2 + SUBLANES * 2 * dff * 4 + 2 * (tm + SUBLANES) * MXU_COLS * 4
                     + tm * dff * 2 + tm * d * 4)
    return pl.pallas_call(
        functools.partial(_ffn_kernel, tm=tm),
        grid=(b, t // tm),
        in_specs=[xspec, cspec, _full(gpre.shape), _full(wup.shape), _full(wc.shape), _full(bc.shape),
                  _full(wdown.shape), _full(gpost.shape)],
        out_specs=[xspec, cspec],
        out_shape=[jax.ShapeDtypeStruct((b, t, d), F32),
                   jax.ShapeDtypeStruct((b, CONV_WIDTH - 1, 2 * dff), F32)],
        scratch_shapes=[pltpu.VMEM((tm, d), BF16),
                        pltpu.VMEM((SUBLANES, 2 * dff), F32),
                        pltpu.VMEM((tm + SUBLANES, MXU_COLS), F32),
                        pltpu.VMEM((tm + SUBLANES, MXU_COLS), F32),
                        pltpu.VMEM((tm, dff), BF16),
                        pltpu.VMEM((tm, d), F32)],
        compiler_params=_params(2, weight_bytes + block_bytes + 2 * scratch_bytes),
        name="ffn",
    )(x, conv_prev, gpre, wup, wc, bc, wdown, gpost)


def _layer(x, past_k, past_v, conv_prev, w):
    b, t, d = x.shape
    gm = GM_GROUPS * GM_GROUP_DIM
    sbw = SB_HEADS * SB_HEAD_DIM
    chunk = min(t, GM_CHUNK)
    n_tok = b * t
    decode = past_k is not None
    tm = min(256, t)
    assert t % tm == 0 and tm % chunk == 0, (t, tm, chunk)

    tri = jnp.tril(jnp.ones((chunk, chunk), dtype=bool))
    ws = jnp.where(tri[None], w["w_s"][:, :chunk, :chunk], 0.0).astype(BF16)
    bias = jnp.repeat(w["b_s"][:, :chunk].T, GM_GROUP_DIM, axis=1)
    w_in = w["w_in"]
    w_uvq = w_in[:, :2 * gm + sbw]
    w_k = w_in[:, 2 * gm + sbw:2 * gm + 2 * sbw]
    w_v = w_in[:, 2 * gm + 2 * sbw:]
    if not decode:
        w_k, w_v = w_k.T, w_v.T

    outs = _proj_call(
        x.reshape(n_tok, d), w["g_pre_mix"], w_uvq, w_k, w_v, w["ln_v_g"], w["ln_v_b"], ws, bias,
        w["w_gate"], w["b_gate"], tm=tm, chunk=chunk, seq_len=t, transposed_kv=not decode, emit_va=decode)
    ya, q, k_out, v_out, kb, vb, gates = outs[:7]
    q3 = q.reshape(b, t, sbw)

    if decode:
        p = past_k.shape[1]
        kt_past = jnp.transpose(past_k, (0, 2, 3, 1)).reshape(b, sbw, p)
        vt_past = jnp.transpose(past_v, (0, 2, 3, 1)).reshape(b, sbw, p)
        yb = _sb_sample_call(q3, kb.reshape(b, t, sbw), vb.reshape(b, t, sbw), kt_past, vt_past, tk=min(p, 256))
        k4 = k_out.reshape(b, t, SB_HEADS, SB_HEAD_DIM)
        v4 = v_out.reshape(b, t, SB_HEADS, SB_HEAD_DIM)
        va = outs[7].reshape(b, t, gm)
    else:
        yb = _sb_prompt_call(q3, kb, vb, tq=tm, heads=HEADS_PER_MXU_TILE)
        k4 = jnp.transpose(k_out.reshape(b, SB_HEADS, SB_HEAD_DIM, t), (0, 3, 1, 2))
        v4 = jnp.transpose(v_out.reshape(b, SB_HEADS, SB_HEAD_DIM, t), (0, 3, 1, 2))
        va = None

    x1 = _merge_call(x.reshape(n_tok, d), ya, yb.reshape(n_tok, sbw), gates,
                     w["w_branch_a"], w["w_branch_b"], w["w_o"], w["g_post_mix"], tm=tm)
    y, conv_new = _ffn_call(x1.reshape(b, t, d), conv_prev, w["g_pre_ffn"], w["w_up"], w["w_conv"],
                            w["b_conv"], w["w_down"], w["g_post_ffn"], tm=tm)
    return y, k4, v4, conv_new, va


def kernel(x_prompt, x_sample, cache_sb_k, cache_sb_v, state_ffn_conv, g_pre_mix, w_in, ln_v_g, ln_v_b, w_s, b_s,
           w_gate, b_gate, w_branch_a, w_branch_b, w_o, g_post_mix, g_pre_ffn, w_up, w_conv, b_conv, w_down,
           g_post_ffn):
    depth = w_in.shape[0]
    y_p, y_s = x_prompt, x_sample
    outs = [[] for _ in range(7)]
    for l in range(depth):
        row = lambda a: a[l][None, :]
        w = dict(
            g_pre_mix=row(g_pre_mix), w_in=w_in[l].astype(BF16), ln_v_g=row(ln_v_g), ln_v_b=row(ln_v_b),
            w_s=w_s[l], b_s=b_s[l], w_gate=w_gate[l].astype(BF16), b_gate=row(b_gate),
            w_branch_a=w_branch_a[l].astype(BF16), w_branch_b=w_branch_b[l].astype(BF16),
            w_o=w_o[l].astype(BF16), g_post_mix=row(g_post_mix), g_pre_ffn=row(g_pre_ffn),
            w_up=w_up[l].astype(BF16), w_conv=w_conv[l], b_conv=row(b_conv),
            w_down=w_down[l].astype(BF16), g_post_ffn=row(g_post_ffn))
        zero_conv = jnp.zeros((y_p.shape[0], CONV_WIDTH - 1, w_up.shape[2]), y_p.dtype)
        y_p, kp, vp, cp, _ = _layer(y_p, None, None, zero_conv, w)
        y_s, ks, vs, cs, gvs = _layer(y_s, cache_sb_k[l], cache_sb_v[l], state_ffn_conv[l], w)
        for lst, a in zip(outs, (kp, vp, cp, ks, vs, cs, gvs)):
            lst.append(a)
    return (y_p, y_s) + tuple(jnp.stack(o) for o in outs)
```

```python
import functools
import math

import jax
import jax.numpy as jnp
from jax import lax
from jax.experimental import pallas as pl
from jax.experimental.pallas import tpu as pltpu

F32 = jnp.float32
BF16 = jnp.bfloat16

EPS = 1e-6
SB_HEADS = 16
SB_HEAD_DIM = 64
GM_GROUPS = 8
GM_GROUP_DIM = 128
GM_CHUNK = 128
CONV_WIDTH = 3

LANES = 128
SUBLANES = 8
MXU_COLS = 256
HEADS_PER_LANE_TILE = LANES // SB_HEAD_DIM
HEADS_PER_MXU_TILE = MXU_COLS // SB_HEAD_DIM
SB_PROMPT_HEADS_PER_STEP = 16
V7X_SCOPED_VMEM_BYTES = 60000 * 1024

LOG2E = math.log2(math.e)
F32_EXP2_UNDERFLOW = 150.0
SIGN_BIT = 0x80000000

_ARB = "arbitrary"
_NT = (((1,), (1,)), ((), ()))


def _rms(x, g):
    return x * lax.rsqrt(jnp.mean(x * x, axis=-1, keepdims=True) + EPS) * g


def _dot(a, b):
    return jnp.dot(a, b, preferred_element_type=F32)


def _dot_nt(a, b):
    return lax.dot_general(a, b, _NT, preferred_element_type=F32)


def _params(n_grid, vmem_bytes):
    return pltpu.CompilerParams(
        dimension_semantics=(_ARB,) * n_grid,
        vmem_limit_bytes=int(min(vmem_bytes, V7X_SCOPED_VMEM_BYTES)))


def _full(shape):
    nd = len(shape)
    return pl.BlockSpec(shape, lambda *_: (0,) * nd)


def _proj_kernel(x_ref, gpre_ref, win_ref, wk_ref, wv_ref, lng_ref, lnb_ref, ws_ref, bias_ref, wg_ref, bg_ref,
                 ya_ref, q_ref, k_ref, v_ref, kb_ref, vb_ref, gate_ref, *rest, tm, chunk, transposed_kv, emit_va):
    if emit_va:
        va_ref, hb_s, u_s, gv_s, vab_s = rest
    else:
        hb_s, u_s, gv_s, vab_s = rest
    gm = GM_GROUPS * GM_GROUP_DIM
    sb = SB_HEADS * SB_HEAD_DIM
    hb_s[...] = _rms(x_ref[...], gpre_ref[...]).astype(BF16)

    def proj(col0, n):
        cols = slice(col0 + n * MXU_COLS, col0 + (n + 1) * MXU_COLS)
        return _dot(hb_s[...], win_ref[:, cols])

    for n in range(gm // MXU_COLS):
        oc = slice(n * MXU_COLS, (n + 1) * MXU_COLS)
        u_s[:, oc] = jax.nn.gelu(proj(0, n))
        gv_s[:, oc] = jax.nn.gelu(proj(gm, n))

    gv = gv_s[...]
    mu = jnp.mean(gv, axis=-1, keepdims=True)
    dev = gv - mu
    var = jnp.mean(dev * dev, axis=-1, keepdims=True)
    va = dev * lax.rsqrt(var + EPS) * lng_ref[...] + lnb_ref[...]
    if emit_va:
        va_ref[...] = va
    vab_s[...] = va.astype(BF16)

    nch = tm // chunk
    for g in range(GM_GROUPS):
        cols = slice(g * GM_GROUP_DIM, (g + 1) * GM_GROUP_DIM)
        pieces = [vab_s[c * chunk:(c + 1) * chunk, cols] for c in range(nch)]
        rhs = pieces[0] if nch == 1 else jnp.concatenate(pieces, axis=1)
        mixed = _dot(ws_ref[g], rhs)
        for c in range(nch):
            rows = slice(c * chunk, (c + 1) * chunk)
            m = mixed[:, c * GM_GROUP_DIM:(c + 1) * GM_GROUP_DIM] + bias_ref[:, cols]
            ya_ref[rows, cols] = (u_s[rows, cols] * m).astype(BF16)

    qscale = SB_HEAD_DIM ** -0.5 * LOG2E
    for n in range(sb // MXU_COLS):
        oc = slice(n * MXU_COLS, (n + 1) * MXU_COLS)
        q_ref[:, oc] = (proj(2 * gm, n) * qscale).astype(BF16)

    for w_ref, f32_out, bf16_out in ((wk_ref, k_ref, kb_ref), (wv_ref, v_ref, vb_ref)):
        for n in range(sb // MXU_COLS):
            oc = slice(n * MXU_COLS, (n + 1) * MXU_COLS)
            if transposed_kv:
                pt = _dot_nt(w_ref[oc, :], hb_s[...])
                f32_out[oc, :] = pt
                bf16_out[oc, :] = pt.astype(BF16)
            else:
                p = _dot(hb_s[...], w_ref[:, oc])
                bf16_out[:, oc] = p.astype(BF16)
                for hs in range(HEADS_PER_MXU_TILE):
                    h = n * HEADS_PER_MXU_TILE + hs
                    f32_out[pl.ds(h, tm, stride=SB_HEADS), :] = p[:, hs * SB_HEAD_DIM:(hs + 1) * SB_HEAD_DIM]

    for n in range(gate_ref.shape[1] // MXU_COLS):
        oc = slice(n * MXU_COLS, (n + 1) * MXU_COLS)
        gate_ref[:, oc] = jax.nn.sigmoid(_dot(hb_s[...], wg_ref[:, oc]) + bg_ref[:, oc]).astype(BF16)


def _proj_call(x2d, gpre, win, wk, wv, lng, lnb, ws, bias, wg, bg, *, tm, chunk, seq_len, transposed_kv, emit_va):
    n_tok, d = x2d.shape
    gm = GM_GROUPS * GM_GROUP_DIM
    sb = SB_HEADS * SB_HEAD_DIM
    n_gate = wg.shape[1]

    def tok(width):
        return pl.BlockSpec((tm, width), lambda i: (i, 0))

    if transposed_kv:
        nt = seq_len // tm
        kv_spec = pl.BlockSpec((None, sb, tm), lambda i: (i // nt, 0, i % nt))
        kv_f32 = jax.ShapeDtypeStruct((n_tok // seq_len, sb, seq_len), F32)
        kv_bf16 = jax.ShapeDtypeStruct((n_tok // seq_len, sb, seq_len), BF16)
        kv_specs = [kv_spec] * 4
        kv_bytes = 2 * 2 * sb * tm * (4 + 2)
    else:
        hm_spec = pl.BlockSpec((tm * SB_HEADS, SB_HEAD_DIM), lambda i: (i, 0))
        kv_f32 = jax.ShapeDtypeStruct((n_tok * SB_HEADS, SB_HEAD_DIM), F32)
        kv_bf16 = jax.ShapeDtypeStruct((n_tok, sb), BF16)
        kv_specs = [hm_spec, hm_spec, tok(sb), tok(sb)]
        kv_bytes = 2 * 2 * tm * (SB_HEADS * LANES * 4 + sb * 2)
    out_specs = [tok(gm), tok(sb)] + kv_specs + [tok(n_gate)]
    out_shape = [jax.ShapeDtypeStruct((n_tok, gm), BF16), jax.ShapeDtypeStruct((n_tok, sb), BF16),
                 kv_f32, kv_f32, kv_bf16, kv_bf16, jax.ShapeDtypeStruct((n_tok, n_gate), BF16)]
    if emit_va:
        out_specs.append(tok(gm))
        out_shape.append(jax.ShapeDtypeStruct((n_tok, gm), F32))

    weight_bytes = 2 * (win.size + wk.size + wv.size + wg.size + ws.size) * 2
    block_bytes = 2 * tm * (d * 4 + gm * 2 + sb * 2 + n_gate * 2 + gm * 4) + kv_bytes
    scratch_bytes = tm * (d * 2 + gm * 4 + gm * 4 + gm * 2)
    return pl.pallas_call(
        functools.partial(_proj_kernel, tm=tm, chunk=chunk, transposed_kv=transposed_kv, emit_va=emit_va),
        grid=(n_tok // tm,),
        in_specs=[tok(d), _full(gpre.shape), _full(win.shape), _full(wk.shape), _full(wv.shape),
                  _full(lng.shape), _full(lnb.shape), _full(ws.shape), _full(bias.shape), _full(wg.shape),
                  _full(bg.shape)],
        out_specs=out_specs,
        out_shape=out_shape,
        scratch_shapes=[pltpu.VMEM((tm, d), BF16), pltpu.VMEM((tm, gm), F32),
                        pltpu.VMEM((tm, gm), F32), pltpu.VMEM((tm, gm), BF16)],
        compiler_params=_params(1, weight_bytes + block_bytes + 2 * scratch_bytes),
        name="proj",
    )(x2d, gpre, win, wk, wv, lng, lnb, ws, bias, wg, bg)


def _sb_tiles(n, z_fn, v_fn, v_transposed, tri, c_ref, acc_ref, mask):
    z, sp, cum, weight = ([None] * n for _ in range(4))

    def scores(h):
        z[h] = z_fn(h)

    def softplus(h):
        neg_abs = pltpu.bitcast(pltpu.bitcast(z[h], jnp.uint32) | jnp.uint32(SIGN_BIT), F32)
        s = jnp.maximum(z[h], 0.0) + jnp.log2(1.0 + jnp.exp2(neg_abs))
        sp[h] = s if mask is None else jnp.where(mask, s, 0.0)

    def cumsum(h):
        cum[h] = _dot(sp[h].astype(BF16), tri)

    def weights(h):
        total = sp[h] + cum[h] + c_ref[h]
        a = jnp.exp2(z[h] - total)
        if mask is not None:
            a = jnp.where(mask, a, 0.0)
        weight[h] = a.astype(BF16)
        c_ref[h] = total[:, 0:1]
        z[h] = sp[h] = cum[h] = None

    def values(h):
        v = v_fn(h)
        contrib = _dot_nt(weight[h], v) if v_transposed else _dot(weight[h], v)
        lanes = _pair_lanes(h)
        acc_ref[:, lanes] = acc_ref[:, lanes] + contrib
        weight[h] = None

    stages = (scores, softplus, cumsum, weights, values)
    for step in range(n + len(stages) - 1):
        for s, stage in enumerate(stages):
            if 0 <= step - s < n:
                stage(step - s)


def _pair_lanes(h):
    return slice(h // HEADS_PER_LANE_TILE * LANES, (h // HEADS_PER_LANE_TILE + 1) * LANES)


def _strict_tri(n):
    r = lax.broadcasted_iota(jnp.int32, (n, n), 0)
    s = lax.broadcasted_iota(jnp.int32, (n, n), 1)
    return jnp.where(r > s, 1.0, 0.0).astype(BF16)


def _causal_mask(n):
    r = lax.broadcasted_iota(jnp.int32, (n, n), 0)
    s = lax.broadcasted_iota(jnp.int32, (n, n), 1)
    return s < r


def _head_lane_masks():
    lane = lax.broadcasted_iota(jnp.int32, (1, LANES), 1)
    return [(lane >= hh * SB_HEAD_DIM) & (lane < (hh + 1) * SB_HEAD_DIM) for hh in range(HEADS_PER_LANE_TILE)]


def _head_row_masks():
    row = lax.broadcasted_iota(jnp.int32, (LANES, 1), 0)
    return [(row >= hh * SB_HEAD_DIM) & (row < (hh + 1) * SB_HEAD_DIM) for hh in range(HEADS_PER_LANE_TILE)]


def _masked_queries(q_ref, qm_s, heads):
    lane_masks = _head_lane_masks()
    for h in range(heads):
        lt = h // HEADS_PER_LANE_TILE
        qt = q_ref[:, lt * LANES:(lt + 1) * LANES]
        qm_s[h] = jnp.where(lane_masks[h % HEADS_PER_LANE_TILE], qt, jnp.zeros_like(qt))


def _sb_prompt_kernel(q_ref, kt_ref, vt_ref, o_ref, acc_s, c_s, u_s, qm_s, *, tq, heads):
    i = pl.program_id(2)
    u_s[...] = _strict_tri(tq)
    _masked_queries(q_ref, qm_s, heads)
    row_masks = _head_row_masks()
    acc_s[...] = jnp.zeros_like(acc_s)
    c_s[...] = jnp.zeros_like(c_s)

    def tile(j, mask):
        keys = pl.ds(pl.multiple_of(j * tq, tq), tq)

        def z_fn(h):
            return _dot(qm_s[h], kt_ref[_pair_lanes(h), keys])

        def v_fn(h):
            vt = vt_ref[_pair_lanes(h), keys]
            return jnp.where(row_masks[h % HEADS_PER_LANE_TILE], vt, jnp.zeros_like(vt))

        _sb_tiles(heads, z_fn, v_fn, True, u_s[...], c_s, acc_s, mask)

    def live():
        return (jnp.min(c_s[...]) < F32_EXP2_UNDERFLOW).astype(jnp.int32)

    tile(i, _causal_mask(tq))

    def cond(carry):
        n, go = carry
        return jnp.logical_and(n < i, go > 0)

    def body(carry):
        n, _ = carry
        tile(i - 1 - n, None)
        return n + 1, live()

    lax.while_loop(cond, body, (jnp.int32(0), live()))
    o_ref[...] = acc_s[...].astype(BF16)


def _sb_prompt_call(q, kt, vt, *, tq, heads):
    b, t, w = q.shape
    width = heads * SB_HEAD_DIM
    qspec = pl.BlockSpec((None, tq, width), lambda bi, hg, i: (bi, i, hg))
    kvspec = pl.BlockSpec((None, width, t), lambda bi, hg, i: (bi, hg, 0))
    vmem = (2 * 2 * t * width * 2 + 2 * 2 * tq * width * 2 + tq * width * 4 + heads * tq * LANES * 4
            + 2 * tq * tq * 2 + heads * tq * LANES * 2 + heads * 12 * tq * tq * 4)
    return pl.pallas_call(
        functools.partial(_sb_prompt_kernel, tq=tq, heads=heads),
        grid=(b, w // width, t // tq),
        in_specs=[qspec, kvspec, kvspec],
        out_specs=qspec,
        out_shape=jax.ShapeDtypeStruct((b, t, w), BF16),
        scratch_shapes=[pltpu.VMEM((tq, width), F32),
                        pltpu.VMEM((heads, tq, 1), F32),
                        pltpu.VMEM((tq, tq), BF16),
                        pltpu.VMEM((heads, tq, LANES), BF16)],
        compiler_params=_params(3, vmem),
        name="sb_prompt",
    )(q, kt, vt)


def _sb_sample_kernel(q_ref, kn_ref, vn_ref, kp_hbm, vp_hbm, o_ref, kbuf, vbuf, sem, acc_s, c_s, u_s, ud_s, qm_s,
                      *, tq, tk, n_past):
    b = pl.program_id(0)

    def copies(j, slot):
        keys = pl.ds(pl.multiple_of(j * tk, tk), tk)
        return (pltpu.make_async_copy(kp_hbm.at[b, :, keys], kbuf.at[slot], sem.at[0, slot]),
                pltpu.make_async_copy(vp_hbm.at[b, :, keys], vbuf.at[slot], sem.at[1, slot]))

    def start(j, slot):
        for cp in copies(j, slot):
            cp.start()

    def wait(j, slot):
        for cp in copies(j, slot):
            cp.wait()

    start(n_past - 1, 0)
    u_s[...] = _strict_tri(tk)
    ud_s[...] = _strict_tri(tq)
    _masked_queries(q_ref, qm_s, SB_HEADS)
    lane_masks = _head_lane_masks()
    row_masks = _head_row_masks()
    acc_s[...] = jnp.zeros_like(acc_s)
    c_s[...] = jnp.zeros_like(c_s)

    def live():
        return (jnp.min(c_s[...]) < F32_EXP2_UNDERFLOW).astype(jnp.int32)

    def z_new(h):
        return _dot_nt(qm_s[h], kn_ref[:, _pair_lanes(h)])

    def v_new(h):
        vn = vn_ref[:, _pair_lanes(h)]
        return jnp.where(lane_masks[h % HEADS_PER_LANE_TILE], vn, jnp.zeros_like(vn))

    _sb_tiles(SB_HEADS, z_new, v_new, False, ud_s[...], c_s, acc_s, _causal_mask(tq))

    def cond(carry):
        n, go = carry
        return jnp.logical_and(n < n_past, go > 0)

    def body(carry):
        n, _ = carry
        j = n_past - 1 - n
        slot = lax.rem(n, 2)
        wait(j, slot)

        @pl.when(n + 1 < n_past)
        def _():
            start(j - 1, 1 - slot)

        def z_past(h):
            return _dot(qm_s[h], kbuf[slot, _pair_lanes(h), :].astype(BF16))

        def v_past(h):
            vt = vbuf[slot, _pair_lanes(h), :].astype(BF16)
            return jnp.where(row_masks[h % HEADS_PER_LANE_TILE], vt, jnp.zeros_like(vt))

        _sb_tiles(SB_HEADS, z_past, v_past, True, u_s[...], c_s, acc_s, None)
        return n + 1, live()

    n_done, _ = lax.while_loop(cond, body, (jnp.int32(0), live()))

    @pl.when(n_done < n_past)
    def _():
        wait(n_past - 1 - n_done, lax.rem(n_done, 2))

    o_ref[...] = acc_s[...].astype(BF16)


def _sb_sample_call(q, k_new, v_new, kt_past, vt_past, *, tk):
    b, tq, w = q.shape
    p = kt_past.shape[2]
    newspec = pl.BlockSpec((None, tq, w), lambda bi: (bi, 0, 0))
    anyspec = pl.BlockSpec(memory_space=pl.ANY)
    vmem = (2 * 2 * w * tk * 4 + 2 * 4 * tq * w * 2 + tq * w * 4 + SB_HEADS * tq * LANES * 4
            + 2 * tk * tk * 2 + SB_HEADS * tq * LANES * 2 + SB_HEADS * 12 * tq * tk * 4)
    return pl.pallas_call(
        functools.partial(_sb_sample_kernel, tq=tq, tk=tk, n_past=p // tk),
        grid=(b,),
        in_specs=[newspec, newspec, newspec, anyspec, anyspec],
        out_specs=newspec,
        out_shape=jax.ShapeDtypeStruct((b, tq, w), BF16),
        scratch_shapes=[pltpu.VMEM((2, w, tk), F32),
                        pltpu.VMEM((2, w, tk), F32),
                        pltpu.SemaphoreType.DMA((2, 2)),
                        pltpu.VMEM((tq, w), F32),
                        pltpu.VMEM((SB_HEADS, tq, 1), F32),
                        pltpu.VMEM((tk, tk), BF16),
                        pltpu.VMEM((tq, tq), BF16),
                        pltpu.VMEM((SB_HEADS, tq, LANES), BF16)],
        compiler_params=_params(1, vmem),
        name="sb_sample",
    )(q, k_new, v_new, kt_past, vt_past)


def _merge_kernel(x_ref, ya_ref, yb_ref, gate_ref, wa_ref, wb_ref, wo_ref, gpost_ref, o_ref, m_s, p_s):
    d = x_ref.shape[1]
    for n in range(d // MXU_COLS):
        cols = slice(n * MXU_COLS, (n + 1) * MXU_COLS)
        gcols = slice(d + n * MXU_COLS, d + (n + 1) * MXU_COLS)
        merged = (gate_ref[:, cols].astype(F32) * _dot(ya_ref[...], wa_ref[:, cols])
                  + gate_ref[:, gcols].astype(F32) * _dot(yb_ref[...], wb_ref[:, cols]))
        m_s[:, cols] = merged.astype(BF16)
    for n in range(d // MXU_COLS):
        cols = slice(n * MXU_COLS, (n + 1) * MXU_COLS)
        p_s[:, cols] = _dot(m_s[...], wo_ref[:, cols])
    o_ref[...] = x_ref[...] + _rms(p_s[...], gpost_ref[...])


def _merge_call(x2d, ya, yb, gates, wa, wb, wo, gpost, *, tm):
    n_tok, d = x2d.shape

    def tok(width):
        return pl.BlockSpec((tm, width), lambda i: (i, 0))

    weight_bytes = 2 * (wa.size + wb.size + wo.size) * 2
    block_bytes = 2 * tm * (d * 4 + ya.shape[1] * 2 + yb.shape[1] * 2 + gates.shape[1] * 2 + d * 4)
    scratch_bytes = tm * d * 6
    return pl.pallas_call(
        _merge_kernel,
        grid=(n_tok // tm,),
        in_specs=[tok(d), tok(ya.shape[1]), tok(yb.shape[1]), tok(gates.shape[1]),
                  _full(wa.shape), _full(wb.shape), _full(wo.shape), _full(gpost.shape)],
        out_specs=tok(d),
        out_shape=jax.ShapeDtypeStruct((n_tok, d), F32),
        scratch_shapes=[pltpu.VMEM((tm, d), BF16), pltpu.VMEM((tm, d), F32)],
        compiler_params=_params(1, weight_bytes + block_bytes + 2 * scratch_bytes),
        name="merge",
    )(x2d, ya, yb, gates, wa, wb, wo, gpost)


def _ffn_kernel(x_ref, prev_ref, gpre_ref, wup_ref, wc_ref, bc_ref, wdown_ref, gpost_ref,
                y_ref, newconv_ref, xp_s, hn_s, carry_s, act_s, f_s, *, tm):
    assert CONV_WIDTH == 3
    dff = wdown_ref.shape[0]
    d = x_ref.shape[1]
    ng = tm // SUBLANES
    assert ng >= CONV_WIDTH - 1
    t = pl.program_id(1)

    @pl.when(t == 0)
    def _():
        carry_s[0:CONV_WIDTH - 1, :] = prev_ref[...]

    assert ng % SUBLANES == 0
    n_slabs = d // LANES

    def permuted_rows(r):
        return pl.ds((SUBLANES * r % ng) * SUBLANES + SUBLANES * r // ng, SUBLANES, stride=SUBLANES)

    for r in range(ng):
        for c in range(n_slabs):
            xp_s[c, permuted_rows(r), :] = x_ref[r * SUBLANES:(r + 1) * SUBLANES, c * LANES:(c + 1) * LANES]
    xp = jnp.concatenate([xp_s[c] for c in range(n_slabs)], axis=1)
    hn_s[...] = _rms(xp, gpre_ref[...]).astype(BF16)
    top = lax.broadcasted_iota(jnp.int32, (SUBLANES, MXU_COLS), 0) == 0

    for n in range(dff // MXU_COLS):
        halves = []
        for off in (0, dff):
            cols = slice(off + n * MXU_COLS, off + (n + 1) * MXU_COLS)
            up = _dot(hn_s[...], wup_ref[:, cols])
            back1 = jnp.where(top, carry_s[1:2, cols], pltpu.roll(up[tm - SUBLANES:tm], 1, axis=0))
            back2 = jnp.where(top, carry_s[0:1, cols],
                              pltpu.roll(up[tm - 2 * SUBLANES:tm - SUBLANES], 1, axis=0))
            carry_s[0:1, cols] = up[tm - SUBLANES - 1:tm - SUBLANES]
            carry_s[1:2, cols] = up[tm - 1:tm]
            prev1 = jnp.concatenate([back1, up[:tm - SUBLANES]], axis=0)
            prev2 = jnp.concatenate([back2, back1, up[:tm - 2 * SUBLANES]], axis=0)
            halves.append(bc_ref[:, cols] + wc_ref[0:1, cols] * prev2 + wc_ref[1:2, cols] * prev1
                          + wc_ref[2:3, cols] * up)
        act_s[:, n * MXU_COLS:(n + 1) * MXU_COLS] = (jax.nn.gelu(halves[0]) * halves[1]).astype(BF16)
    newconv_ref[...] = carry_s[0:CONV_WIDTH - 1, :]
    for n in range(d // MXU_COLS):
        cols = slice(n * MXU_COLS, (n + 1) * MXU_COLS)
        f_s[:, cols] = _dot(act_s[...], wdown_ref[:, cols])
    normed = _rms(f_s[...], gpost_ref[...])
    for c in range(n_slabs):
        xp_s[c] = xp_s[c] + normed[:, c * LANES:(c + 1) * LANES]
    for r in range(ng):
        for c in range(n_slabs):
            y_ref[r * SUBLANES:(r + 1) * SUBLANES, c * LANES:(c + 1) * LANES] = xp_s[c, permuted_rows(r), :]


def _ffn_call(x, conv_prev, gpre, wup, wc, bc, wdown, gpost, *, tm):
    b, t, d = x.shape
    dff = wdown.shape[0]
    xspec = pl.BlockSpec((None, tm, d), lambda bi, ti: (bi, ti, 0))
    cspec = pl.BlockSpec((None, CONV_WIDTH - 1, 2 * dff), lambda bi, ti: (bi, 0, 0))
    weight_bytes = 2 * (wup.size + wdown.size) * 2
    block_bytes = 2 * 2 * tm * d * 4
    scratch_bytes = tm * d * 4 + tm * d * 2 + SUBLANES * 2 * dff * 4 + tm * dff * 2 + tm * d * 4
    return pl.pallas_call(
        functools.partial(_ffn_kernel, tm=tm),
        grid=(b, t // tm),
        in_specs=[xspec, cspec, _full(gpre.shape), _full(wup.shape), _full(wc.shape), _full(bc.shape),
                  _full(wdown.shape), _full(gpost.shape)],
        out_specs=[xspec, cspec],
        out_shape=[jax.ShapeDtypeStruct((b, t, d), F32),
                   jax.ShapeDtypeStruct((b, CONV_WIDTH - 1, 2 * dff), F32)],
        scratch_shapes=[pltpu.VMEM((d // LANES, tm, LANES), F32),
                        pltpu.VMEM((tm, d), BF16),
                        pltpu.VMEM((SUBLANES, 2 * dff), F32),
                        pltpu.VMEM((tm, dff), BF16),
                        pltpu.VMEM((tm, d), F32)],
        compiler_params=_params(2, weight_bytes + block_bytes + 2 * scratch_bytes),
        name="ffn",
    )(x, conv_prev, gpre, wup, wc, bc, wdown, gpost)


def _layer(x, past_k, past_v, conv_prev, w):
    b, t, d = x.shape
    gm = GM_GROUPS * GM_GROUP_DIM
    sbw = SB_HEADS * SB_HEAD_DIM
    chunk = min(t, GM_CHUNK)
    n_tok = b * t
    decode = past_k is not None
    tm = min(256, t)
    assert t % tm == 0 and tm % chunk == 0, (t, tm, chunk)

    tri = jnp.tril(jnp.ones((chunk, chunk), dtype=bool))
    ws = jnp.where(tri[None], w["w_s"][:, :chunk, :chunk], 0.0).astype(BF16)
    bias = jnp.repeat(w["b_s"][:, :chunk].T, GM_GROUP_DIM, axis=1)
    w_in = w["w_in"]
    w_uvq = w_in[:, :2 * gm + sbw]
    w_k = w_in[:, 2 * gm + sbw:2 * gm + 2 * sbw]
    w_v = w_in[:, 2 * gm + 2 * sbw:]
    if not decode:
        w_k, w_v = w_k.T, w_v.T

    outs = _proj_call(
        x.reshape(n_tok, d), w["g_pre_mix"], w_uvq, w_k, w_v, w["ln_v_g"], w["ln_v_b"], ws, bias,
        w["w_gate"], w["b_gate"], tm=tm, chunk=chunk, seq_len=t, transposed_kv=not decode, emit_va=decode)
    ya, q, k_out, v_out, kb, vb, gates = outs[:7]
    q3 = q.reshape(b, t, sbw)

    if decode:
        p = past_k.shape[1]
        kt_past = jnp.transpose(past_k, (0, 2, 3, 1)).reshape(b, sbw, p)
        vt_past = jnp.transpose(past_v, (0, 2, 3, 1)).reshape(b, sbw, p)
        yb = _sb_sample_call(q3, kb.reshape(b, t, sbw), vb.reshape(b, t, sbw), kt_past, vt_past, tk=min(p, 256))
        k4 = k_out.reshape(b, t, SB_HEADS, SB_HEAD_DIM)
        v4 = v_out.reshape(b, t, SB_HEADS, SB_HEAD_DIM)
        va = outs[7].reshape(b, t, gm)
    else:
        yb = _sb_prompt_call(q3, kb, vb, tq=tm, heads=SB_PROMPT_HEADS_PER_STEP)
        k4 = jnp.transpose(k_out.reshape(b, SB_HEADS, SB_HEAD_DIM, t), (0, 3, 1, 2))
        v4 = jnp.transpose(v_out.reshape(b, SB_HEADS, SB_HEAD_DIM, t), (0, 3, 1, 2))
        va = None

    x1 = _merge_call(x.reshape(n_tok, d), ya, yb.reshape(n_tok, sbw), gates,
                     w["w_branch_a"], w["w_branch_b"], w["w_o"], w["g_post_mix"], tm=tm)
    y, conv_new = _ffn_call(x1.reshape(b, t, d), conv_prev, w["g_pre_ffn"], w["w_up"], w["w_conv"],
                            w["b_conv"], w["w_down"], w["g_post_ffn"], tm=tm)
    return y, k4, v4, conv_new, va


def kernel(x_prompt, x_sample, cache_sb_k, cache_sb_v, state_ffn_conv, g_pre_mix, w_in, ln_v_g, ln_v_b, w_s, b_s,
           w_gate, b_gate, w_branch_a, w_branch_b, w_o, g_post_mix, g_pre_ffn, w_up, w_conv, b_conv, w_down,
           g_post_ffn):
    depth = w_in.shape[0]
    y_p, y_s = x_prompt, x_sample
    outs = [[] for _ in range(7)]
    for l in range(depth):
        row = lambda a: a[l][None, :]
        w = dict(
            g_pre_mix=row(g_pre_mix), w_in=w_in[l].astype(BF16), ln_v_g=row(ln_v_g), ln_v_b=row(ln_v_b),
            w_s=w_s[l], b_s=b_s[l], w_gate=w_gate[l].astype(BF16), b_gate=row(b_gate),
            w_branch_a=w_branch_a[l].astype(BF16), w_branch_b=w_branch_b[l].astype(BF16),
            w_o=w_o[l].astype(BF16), g_post_mix=row(g_post_mix), g_pre_ffn=row(g_pre_ffn),
            w_up=w_up[l].astype(BF16), w_conv=w_conv[l], b_conv=row(b_conv),
            w_down=w_down[l].astype(BF16), g_post_ffn=row(g_post_ffn))
        zero_conv = jnp.zeros((y_p.shape[0], CONV_WIDTH - 1, w_up.shape[2]), y_p.dtype)
        y_p, kp, vp, cp, _ = _layer(y_p, None, None, zero_conv, w)
        y_s, ks, vs, cs, gvs = _layer(y_s, cache_sb_k[l], cache_sb_v[l], state_ffn_conv[l], w)
        for lst, a in zip(outs, (kp, vp, cp, ks, vs, cs, gvs)):
            lst.append(a)
    return (y_p, y_s) + tuple(jnp.stack(o) for o in outs)
```

```python
import functools
import math

import jax
import jax.numpy as jnp
from jax import lax
from jax.experimental import pallas as pl
from jax.experimental.pallas import tpu as pltpu

F32 = jnp.float32
BF16 = jnp.bfloat16

EPS = 1e-6
SB_HEADS = 16
SB_HEAD_DIM = 64
GM_GROUPS = 8
GM_GROUP_DIM = 128
GM_CHUNK = 128
CONV_WIDTH = 3

LANES = 128
SUBLANES = 8
MXU_COLS = 256
HEADS_PER_LANE_TILE = LANES // SB_HEAD_DIM
HEADS_PER_MXU_TILE = MXU_COLS // SB_HEAD_DIM
SUB_BLOCKS_PER_STEP = 2
SB_PROMPT_HEADS_PER_STEP = 16
V7X_SCOPED_VMEM_BYTES = 60000 * 1024

LOG2E = math.log2(math.e)
F32_EXP2_UNDERFLOW = 150.0
SIGN_BIT = 0x80000000

_ARB = "arbitrary"
_NT = (((1,), (1,)), ((), ()))


def _rms(x, g):
    return x * lax.rsqrt(jnp.mean(x * x, axis=-1, keepdims=True) + EPS) * g


def _dot(a, b):
    return jnp.dot(a, b, preferred_element_type=F32)


def _dot_nt(a, b):
    return lax.dot_general(a, b, _NT, preferred_element_type=F32)


def _params(n_grid, vmem_bytes):
    return pltpu.CompilerParams(
        dimension_semantics=(_ARB,) * n_grid,
        vmem_limit_bytes=int(min(vmem_bytes, V7X_SCOPED_VMEM_BYTES)))


def _full(shape):
    nd = len(shape)
    return pl.BlockSpec(shape, lambda *_: (0,) * nd, pipeline_mode=pl.Buffered(1))


def _proj_kernel(x_ref, gpre_ref, win_ref, wk_ref, wv_ref, lng_ref, lnb_ref, ws_ref, bias_ref, wg_ref, bg_ref,
                 ya_ref, q_ref, k_ref, v_ref, kb_ref, vb_ref, gate_ref, *rest,
                 sub, n_sub, chunk, transposed_kv, emit_va):
    if emit_va:
        va_ref, hb_s, u_s, gv_s, vab_s = rest
    else:
        hb_s, u_s, gv_s, vab_s = rest
    gm = GM_GROUPS * GM_GROUP_DIM
    sb = SB_HEADS * SB_HEAD_DIM
    qscale = SB_HEAD_DIM ** -0.5 * LOG2E
    nch = sub // chunk

    def stages_of(i):
        base = i * sub
        tok = slice(base, base + sub)

        def norm():
            hb_s[i] = _rms(x_ref[tok, :], gpre_ref[...]).astype(BF16)

        def proj(col0, n):
            cols = slice(col0 + n * MXU_COLS, col0 + (n + 1) * MXU_COLS)
            return _dot(hb_s[i], win_ref[:, cols])

        def gating_inputs(n):
            oc = slice(n * MXU_COLS, (n + 1) * MXU_COLS)
            u_s[i, :, oc] = jax.nn.gelu(proj(0, n))
            gv_s[i, :, oc] = jax.nn.gelu(proj(gm, n))

        def layer_norm():
            gv = gv_s[i]
            mu = jnp.mean(gv, axis=-1, keepdims=True)
            dev = gv - mu
            var = jnp.mean(dev * dev, axis=-1, keepdims=True)
            va = dev * lax.rsqrt(var + EPS) * lng_ref[...] + lnb_ref[...]
            if emit_va:
                va_ref[tok, :] = va
            vab_s[i] = va.astype(BF16)

        def spatial_gating(g):
            cols = slice(g * GM_GROUP_DIM, (g + 1) * GM_GROUP_DIM)
            pieces = [vab_s[i, c * chunk:(c + 1) * chunk, cols] for c in range(nch)]
            mixed = _dot(ws_ref[g], pieces[0] if nch == 1 else jnp.concatenate(pieces, axis=1))
            for c in range(nch):
                rows = slice(c * chunk, (c + 1) * chunk)
                m = mixed[:, c * GM_GROUP_DIM:(c + 1) * GM_GROUP_DIM] + bias_ref[:, cols]
                ya_ref[base + c * chunk:base + (c + 1) * chunk, cols] = (u_s[i, rows, cols] * m).astype(BF16)

        def queries(n):
            q_ref[tok, n * MXU_COLS:(n + 1) * MXU_COLS] = (proj(2 * gm, n) * qscale).astype(BF16)

        def keys_or_values(w_ref, f32_out, bf16_out, n):
            oc = slice(n * MXU_COLS, (n + 1) * MXU_COLS)
            if transposed_kv:
                pt = _dot_nt(w_ref[oc, :], hb_s[i])
                f32_out[oc, tok] = pt
                bf16_out[oc, tok] = pt.astype(BF16)
            else:
                p = _dot(hb_s[i], w_ref[:, oc])
                bf16_out[tok, oc] = p.astype(BF16)
                for hs in range(HEADS_PER_MXU_TILE):
                    h = n * HEADS_PER_MXU_TILE + hs
                    f32_out[pl.ds(base * SB_HEADS + h, sub, stride=SB_HEADS), :] = (
                        p[:, hs * SB_HEAD_DIM:(hs + 1) * SB_HEAD_DIM])

        def gates(n):
            oc = slice(n * MXU_COLS, (n + 1) * MXU_COLS)
            gate_ref[tok, oc] = jax.nn.sigmoid(_dot(hb_s[i], wg_ref[:, oc]) + bg_ref[:, oc]).astype(BF16)

        part = functools.partial
        n_sb = sb // MXU_COLS
        return ([norm] + [part(gating_inputs, n) for n in range(gm // MXU_COLS)]
                + [part(queries, n) for n in range(n_sb)] + [layer_norm]
                + [part(keys_or_values, wk_ref, k_ref, kb_ref, n) for n in range(n_sb)]
                + [part(spatial_gating, g) for g in range(GM_GROUPS)]
                + [part(keys_or_values, wv_ref, v_ref, vb_ref, n) for n in range(n_sb)]
                + [part(gates, n) for n in range(gate_ref.shape[1] // MXU_COLS)])

    _skewed([stages_of(i) for i in range(n_sub)], skew=2)


def _proj_call(x2d, gpre, win, wk, wv, lng, lnb, ws, bias, wg, bg, *, sub, n_sub, chunk, seq_len, transposed_kv,
               emit_va):
    n_tok, d = x2d.shape
    gm = GM_GROUPS * GM_GROUP_DIM
    sb = SB_HEADS * SB_HEAD_DIM
    n_gate = wg.shape[1]
    tm = sub * n_sub

    def tok(width):
        return pl.BlockSpec((tm, width), lambda i: (i, 0))

    if transposed_kv:
        nt = seq_len // tm
        kv_spec = pl.BlockSpec((None, sb, tm), lambda i: (i // nt, 0, i % nt))
        kv_f32 = jax.ShapeDtypeStruct((n_tok // seq_len, sb, seq_len), F32)
        kv_bf16 = jax.ShapeDtypeStruct((n_tok // seq_len, sb, seq_len), BF16)
        kv_specs = [kv_spec] * 4
        kv_bytes = 2 * 2 * sb * tm * (4 + 2)
    else:
        hm_spec = pl.BlockSpec((tm * SB_HEADS, SB_HEAD_DIM), lambda i: (i, 0))
        kv_f32 = jax.ShapeDtypeStruct((n_tok * SB_HEADS, SB_HEAD_DIM), F32)
        kv_bf16 = jax.ShapeDtypeStruct((n_tok, sb), BF16)
        kv_specs = [hm_spec, hm_spec, tok(sb), tok(sb)]
        kv_bytes = 2 * 2 * tm * (SB_HEADS * LANES * 4 + sb * 2)
    out_specs = [tok(gm), tok(sb)] + kv_specs + [tok(n_gate)]
    out_shape = [jax.ShapeDtypeStruct((n_tok, gm), BF16), jax.ShapeDtypeStruct((n_tok, sb), BF16),
                 kv_f32, kv_f32, kv_bf16, kv_bf16, jax.ShapeDtypeStruct((n_tok, n_gate), BF16)]
    if emit_va:
        out_specs.append(tok(gm))
        out_shape.append(jax.ShapeDtypeStruct((n_tok, gm), F32))

    weight_bytes = (win.size + wk.size + wv.size + wg.size + ws.size) * 2
    block_bytes = 2 * tm * (d * 4 + gm * 2 + sb * 2 + n_gate * 2 + gm * 4) + kv_bytes
    scratch_bytes = tm * (d * 2 + gm * 4 + gm * 4 + gm * 2)
    return pl.pallas_call(
        functools.partial(_proj_kernel, sub=sub, n_sub=n_sub, chunk=chunk, transposed_kv=transposed_kv,
                          emit_va=emit_va),
        grid=(n_tok // tm,),
        in_specs=[tok(d), _full(gpre.shape), _full(win.shape), _full(wk.shape), _full(wv.shape),
                  _full(lng.shape), _full(lnb.shape), _full(ws.shape), _full(bias.shape), _full(wg.shape),
                  _full(bg.shape)],
        out_specs=out_specs,
        out_shape=out_shape,
        scratch_shapes=[pltpu.VMEM((n_sub, sub, d), BF16), pltpu.VMEM((n_sub, sub, gm), F32),
                        pltpu.VMEM((n_sub, sub, gm), F32), pltpu.VMEM((n_sub, sub, gm), BF16)],
        compiler_params=_params(1, weight_bytes + block_bytes + 2 * scratch_bytes),
        name="proj",
    )(x2d, gpre, win, wk, wv, lng, lnb, ws, bias, wg, bg)


def _sb_tiles(n, z_fn, v_fn, v_transposed, tri, c_ref, acc_ref, mask):
    z, sp, cum, weight = ([None] * n for _ in range(4))

    def scores(h):
        z[h] = z_fn(h)

    def softplus(h):
        neg_abs = pltpu.bitcast(pltpu.bitcast(z[h], jnp.uint32) | jnp.uint32(SIGN_BIT), F32)
        s = jnp.maximum(z[h], 0.0) + jnp.log2(1.0 + jnp.exp2(neg_abs))
        sp[h] = s if mask is None else jnp.where(mask, s, 0.0)

    def cumsum(h):
        cum[h] = _dot(sp[h].astype(BF16), tri)

    def weights(h):
        total = sp[h] + cum[h] + c_ref[h]
        a = jnp.exp2(z[h] - total)
        if mask is not None:
            a = jnp.where(mask, a, 0.0)
        weight[h] = a.astype(BF16)
        c_ref[h] = total[:, 0:1]
        z[h] = sp[h] = cum[h] = None

    def values(h):
        v = v_fn(h)
        contrib = _dot_nt(weight[h], v) if v_transposed else _dot(weight[h], v)
        lanes = _pair_lanes(h)
        acc_ref[:, lanes] = acc_ref[:, lanes] + contrib
        weight[h] = None

    stages = (scores, softplus, cumsum, weights, values)
    for step in range(n + len(stages) - 1):
        for s, stage in enumerate(stages):
            if 0 <= step - s < n:
                stage(step - s)


def _pair_lanes(h):
    return slice(h // HEADS_PER_LANE_TILE * LANES, (h // HEADS_PER_LANE_TILE + 1) * LANES)


def _strict_tri(n):
    r = lax.broadcasted_iota(jnp.int32, (n, n), 0)
    s = lax.broadcasted_iota(jnp.int32, (n, n), 1)
    return jnp.where(r > s, 1.0, 0.0).astype(BF16)


def _causal_mask(n):
    r = lax.broadcasted_iota(jnp.int32, (n, n), 0)
    s = lax.broadcasted_iota(jnp.int32, (n, n), 1)
    return s < r


def _head_lane_masks():
    lane = lax.broadcasted_iota(jnp.int32, (1, LANES), 1)
    return [(lane >= hh * SB_HEAD_DIM) & (lane < (hh + 1) * SB_HEAD_DIM) for hh in range(HEADS_PER_LANE_TILE)]


def _head_row_masks():
    row = lax.broadcasted_iota(jnp.int32, (LANES, 1), 0)
    return [(row >= hh * SB_HEAD_DIM) & (row < (hh + 1) * SB_HEAD_DIM) for hh in range(HEADS_PER_LANE_TILE)]


def _masked_queries(q_ref, qm_s, heads):
    lane_masks = _head_lane_masks()
    for h in range(heads):
        lt = h // HEADS_PER_LANE_TILE
        qt = q_ref[:, lt * LANES:(lt + 1) * LANES]
        qm_s[h] = jnp.where(lane_masks[h % HEADS_PER_LANE_TILE], qt, jnp.zeros_like(qt))


def _sb_prompt_kernel(q_ref, kt_ref, vt_ref, o_ref, acc_s, c_s, u_s, qm_s, *, tq, heads):
    i = pl.program_id(2)
    u_s[...] = _strict_tri(tq)
    _masked_queries(q_ref, qm_s, heads)
    row_masks = _head_row_masks()
    acc_s[...] = jnp.zeros_like(acc_s)
    c_s[...] = jnp.zeros_like(c_s)

    def tile(j, mask):
        keys = pl.ds(pl.multiple_of(j * tq, tq), tq)

        def z_fn(h):
            return _dot(qm_s[h], kt_ref[_pair_lanes(h), keys])

        def v_fn(h):
            vt = vt_ref[_pair_lanes(h), keys]
            return jnp.where(row_masks[h % HEADS_PER_LANE_TILE], vt, jnp.zeros_like(vt))

        _sb_tiles(heads, z_fn, v_fn, True, u_s[...], c_s, acc_s, mask)

    def live():
        return (jnp.min(c_s[...]) < F32_EXP2_UNDERFLOW).astype(jnp.int32)

    tile(i, _causal_mask(tq))

    def cond(carry):
        n, go = carry
        return jnp.logical_and(n < i, go > 0)

    def body(carry):
        n, _ = carry
        tile(i - 1 - n, None)
        return n + 1, live()

    lax.while_loop(cond, body, (jnp.int32(0), live()))
    o_ref[...] = acc_s[...].astype(BF16)


def _sb_prompt_call(q, kt, vt, *, tq, heads):
    b, t, w = q.shape
    width = heads * SB_HEAD_DIM
    qspec = pl.BlockSpec((None, tq, width), lambda bi, hg, i: (bi, i, hg))
    kvspec = pl.BlockSpec((None, width, t), lambda bi, hg, i: (bi, hg, 0))
    vmem = (2 * 2 * t * width * 2 + 2 * 2 * tq * width * 2 + tq * width * 4 + heads * tq * LANES * 4
            + 2 * tq * tq * 2 + heads * tq * LANES * 2 + heads * 12 * tq * tq * 4)
    return pl.pallas_call(
        functools.partial(_sb_prompt_kernel, tq=tq, heads=heads),
        grid=(b, w // width, t // tq),
        in_specs=[qspec, kvspec, kvspec],
        out_specs=qspec,
        out_shape=jax.ShapeDtypeStruct((b, t, w), BF16),
        scratch_shapes=[pltpu.VMEM((tq, width), F32),
                        pltpu.VMEM((heads, tq, 1), F32),
                        pltpu.VMEM((tq, tq), BF16),
                        pltpu.VMEM((heads, tq, LANES), BF16)],
        compiler_params=_params(3, vmem),
        name="sb_prompt",
    )(q, kt, vt)


def _sb_sample_kernel(q_ref, kn_ref, vn_ref, kp_hbm, vp_hbm, o_ref, kbuf, vbuf, sem, acc_s, c_s, u_s, ud_s, qm_s,
                      *, tq, tk, n_past):
    b = pl.program_id(0)

    def copies(j, slot):
        keys = pl.ds(pl.multiple_of(j * tk, tk), tk)
        return (pltpu.make_async_copy(kp_hbm.at[b, :, keys], kbuf.at[slot], sem.at[0, slot]),
                pltpu.make_async_copy(vp_hbm.at[b, :, keys], vbuf.at[slot], sem.at[1, slot]))

    def start(j, slot):
        for cp in copies(j, slot):
            cp.start()

    def wait(j, slot):
        for cp in copies(j, slot):
            cp.wait()

    start(n_past - 1, 0)
    u_s[...] = _strict_tri(tk)
    ud_s[...] = _strict_tri(tq)
    _masked_queries(q_ref, qm_s, SB_HEADS)
    lane_masks = _head_lane_masks()
    row_masks = _head_row_masks()
    acc_s[...] = jnp.zeros_like(acc_s)
    c_s[...] = jnp.zeros_like(c_s)

    def live():
        return (jnp.min(c_s[...]) < F32_EXP2_UNDERFLOW).astype(jnp.int32)

    def z_new(h):
        return _dot_nt(qm_s[h], kn_ref[:, _pair_lanes(h)])

    def v_new(h):
        vn = vn_ref[:, _pair_lanes(h)]
        return jnp.where(lane_masks[h % HEADS_PER_LANE_TILE], vn, jnp.zeros_like(vn))

    _sb_tiles(SB_HEADS, z_new, v_new, False, ud_s[...], c_s, acc_s, _causal_mask(tq))

    def cond(carry):
        n, go = carry
        return jnp.logical_and(n < n_past, go > 0)

    def body(carry):
        n, _ = carry
        j = n_past - 1 - n
        slot = lax.rem(n, 2)
        wait(j, slot)

        @pl.when(n + 1 < n_past)
        def _():
            start(j - 1, 1 - slot)

        def z_past(h):
            return _dot(qm_s[h], kbuf[slot, _pair_lanes(h), :].astype(BF16))

        def v_past(h):
            vt = vbuf[slot, _pair_lanes(h), :].astype(BF16)
            return jnp.where(row_masks[h % HEADS_PER_LANE_TILE], vt, jnp.zeros_like(vt))

        _sb_tiles(SB_HEADS, z_past, v_past, True, u_s[...], c_s, acc_s, None)
        return n + 1, live()

    n_done, _ = lax.while_loop(cond, body, (jnp.int32(0), live()))

    @pl.when(n_done < n_past)
    def _():
        wait(n_past - 1 - n_done, lax.rem(n_done, 2))

    o_ref[...] = acc_s[...].astype(BF16)


def _sb_sample_call(q, k_new, v_new, kt_past, vt_past, *, tk):
    b, tq, w = q.shape
    p = kt_past.shape[2]
    newspec = pl.BlockSpec((None, tq, w), lambda bi: (bi, 0, 0))
    anyspec = pl.BlockSpec(memory_space=pl.ANY)
    vmem = (2 * 2 * w * tk * 4 + 2 * 4 * tq * w * 2 + tq * w * 4 + SB_HEADS * tq * LANES * 4
            + 2 * tk * tk * 2 + SB_HEADS * tq * LANES * 2 + SB_HEADS * 12 * tq * tk * 4)
    return pl.pallas_call(
        functools.partial(_sb_sample_kernel, tq=tq, tk=tk, n_past=p // tk),
        grid=(b,),
        in_specs=[newspec, newspec, newspec, anyspec, anyspec],
        out_specs=newspec,
        out_shape=jax.ShapeDtypeStruct((b, tq, w), BF16),
        scratch_shapes=[pltpu.VMEM((2, w, tk), F32),
                        pltpu.VMEM((2, w, tk), F32),
                        pltpu.SemaphoreType.DMA((2, 2)),
                        pltpu.VMEM((tq, w), F32),
                        pltpu.VMEM((SB_HEADS, tq, 1), F32),
                        pltpu.VMEM((tk, tk), BF16),
                        pltpu.VMEM((tq, tq), BF16),
                        pltpu.VMEM((SB_HEADS, tq, LANES), BF16)],
        compiler_params=_params(1, vmem),
        name="sb_sample",
    )(q, k_new, v_new, kt_past, vt_past)


def _merge_kernel(x_ref, ya_ref, yb_ref, gate_ref, wa_ref, wb_ref, wo_ref, gpost_ref, o_ref, m_s, p_s,
                  *, sub, n_sub):
    d = x_ref.shape[1]

    def stages_of(i):
        tok = slice(i * sub, (i + 1) * sub)

        def merge(n):
            cols = slice(n * MXU_COLS, (n + 1) * MXU_COLS)
            gcols = slice(d + n * MXU_COLS, d + (n + 1) * MXU_COLS)
            merged = (gate_ref[tok, cols].astype(F32) * _dot(ya_ref[tok, :], wa_ref[:, cols])
                      + gate_ref[tok, gcols].astype(F32) * _dot(yb_ref[tok, :], wb_ref[:, cols]))
            m_s[i, :, cols] = merged.astype(BF16)

        def project(n):
            cols = slice(n * MXU_COLS, (n + 1) * MXU_COLS)
            p_s[i, :, cols] = _dot(m_s[i], wo_ref[:, cols])

        def norm_and_store():
            o_ref[tok, :] = x_ref[tok, :] + _rms(p_s[i], gpost_ref[...])

        n_chunks = d // MXU_COLS
        return ([functools.partial(merge, n) for n in range(n_chunks)]
                + [functools.partial(project, n) for n in range(n_chunks)] + [norm_and_store])

    _skewed([stages_of(i) for i in range(n_sub)], skew=2)


def _merge_call(x2d, ya, yb, gates, wa, wb, wo, gpost, *, sub, n_sub):
    n_tok, d = x2d.shape
    tm = sub * n_sub

    def tok(width):
        return pl.BlockSpec((tm, width), lambda i: (i, 0))

    weight_bytes = (wa.size + wb.size + wo.size) * 2
    block_bytes = 2 * tm * (d * 4 + ya.shape[1] * 2 + yb.shape[1] * 2 + gates.shape[1] * 2 + d * 4)
    scratch_bytes = tm * d * 6
    return pl.pallas_call(
        functools.partial(_merge_kernel, sub=sub, n_sub=n_sub),
        grid=(n_tok // tm,),
        in_specs=[tok(d), tok(ya.shape[1]), tok(yb.shape[1]), tok(gates.shape[1]),
                  _full(wa.shape), _full(wb.shape), _full(wo.shape), _full(gpost.shape)],
        out_specs=tok(d),
        out_shape=jax.ShapeDtypeStruct((n_tok, d), F32),
        scratch_shapes=[pltpu.VMEM((n_sub, sub, d), BF16), pltpu.VMEM((n_sub, sub, d), F32)],
        compiler_params=_params(1, weight_bytes + block_bytes + 2 * scratch_bytes),
        name="merge",
    )(x2d, ya, yb, gates, wa, wb, wo, gpost)


def _skewed(stage_lists, skew):
    n_stages = len(stage_lists[0])
    for step in range(n_stages + skew * (len(stage_lists) - 1)):
        for i, stages in enumerate(stage_lists):
            if 0 <= step - i * skew < n_stages:
                stages[step - i * skew]()


def _ffn_kernel(x_ref, prev_ref, gpre_ref, wup_ref, wc_ref, bc_ref, wdown_ref, gpost_ref,
                y_ref, newconv_ref, xp_s, hn_s, carry_s, act_s, f_s, *, sub, n_sub):
    assert CONV_WIDTH == 3
    dff = wdown_ref.shape[0]
    d = x_ref.shape[1]
    ng = sub // SUBLANES
    assert ng >= CONV_WIDTH - 1 and ng % SUBLANES == 0
    n_slabs = d // LANES
    t = pl.program_id(1)

    @pl.when(t == 0)
    def _():
        carry_s[0:CONV_WIDTH - 1, :] = prev_ref[...]

    top = lax.broadcasted_iota(jnp.int32, (SUBLANES, MXU_COLS), 0) == 0

    def permuted_rows(r):
        return pl.ds((SUBLANES * r % ng) * SUBLANES + SUBLANES * r // ng, SUBLANES, stride=SUBLANES)

    def stages_of(i):
        base = i * sub

        def load_and_norm():
            for r in range(ng):
                rows = slice(base + r * SUBLANES, base + (r + 1) * SUBLANES)
                for c in range(n_slabs):
                    xp_s[i, c, permuted_rows(r), :] = x_ref[rows, c * LANES:(c + 1) * LANES]
            xp = jnp.concatenate([xp_s[i, c] for c in range(n_slabs)], axis=1)
            hn_s[i] = _rms(xp, gpre_ref[...]).astype(BF16)

        def up_conv_act(n):
            halves = []
            for off in (0, dff):
                cols = slice(off + n * MXU_COLS, off + (n + 1) * MXU_COLS)
                up = _dot(hn_s[i], wup_ref[:, cols])
                back1 = jnp.where(top, carry_s[1:2, cols], pltpu.roll(up[sub - SUBLANES:sub], 1, axis=0))
                back2 = jnp.where(top, carry_s[0:1, cols],
                                  pltpu.roll(up[sub - 2 * SUBLANES:sub - SUBLANES], 1, axis=0))
                carry_s[0:1, cols] = up[sub - SUBLANES - 1:sub - SUBLANES]
                carry_s[1:2, cols] = up[sub - 1:sub]
                prev1 = jnp.concatenate([back1, up[:sub - SUBLANES]], axis=0)
                prev2 = jnp.concatenate([back2, back1, up[:sub - 2 * SUBLANES]], axis=0)
                halves.append(bc_ref[:, cols] + wc_ref[0:1, cols] * prev2 + wc_ref[1:2, cols] * prev1
                              + wc_ref[2:3, cols] * up)
            act_s[i, :, n * MXU_COLS:(n + 1) * MXU_COLS] = (jax.nn.gelu(halves[0]) * halves[1]).astype(BF16)

        def down(n):
            cols = slice(n * MXU_COLS, (n + 1) * MXU_COLS)
            f_s[i, :, cols] = _dot(act_s[i], wdown_ref[:, cols])

        def norm_and_store():
            normed = _rms(f_s[i], gpost_ref[...])
            for c in range(n_slabs):
                xp_s[i, c] = xp_s[i, c] + normed[:, c * LANES:(c + 1) * LANES]
            for r in range(ng):
                rows = slice(base + r * SUBLANES, base + (r + 1) * SUBLANES)
                for c in range(n_slabs):
                    y_ref[rows, c * LANES:(c + 1) * LANES] = xp_s[i, c, permuted_rows(r), :]
            if i == n_sub - 1:
                newconv_ref[...] = carry_s[0:CONV_WIDTH - 1, :]

        return ([load_and_norm] + [functools.partial(up_conv_act, n) for n in range(dff // MXU_COLS)]
                + [functools.partial(down, n) for n in range(d // MXU_COLS)] + [norm_and_store])

    _skewed([stages_of(i) for i in range(n_sub)], skew=2)


def _ffn_call(x, conv_prev, gpre, wup, wc, bc, wdown, gpost, *, sub, n_sub):
    b, t, d = x.shape
    dff = wdown.shape[0]
    tm = sub * n_sub
    xspec = pl.BlockSpec((None, tm, d), lambda bi, ti: (bi, ti, 0))
    cspec = pl.BlockSpec((None, CONV_WIDTH - 1, 2 * dff), lambda bi, ti: (bi, 0, 0))
    weight_bytes = (wup.size + wdown.size) * 2
    block_bytes = 2 * 2 * tm * d * 4
    scratch_bytes = tm * d * 4 + tm * d * 2 + SUBLANES * 2 * dff * 4 + tm * dff * 2 + tm * d * 4
    return pl.pallas_call(
        functools.partial(_ffn_kernel, sub=sub, n_sub=n_sub),
        grid=(b, t // tm),
        in_specs=[xspec, cspec, _full(gpre.shape), _full(wup.shape), _full(wc.shape), _full(bc.shape),
                  _full(wdown.shape), _full(gpost.shape)],
        out_specs=[xspec, cspec],
        out_shape=[jax.ShapeDtypeStruct((b, t, d), F32),
                   jax.ShapeDtypeStruct((b, CONV_WIDTH - 1, 2 * dff), F32)],
        scratch_shapes=[pltpu.VMEM((n_sub, d // LANES, sub, LANES), F32),
                        pltpu.VMEM((n_sub, sub, d), BF16),
                        pltpu.VMEM((SUBLANES, 2 * dff), F32),
                        pltpu.VMEM((n_sub, sub, dff), BF16),
                        pltpu.VMEM((n_sub, sub, d), F32)],
        compiler_params=_params(2, weight_bytes + block_bytes + 2 * scratch_bytes),
        name="ffn",
    )(x, conv_prev, gpre, wup, wc, bc, wdown, gpost)


def _layer(x, past_k, past_v, conv_prev, w):
    b, t, d = x.shape
    gm = GM_GROUPS * GM_GROUP_DIM
    sbw = SB_HEADS * SB_HEAD_DIM
    chunk = min(t, GM_CHUNK)
    n_tok = b * t
    decode = past_k is not None
    tm = min(256, t)
    n_sub = min(SUB_BLOCKS_PER_STEP, t // tm)
    assert t % (tm * n_sub) == 0 and tm % chunk == 0, (t, tm, n_sub, chunk)

    tri = jnp.tril(jnp.ones((chunk, chunk), dtype=bool))
    ws = jnp.where(tri[None], w["w_s"][:, :chunk, :chunk], 0.0).astype(BF16)
    bias = jnp.repeat(w["b_s"][:, :chunk].T, GM_GROUP_DIM, axis=1)
    w_in = w["w_in"]
    w_uvq = w_in[:, :2 * gm + sbw]
    w_k = w_in[:, 2 * gm + sbw:2 * gm + 2 * sbw]
    w_v = w_in[:, 2 * gm + 2 * sbw:]
    if not decode:
        w_k, w_v = w_k.T, w_v.T

    outs = _proj_call(
        x.reshape(n_tok, d), w["g_pre_mix"], w_uvq, w_k, w_v, w["ln_v_g"], w["ln_v_b"], ws, bias,
        w["w_gate"], w["b_gate"], sub=tm, n_sub=n_sub, chunk=chunk, seq_len=t, transposed_kv=not decode,
        emit_va=decode)
    ya, q, k_out, v_out, kb, vb, gates = outs[:7]
    q3 = q.reshape(b, t, sbw)

    if decode:
        p = past_k.shape[1]
        kt_past = jnp.transpose(past_k, (0, 2, 3, 1)).reshape(b, sbw, p)
        vt_past = jnp.transpose(past_v, (0, 2, 3, 1)).reshape(b, sbw, p)
        yb = _sb_sample_call(q3, kb.reshape(b, t, sbw), vb.reshape(b, t, sbw), kt_past, vt_past, tk=min(p, 256))
        k4 = k_out.reshape(b, t, SB_HEADS, SB_HEAD_DIM)
        v4 = v_out.reshape(b, t, SB_HEADS, SB_HEAD_DIM)
        va = outs[7].reshape(b, t, gm)
    else:
        yb = _sb_prompt_call(q3, kb, vb, tq=tm, heads=SB_PROMPT_HEADS_PER_STEP)
        k4 = jnp.transpose(k_out.reshape(b, SB_HEADS, SB_HEAD_DIM, t), (0, 3, 1, 2))
        v4 = jnp.transpose(v_out.reshape(b, SB_HEADS, SB_HEAD_DIM, t), (0, 3, 1, 2))
        va = None

    x1 = _merge_call(x.reshape(n_tok, d), ya, yb.reshape(n_tok, sbw), gates,
                     w["w_branch_a"], w["w_branch_b"], w["w_o"], w["g_post_mix"], sub=tm, n_sub=n_sub)
    y, conv_new = _ffn_call(x1.reshape(b, t, d), conv_prev, w["g_pre_ffn"], w["w_up"], w["w_conv"],
                            w["b_conv"], w["w_down"], w["g_post_ffn"], sub=tm, n_sub=n_sub)
    return y, k4, v4, conv_new, va


def kernel(x_prompt, x_sample, cache_sb_k, cache_sb_v, state_ffn_conv, g_pre_mix, w_in, ln_v_g, ln_v_b, w_s, b_s,
           w_gate, b_gate, w_branch_a, w_branch_b, w_o, g_post_mix, g_pre_ffn, w_up, w_conv, b_conv, w_down,
           g_post_ffn):
    depth = w_in.shape[0]
    y_p, y_s = x_prompt, x_sample
    outs = [[] for _ in range(7)]
    for l in range(depth):
        row = lambda a: a[l][None, :]
        w = dict(
            g_pre_mix=row(g_pre_mix), w_in=w_in[l].astype(BF16), ln_v_g=row(ln_v_g), ln_v_b=row(ln_v_b),
            w_s=w_s[l], b_s=b_s[l], w_gate=w_gate[l].astype(BF16), b_gate=row(b_gate),
            w_branch_a=w_branch_a[l].astype(BF16), w_branch_b=w_branch_b[l].astype(BF16),
            w_o=w_o[l].astype(BF16), g_post_mix=row(g_post_mix), g_pre_ffn=row(g_pre_ffn),
            w_up=w_up[l].astype(BF16), w_conv=w_conv[l], b_conv=row(b_conv),
            w_down=w_down[l].astype(BF16), g_post_ffn=row(g_post_ffn))
        zero_conv = jnp.zeros((y_p.shape[0], CONV_WIDTH - 1, w_up.shape[2]), y_p.dtype)
        y_p, kp, vp, cp, _ = _layer(y_p, None, None, zero_conv, w)
        y_s, ks, vs, cs, gvs = _layer(y_s, cache_sb_k[l], cache_sb_v[l], state_ffn_conv[l], w)
        for lst, a in zip(outs, (kp, vp, cp, ks, vs, cs, gvs)):
            lst.append(a)
    return (y_p, y_s) + tuple(jnp.stack(o) for o in outs)
```

```python
import functools
import math
from typing import Callable, NamedTuple, Optional

import jax
import jax.numpy as jnp
from jax import lax
from jax.experimental import pallas as pl
from jax.experimental.pallas import tpu as pltpu

F32 = jnp.float32
BF16 = jnp.bfloat16

EPS = 1e-6
SB_HEADS = 16
SB_HEAD_DIM = 64
GM_GROUPS = 8
GM_GROUP_DIM = 128
GM_CHUNK = 128
CONV_WIDTH = 3

LANES = 128
SUBLANES = 8
MXU_COLS = 256
HEADS_PER_LANE_TILE = LANES // SB_HEAD_DIM
HEADS_PER_MXU_TILE = MXU_COLS // SB_HEAD_DIM
SUB_BLOCKS_PER_STEP = 2
SB_PROMPT_HEADS_PER_STEP = 16
V7X_SCOPED_VMEM_BYTES = 60000 * 1024

LOG2E = math.log2(math.e)
F32_EXP2_UNDERFLOW = 150.0
SIGN_BIT = 0x80000000

_ARB = "arbitrary"
_NT = (((1,), (1,)), ((), ()))


def _rms(x, g):
    return x * lax.rsqrt(jnp.mean(x * x, axis=-1, keepdims=True) + EPS) * g


def _dot(a, b):
    return jnp.dot(a, b, preferred_element_type=F32)


def _dot_nt(a, b):
    return lax.dot_general(a, b, _NT, preferred_element_type=F32)


def _params(n_grid, vmem_bytes):
    return pltpu.CompilerParams(
        dimension_semantics=(_ARB,) * n_grid,
        vmem_limit_bytes=int(min(vmem_bytes, V7X_SCOPED_VMEM_BYTES)))


def _full(shape):
    nd = len(shape)
    return pl.BlockSpec(shape, lambda *_: (0,) * nd, pipeline_mode=pl.Buffered(1))


def _proj_kernel(x_ref, gpre_ref, win_ref, wk_ref, wv_ref, lng_ref, lnb_ref, ws_ref, bias_ref, wg_ref, bg_ref,
                 ya_ref, q_ref, k_ref, v_ref, kb_ref, vb_ref, gate_ref, *rest,
                 sub, n_sub, chunk, transposed_kv, emit_va):
    if emit_va:
        va_ref, hb_s, u_s, gv_s, vab_s = rest
    else:
        hb_s, u_s, gv_s, vab_s = rest
    gm = GM_GROUPS * GM_GROUP_DIM
    sb = SB_HEADS * SB_HEAD_DIM
    qscale = SB_HEAD_DIM ** -0.5 * LOG2E
    nch = sub // chunk

    def stages_of(i):
        base = i * sub
        tok = slice(base, base + sub)

        def norm():
            hb_s[i] = _rms(x_ref[tok, :], gpre_ref[...]).astype(BF16)

        def proj(col0, n):
            cols = slice(col0 + n * MXU_COLS, col0 + (n + 1) * MXU_COLS)
            return _dot(hb_s[i], win_ref[:, cols])

        def gating_inputs(n):
            oc = slice(n * MXU_COLS, (n + 1) * MXU_COLS)
            u_s[i, :, oc] = jax.nn.gelu(proj(0, n))
            gv_s[i, :, oc] = jax.nn.gelu(proj(gm, n))

        def layer_norm():
            gv = gv_s[i]
            mu = jnp.mean(gv, axis=-1, keepdims=True)
            dev = gv - mu
            var = jnp.mean(dev * dev, axis=-1, keepdims=True)
            va = dev * lax.rsqrt(var + EPS) * lng_ref[...] + lnb_ref[...]
            if emit_va:
                va_ref[tok, :] = va
            vab_s[i] = va.astype(BF16)

        def spatial_gating(g):
            cols = slice(g * GM_GROUP_DIM, (g + 1) * GM_GROUP_DIM)
            pieces = [vab_s[i, c * chunk:(c + 1) * chunk, cols] for c in range(nch)]
            mixed = _dot(ws_ref[g], pieces[0] if nch == 1 else jnp.concatenate(pieces, axis=1))
            for c in range(nch):
                rows = slice(c * chunk, (c + 1) * chunk)
                m = mixed[:, c * GM_GROUP_DIM:(c + 1) * GM_GROUP_DIM] + bias_ref[:, cols]
                ya_ref[base + c * chunk:base + (c + 1) * chunk, cols] = (u_s[i, rows, cols] * m).astype(BF16)

        def queries(n):
            q_ref[tok, n * MXU_COLS:(n + 1) * MXU_COLS] = (proj(2 * gm, n) * qscale).astype(BF16)

        def keys_or_values(w_ref, f32_out, bf16_out, n):
            oc = slice(n * MXU_COLS, (n + 1) * MXU_COLS)
            if transposed_kv:
                pt = _dot_nt(w_ref[oc, :], hb_s[i])
                f32_out[oc, tok] = pt
                bf16_out[oc, tok] = pt.astype(BF16)
            else:
                p = _dot(hb_s[i], w_ref[:, oc])
                bf16_out[tok, oc] = p.astype(BF16)
                for hs in range(HEADS_PER_MXU_TILE):
                    h = n * HEADS_PER_MXU_TILE + hs
                    f32_out[pl.ds(base * SB_HEADS + h, sub, stride=SB_HEADS), :] = (
                        p[:, hs * SB_HEAD_DIM:(hs + 1) * SB_HEAD_DIM])

        def gates(n):
            oc = slice(n * MXU_COLS, (n + 1) * MXU_COLS)
            gate_ref[tok, oc] = jax.nn.sigmoid(_dot(hb_s[i], wg_ref[:, oc]) + bg_ref[:, oc]).astype(BF16)

        part = functools.partial
        n_sb = sb // MXU_COLS
        return ([norm] + [part(gating_inputs, n) for n in range(gm // MXU_COLS)]
                + [part(queries, n) for n in range(n_sb)] + [layer_norm]
                + [part(keys_or_values, wk_ref, k_ref, kb_ref, n) for n in range(n_sb)]
                + [part(spatial_gating, g) for g in range(GM_GROUPS)]
                + [part(keys_or_values, wv_ref, v_ref, vb_ref, n) for n in range(n_sb)]
                + [part(gates, n) for n in range(gate_ref.shape[1] // MXU_COLS)])

    _skewed([stages_of(i) for i in range(n_sub)], skew=2)


def _proj_call(x2d, gpre, win, wk, wv, lng, lnb, ws, bias, wg, bg, *, sub, n_sub, chunk, seq_len, transposed_kv,
               emit_va):
    n_tok, d = x2d.shape
    gm = GM_GROUPS * GM_GROUP_DIM
    sb = SB_HEADS * SB_HEAD_DIM
    n_gate = wg.shape[1]
    tm = sub * n_sub

    def tok(width):
        return pl.BlockSpec((tm, width), lambda i: (i, 0))

    if transposed_kv:
        nt = seq_len // tm
        kv_spec = pl.BlockSpec((None, sb, tm), lambda i: (i // nt, 0, i % nt))
        kv_f32 = jax.ShapeDtypeStruct((n_tok // seq_len, sb, seq_len), F32)
        kv_bf16 = jax.ShapeDtypeStruct((n_tok // seq_len, sb, seq_len), BF16)
        kv_specs = [kv_spec] * 4
        kv_bytes = 2 * 2 * sb * tm * (4 + 2)
    else:
        hm_spec = pl.BlockSpec((tm * SB_HEADS, SB_HEAD_DIM), lambda i: (i, 0))
        kv_f32 = jax.ShapeDtypeStruct((n_tok * SB_HEADS, SB_HEAD_DIM), F32)
        kv_bf16 = jax.ShapeDtypeStruct((n_tok, sb), BF16)
        kv_specs = [hm_spec, hm_spec, tok(sb), tok(sb)]
        kv_bytes = 2 * 2 * tm * (SB_HEADS * LANES * 4 + sb * 2)
    out_specs = [tok(gm), tok(sb)] + kv_specs + [tok(n_gate)]
    out_shape = [jax.ShapeDtypeStruct((n_tok, gm), BF16), jax.ShapeDtypeStruct((n_tok, sb), BF16),
                 kv_f32, kv_f32, kv_bf16, kv_bf16, jax.ShapeDtypeStruct((n_tok, n_gate), BF16)]
    if emit_va:
        out_specs.append(tok(gm))
        out_shape.append(jax.ShapeDtypeStruct((n_tok, gm), F32))

    weight_bytes = (win.size + wk.size + wv.size + wg.size + ws.size) * 2
    block_bytes = 2 * tm * (d * 4 + gm * 2 + sb * 2 + n_gate * 2 + gm * 4) + kv_bytes
    scratch_bytes = tm * (d * 2 + gm * 4 + gm * 4 + gm * 2)
    return pl.pallas_call(
        functools.partial(_proj_kernel, sub=sub, n_sub=n_sub, chunk=chunk, transposed_kv=transposed_kv,
                          emit_va=emit_va),
        grid=(n_tok // tm,),
        in_specs=[tok(d), _full(gpre.shape), _full(win.shape), _full(wk.shape), _full(wv.shape),
                  _full(lng.shape), _full(lnb.shape), _full(ws.shape), _full(bias.shape), _full(wg.shape),
                  _full(bg.shape)],
        out_specs=out_specs,
        out_shape=out_shape,
        scratch_shapes=[pltpu.VMEM((n_sub, sub, d), BF16), pltpu.VMEM((n_sub, sub, gm), F32),
                        pltpu.VMEM((n_sub, sub, gm), F32), pltpu.VMEM((n_sub, sub, gm), BF16)],
        compiler_params=_params(1, weight_bytes + block_bytes + 2 * scratch_bytes),
        name="proj",
    )(x2d, gpre, win, wk, wv, lng, lnb, ws, bias, wg, bg)


class _Tile(NamedTuple):
    head: int
    scores: Callable
    values: Callable
    v_transposed: bool
    tri: jax.Array
    mask: Optional[jax.Array]
    first: bool


def _sb_tiles(tiles, c_ref, cmin_ref, acc_ref):
    n = len(tiles)
    z, sp, cum, weight = ([None] * n for _ in range(4))

    def scores(i):
        z[i] = tiles[i].scores()

    def softplus(i):
        neg_abs = pltpu.bitcast(pltpu.bitcast(z[i], jnp.uint32) | jnp.uint32(SIGN_BIT), F32)
        s = jnp.maximum(z[i], 0.0) + jnp.log2(1.0 + jnp.exp2(neg_abs))
        sp[i] = s if tiles[i].mask is None else jnp.where(tiles[i].mask, s, 0.0)

    def cumsum(i):
        cum[i] = _dot(sp[i].astype(BF16), tiles[i].tri)

    def weights(i):
        h = tiles[i].head
        total = sp[i] + cum[i]
        if not tiles[i].first:
            total = total + c_ref[h]
        a = jnp.exp2(z[i] - total)
        if tiles[i].mask is not None:
            a = jnp.where(tiles[i].mask, a, 0.0)
        weight[i] = a.astype(BF16)
        c_ref[h] = total[:, 0:1]
        tq, width = total.shape[0], min(LANES, total.shape[1])
        cmin_ref[h, :, 0:width] = jnp.min(total[:, 0:width].reshape(tq // SUBLANES, SUBLANES, width), axis=0)
        z[i] = sp[i] = cum[i] = None

    def values(i):
        v = tiles[i].values()
        contrib = _dot_nt(weight[i], v) if tiles[i].v_transposed else _dot(weight[i], v)
        h = tiles[i].head
        lanes = _pair_lanes(h)
        if tiles[i].first and h % HEADS_PER_LANE_TILE == 0:
            acc_ref[:, lanes] = contrib
        else:
            acc_ref[:, lanes] = acc_ref[:, lanes] + contrib
        weight[i] = None

    stages = (scores, softplus, cumsum, weights, values)
    for step in range(n + len(stages) - 1):
        for s, stage in enumerate(stages):
            if 0 <= step - s < n:
                stage(step - s)


def _pair_lanes(h):
    return slice(h // HEADS_PER_LANE_TILE * LANES, (h // HEADS_PER_LANE_TILE + 1) * LANES)


def _strict_tri(n):
    r = lax.broadcasted_iota(jnp.int32, (n, n), 0)
    s = lax.broadcasted_iota(jnp.int32, (n, n), 1)
    return jnp.where(r > s, 1.0, 0.0).astype(BF16)


def _causal_mask(n):
    r = lax.broadcasted_iota(jnp.int32, (n, n), 0)
    s = lax.broadcasted_iota(jnp.int32, (n, n), 1)
    return s < r


def _head_lane_masks():
    lane = lax.broadcasted_iota(jnp.int32, (1, LANES), 1)
    return [(lane >= hh * SB_HEAD_DIM) & (lane < (hh + 1) * SB_HEAD_DIM) for hh in range(HEADS_PER_LANE_TILE)]


def _head_row_masks():
    row = lax.broadcasted_iota(jnp.int32, (LANES, 1), 0)
    return [(row >= hh * SB_HEAD_DIM) & (row < (hh + 1) * SB_HEAD_DIM) for hh in range(HEADS_PER_LANE_TILE)]


def _masked_queries(q_ref, qm_s, heads):
    lane_masks = _head_lane_masks()
    for h in range(heads):
        lt = h // HEADS_PER_LANE_TILE
        qt = q_ref[:, lt * LANES:(lt + 1) * LANES]
        qm_s[h] = jnp.where(lane_masks[h % HEADS_PER_LANE_TILE], qt, jnp.zeros_like(qt))


def _sweep_is_live(cmin_ref):
    return (jnp.min(cmin_ref[:, :, 0:1]) < F32_EXP2_UNDERFLOW).astype(jnp.int32)


def _sb_prompt_kernel(q_ref, kt_ref, vt_ref, o_ref, acc_s, c_s, cmin_s, u_s, qm_s, *, tq, heads):
    i = pl.program_id(2)

    @pl.when(i == 0)
    def _():
        u_s[...] = _strict_tri(tq)

    _masked_queries(q_ref, qm_s, heads)
    row_masks = _head_row_masks()

    def tiles(j, diagonal):
        keys = pl.ds(pl.multiple_of(j * tq, tq), tq)
        mask = _causal_mask(tq) if diagonal else None

        def scores(h):
            return _dot(qm_s[h], kt_ref[_pair_lanes(h), keys])

        def values(h):
            vt = vt_ref[_pair_lanes(h), keys]
            return jnp.where(row_masks[h % HEADS_PER_LANE_TILE], vt, jnp.zeros_like(vt))

        return [_Tile(h, functools.partial(scores, h), functools.partial(values, h), True, u_s[...], mask,
                      first=diagonal) for h in range(heads)]

    @pl.when(i == 0)
    def _():
        _sb_tiles(tiles(i, True), c_s, cmin_s, acc_s)

    @pl.when(i > 0)
    def _():
        _sb_tiles(tiles(i, True) + tiles(i - 1, False), c_s, cmin_s, acc_s)

    def cond(carry):
        n, go = carry
        return jnp.logical_and(n < i - 1, go > 0)

    def body(carry):
        n, _ = carry
        _sb_tiles(tiles(i - 2 - n, False), c_s, cmin_s, acc_s)
        return n + 1, _sweep_is_live(cmin_s)

    lax.while_loop(cond, body, (jnp.int32(0), _sweep_is_live(cmin_s)))
    o_ref[...] = acc_s[...].astype(BF16)


def _sb_prompt_call(q, kt, vt, *, tq, heads):
    b, t, w = q.shape
    width = heads * SB_HEAD_DIM
    qspec = pl.BlockSpec((None, tq, width), lambda bi, hg, i: (bi, i, hg))
    kvspec = pl.BlockSpec((None, width, t), lambda bi, hg, i: (bi, hg, 0))
    vmem = (2 * 2 * t * width * 2 + 2 * 2 * tq * width * 2 + tq * width * 4 + heads * tq * LANES * 4
            + 2 * tq * tq * 2 + heads * tq * LANES * 2 + heads * 12 * tq * tq * 4)
    return pl.pallas_call(
        functools.partial(_sb_prompt_kernel, tq=tq, heads=heads),
        grid=(b, w // width, t // tq),
        in_specs=[qspec, kvspec, kvspec],
        out_specs=qspec,
        out_shape=jax.ShapeDtypeStruct((b, t, w), BF16),
        scratch_shapes=[pltpu.VMEM((tq, width), F32),
                        pltpu.VMEM((heads, tq, 1), F32),
                        pltpu.VMEM((heads, SUBLANES, LANES), F32),
                        pltpu.VMEM((tq, tq), BF16),
                        pltpu.VMEM((heads, tq, LANES), BF16)],
        compiler_params=_params(3, vmem),
        name="sb_prompt",
    )(q, kt, vt)


def _sb_sample_kernel(q_ref, kn_ref, vn_ref, kp_hbm, vp_hbm, o_ref, kbuf, vbuf, sem, acc_s, c_s, cmin_s, u_s, ud_s,
                      qm_s, *, tq, tk, n_past):
    b = pl.program_id(0)

    def copies(j, slot):
        keys = pl.ds(pl.multiple_of(j * tk, tk), tk)
        return (pltpu.make_async_copy(kp_hbm.at[b, :, keys], kbuf.at[slot], sem.at[0, slot]),
                pltpu.make_async_copy(vp_hbm.at[b, :, keys], vbuf.at[slot], sem.at[1, slot]))

    def start(j, slot):
        for cp in copies(j, slot):
            cp.start()

    def wait(j, slot):
        for cp in copies(j, slot):
            cp.wait()

    start(n_past - 1, 0)
    u_s[...] = _strict_tri(tk)
    ud_s[...] = _strict_tri(tq)
    _masked_queries(q_ref, qm_s, SB_HEADS)
    lane_masks = _head_lane_masks()
    row_masks = _head_row_masks()

    def z_new(h):
        return _dot_nt(qm_s[h], kn_ref[:, _pair_lanes(h)])

    def v_new(h):
        vn = vn_ref[:, _pair_lanes(h)]
        return jnp.where(lane_masks[h % HEADS_PER_LANE_TILE], vn, jnp.zeros_like(vn))

    mask = _causal_mask(tq)
    _sb_tiles([_Tile(h, functools.partial(z_new, h), functools.partial(v_new, h), False, ud_s[...], mask,
                     first=True) for h in range(SB_HEADS)], c_s, cmin_s, acc_s)

    def cond(carry):
        n, go = carry
        return jnp.logical_and(n < n_past, go > 0)

    def body(carry):
        n, _ = carry
        j = n_past - 1 - n
        slot = lax.rem(n, 2)
        wait(j, slot)

        @pl.when(n + 1 < n_past)
        def _():
            start(j - 1, 1 - slot)

        def z_past(h):
            return _dot(qm_s[h], kbuf[slot, _pair_lanes(h), :].astype(BF16))

        def v_past(h):
            vt = vbuf[slot, _pair_lanes(h), :].astype(BF16)
            return jnp.where(row_masks[h % HEADS_PER_LANE_TILE], vt, jnp.zeros_like(vt))

        _sb_tiles([_Tile(h, functools.partial(z_past, h), functools.partial(v_past, h), True, u_s[...], None,
                         first=False) for h in range(SB_HEADS)], c_s, cmin_s, acc_s)
        return n + 1, _sweep_is_live(cmin_s)

    n_done, _ = lax.while_loop(cond, body, (jnp.int32(0), _sweep_is_live(cmin_s)))

    @pl.when(n_done < n_past)
    def _():
        wait(n_past - 1 - n_done, lax.rem(n_done, 2))

    o_ref[...] = acc_s[...].astype(BF16)


def _sb_sample_call(q, k_new, v_new, kt_past, vt_past, *, tk):
    b, tq, w = q.shape
    p = kt_past.shape[2]
    newspec = pl.BlockSpec((None, tq, w), lambda bi: (bi, 0, 0))
    anyspec = pl.BlockSpec(memory_space=pl.ANY)
    vmem = (2 * 2 * w * tk * 4 + 2 * 4 * tq * w * 2 + tq * w * 4 + SB_HEADS * tq * LANES * 4
            + 2 * tk * tk * 2 + SB_HEADS * tq * LANES * 2 + SB_HEADS * 12 * tq * tk * 4)
    return pl.pallas_call(
        functools.partial(_sb_sample_kernel, tq=tq, tk=tk, n_past=p // tk),
        grid=(b,),
        in_specs=[newspec, newspec, newspec, anyspec, anyspec],
        out_specs=newspec,
        out_shape=jax.ShapeDtypeStruct((b, tq, w), BF16),
        scratch_shapes=[pltpu.VMEM((2, w, tk), F32),
                        pltpu.VMEM((2, w, tk), F32),
                        pltpu.SemaphoreType.DMA((2, 2)),
                        pltpu.VMEM((tq, w), F32),
                        pltpu.VMEM((SB_HEADS, tq, 1), F32),
                        pltpu.VMEM((SB_HEADS, SUBLANES, LANES), F32),
                        pltpu.VMEM((tk, tk), BF16),
                        pltpu.VMEM((tq, tq), BF16),
                        pltpu.VMEM((SB_HEADS, tq, LANES), BF16)],
        compiler_params=_params(1, vmem),
        name="sb_sample",
    )(q, k_new, v_new, kt_past, vt_past)


def _merge_kernel(x_ref, ya_ref, yb_ref, gate_ref, wa_ref, wb_ref, wo_ref, gpost_ref, o_ref, m_s, p_s,
                  *, sub, n_sub):
    d = x_ref.shape[1]

    def stages_of(i):
        tok = slice(i * sub, (i + 1) * sub)

        def merge(n):
            cols = slice(n * MXU_COLS, (n + 1) * MXU_COLS)
            gcols = slice(d + n * MXU_COLS, d + (n + 1) * MXU_COLS)
            merged = (gate_ref[tok, cols].astype(F32) * _dot(ya_ref[tok, :], wa_ref[:, cols])
                      + gate_ref[tok, gcols].astype(F32) * _dot(yb_ref[tok, :], wb_ref[:, cols]))
            m_s[i, :, cols] = merged.astype(BF16)

        def project(n):
            cols = slice(n * MXU_COLS, (n + 1) * MXU_COLS)
            p_s[i, :, cols] = _dot(m_s[i], wo_ref[:, cols])

        def norm_and_store():
            o_ref[tok, :] = x_ref[tok, :] + _rms(p_s[i], gpost_ref[...])

        n_chunks = d // MXU_COLS
        return ([functools.partial(merge, n) for n in range(n_chunks)]
                + [functools.partial(project, n) for n in range(n_chunks)] + [norm_and_store])

    _skewed([stages_of(i) for i in range(n_sub)], skew=2)


def _merge_call(x2d, ya, yb, gates, wa, wb, wo, gpost, *, sub, n_sub):
    n_tok, d = x2d.shape
    tm = sub * n_sub

    def tok(width):
        return pl.BlockSpec((tm, width), lambda i: (i, 0))

    weight_bytes = (wa.size + wb.size + wo.size) * 2
    block_bytes = 2 * tm * (d * 4 + ya.shape[1] * 2 + yb.shape[1] * 2 + gates.shape[1] * 2 + d * 4)
    scratch_bytes = tm * d * 6
    return pl.pallas_call(
        functools.partial(_merge_kernel, sub=sub, n_sub=n_sub),
        grid=(n_tok // tm,),
        in_specs=[tok(d), tok(ya.shape[1]), tok(yb.shape[1]), tok(gates.shape[1]),
                  _full(wa.shape), _full(wb.shape), _full(wo.shape), _full(gpost.shape)],
        out_specs=tok(d),
        out_shape=jax.ShapeDtypeStruct((n_tok, d), F32),
        scratch_shapes=[pltpu.VMEM((n_sub, sub, d), BF16), pltpu.VMEM((n_sub, sub, d), F32)],
        compiler_params=_params(1, weight_bytes + block_bytes + 2 * scratch_bytes),
        name="merge",
    )(x2d, ya, yb, gates, wa, wb, wo, gpost)


def _skewed(stage_lists, skew):
    n_stages = len(stage_lists[0])
    for step in range(n_stages + skew * (len(stage_lists) - 1)):
        for i, stages in enumerate(stage_lists):
            if 0 <= step - i * skew < n_stages:
                stages[step - i * skew]()


def _ffn_kernel(x_ref, prev_ref, gpre_ref, wup_ref, wc_ref, bc_ref, wdown_ref, gpost_ref,
                y_ref, newconv_ref, xp_s, hn_s, carry_s, act_s, f_s, *, sub, n_sub):
    assert CONV_WIDTH == 3
    dff = wdown_ref.shape[0]
    d = x_ref.shape[1]
    ng = sub // SUBLANES
    assert ng >= CONV_WIDTH - 1 and ng % SUBLANES == 0
    n_slabs = d // LANES
    t = pl.program_id(1)

    @pl.when(t == 0)
    def _():
        carry_s[0:CONV_WIDTH - 1, :] = prev_ref[...]

    top = lax.broadcasted_iota(jnp.int32, (SUBLANES, MXU_COLS), 0) == 0

    def permuted_rows(r):
        return pl.ds((SUBLANES * r % ng) * SUBLANES + SUBLANES * r // ng, SUBLANES, stride=SUBLANES)

    def stages_of(i):
        base = i * sub

        def load_and_norm():
            for r in range(ng):
                rows = slice(base + r * SUBLANES, base + (r + 1) * SUBLANES)
                for c in range(n_slabs):
                    xp_s[i, c, permuted_rows(r), :] = x_ref[rows, c * LANES:(c + 1) * LANES]
            xp = jnp.concatenate([xp_s[i, c] for c in range(n_slabs)], axis=1)
            hn_s[i] = _rms(xp, gpre_ref[...]).astype(BF16)

        def up_conv_act(n):
            halves = []
            for off in (0, dff):
                cols = slice(off + n * MXU_COLS, off + (n + 1) * MXU_COLS)
                up = _dot(hn_s[i], wup_ref[:, cols])
                back1 = jnp.where(top, carry_s[1:2, cols], pltpu.roll(up[sub - SUBLANES:sub], 1, axis=0))
                back2 = jnp.where(top, carry_s[0:1, cols],
                                  pltpu.roll(up[sub - 2 * SUBLANES:sub - SUBLANES], 1, axis=0))
                carry_s[0:1, cols] = up[sub - SUBLANES - 1:sub - SUBLANES]
                carry_s[1:2, cols] = up[sub - 1:sub]
                prev1 = jnp.concatenate([back1, up[:sub - SUBLANES]], axis=0)
                prev2 = jnp.concatenate([back2, back1, up[:sub - 2 * SUBLANES]], axis=0)
                halves.append(bc_ref[:, cols] + wc_ref[0:1, cols] * prev2 + wc_ref[1:2, cols] * prev1
                              + wc_ref[2:3, cols] * up)
            act_s[i, :, n * MXU_COLS:(n + 1) * MXU_COLS] = (jax.nn.gelu(halves[0]) * halves[1]).astype(BF16)

        def down(n):
            cols = slice(n * MXU_COLS, (n + 1) * MXU_COLS)
            f_s[i, :, cols] = _dot(act_s[i], wdown_ref[:, cols])

        def norm_and_store():
            normed = _rms(f_s[i], gpost_ref[...])
            for c in range(n_slabs):
                xp_s[i, c] = xp_s[i, c] + normed[:, c * LANES:(c + 1) * LANES]
            for r in range(ng):
                rows = slice(base + r * SUBLANES, base + (r + 1) * SUBLANES)
                for c in range(n_slabs):
                    y_ref[rows, c * LANES:(c + 1) * LANES] = xp_s[i, c, permuted_rows(r), :]
            if i == n_sub - 1:
                newconv_ref[...] = carry_s[0:CONV_WIDTH - 1, :]

        return ([load_and_norm] + [functools.partial(up_conv_act, n) for n in range(dff // MXU_COLS)]
                + [functools.partial(down, n) for n in range(d // MXU_COLS)] + [norm_and_store])

    _skewed([stages_of(i) for i in range(n_sub)], skew=2)


def _ffn_call(x, conv_prev, gpre, wup, wc, bc, wdown, gpost, *, sub, n_sub):
    b, t, d = x.shape
    dff = wdown.shape[0]
    tm = sub * n_sub
    xspec = pl.BlockSpec((None, tm, d), lambda bi, ti: (bi, ti, 0))
    cspec = pl.BlockSpec((None, CONV_WIDTH - 1, 2 * dff), lambda bi, ti: (bi, 0, 0))
    weight_bytes = (wup.size + wdown.size) * 2
    block_bytes = 2 * 2 * tm * d * 4
    scratch_bytes = tm * d * 4 + tm * d * 2 + SUBLANES * 2 * dff * 4 + tm * dff * 2 + tm * d * 4
    return pl.pallas_call(
        functools.partial(_ffn_kernel, sub=sub, n_sub=n_sub),
        grid=(b, t // tm),
        in_specs=[xspec, cspec, _full(gpre.shape), _full(wup.shape), _full(wc.shape), _full(bc.shape),
                  _full(wdown.shape), _full(gpost.shape)],
        out_specs=[xspec, cspec],
        out_shape=[jax.ShapeDtypeStruct((b, t, d), F32),
                   jax.ShapeDtypeStruct((b, CONV_WIDTH - 1, 2 * dff), F32)],
        scratch_shapes=[pltpu.VMEM((n_sub, d // LANES, sub, LANES), F32),
                        pltpu.VMEM((n_sub, sub, d), BF16),
                        pltpu.VMEM((SUBLANES, 2 * dff), F32),
                        pltpu.VMEM((n_sub, sub, dff), BF16),
                        pltpu.VMEM((n_sub, sub, d), F32)],
        compiler_params=_params(2, weight_bytes + block_bytes + 2 * scratch_bytes),
        name="ffn",
    )(x, conv_prev, gpre, wup, wc, bc, wdown, gpost)


def _layer(x, past_k, past_v, conv_prev, w):
    b, t, d = x.shape
    gm = GM_GROUPS * GM_GROUP_DIM
    sbw = SB_HEADS * SB_HEAD_DIM
    chunk = min(t, GM_CHUNK)
    n_tok = b * t
    decode = past_k is not None
    tm = min(256, t)
    n_sub = min(SUB_BLOCKS_PER_STEP, t // tm)
    assert t % (tm * n_sub) == 0 and tm % chunk == 0, (t, tm, n_sub, chunk)

    tri = jnp.tril(jnp.ones((chunk, chunk), dtype=bool))
    ws = jnp.where(tri[None], w["w_s"][:, :chunk, :chunk], 0.0).astype(BF16)
    bias = jnp.repeat(w["b_s"][:, :chunk].T, GM_GROUP_DIM, axis=1)
    w_in = w["w_in"]
    w_uvq = w_in[:, :2 * gm + sbw]
    w_k = w_in[:, 2 * gm + sbw:2 * gm + 2 * sbw]
    w_v = w_in[:, 2 * gm + 2 * sbw:]
    if not decode:
        w_k, w_v = w_k.T, w_v.T

    outs = _proj_call(
        x.reshape(n_tok, d), w["g_pre_mix"], w_uvq, w_k, w_v, w["ln_v_g"], w["ln_v_b"], ws, bias,
        w["w_gate"], w["b_gate"], sub=tm, n_sub=n_sub, chunk=chunk, seq_len=t, transposed_kv=not decode,
        emit_va=decode)
    ya, q, k_out, v_out, kb, vb, gates = outs[:7]
    q3 = q.reshape(b, t, sbw)

    if decode:
        p = past_k.shape[1]
        kt_past = jnp.transpose(past_k, (0, 2, 3, 1)).reshape(b, sbw, p)
        vt_past = jnp.transpose(past_v, (0, 2, 3, 1)).reshape(b, sbw, p)
        yb = _sb_sample_call(q3, kb.reshape(b, t, sbw), vb.reshape(b, t, sbw), kt_past, vt_past, tk=min(p, 256))
        k4 = k_out.reshape(b, t, SB_HEADS, SB_HEAD_DIM)
        v4 = v_out.reshape(b, t, SB_HEADS, SB_HEAD_DIM)
        va = outs[7].reshape(b, t, gm)
    else:
        yb = _sb_prompt_call(q3, kb, vb, tq=tm, heads=SB_PROMPT_HEADS_PER_STEP)
        k4 = jnp.transpose(k_out.reshape(b, SB_HEADS, SB_HEAD_DIM, t), (0, 3, 1, 2))
        v4 = jnp.transpose(v_out.reshape(b, SB_HEADS, SB_HEAD_DIM, t), (0, 3, 1, 2))
        va = None

    x1 = _merge_call(x.reshape(n_tok, d), ya, yb.reshape(n_tok, sbw), gates,
                     w["w_branch_a"], w["w_branch_b"], w["w_o"], w["g_post_mix"], sub=tm, n_sub=n_sub)
    y, conv_new = _ffn_call(x1.reshape(b, t, d), conv_prev, w["g_pre_ffn"], w["w_up"], w["w_conv"],
                            w["b_conv"], w["w_down"], w["g_post_ffn"], sub=tm, n_sub=n_sub)
    return y, k4, v4, conv_new, va


def kernel(x_prompt, x_sample, cache_sb_k, cache_sb_v, state_ffn_conv, g_pre_mix, w_in, ln_v_g, ln_v_b, w_s, b_s,
           w_gate, b_gate, w_branch_a, w_branch_b, w_o, g_post_mix, g_pre_ffn, w_up, w_conv, b_conv, w_down,
           g_post_ffn):
    depth = w_in.shape[0]
    y_p, y_s = x_prompt, x_sample
    outs = [[] for _ in range(7)]
    for l in range(depth):
        row = lambda a: a[l][None, :]
        w = dict(
            g_pre_mix=row(g_pre_mix), w_in=w_in[l].astype(BF16), ln_v_g=row(ln_v_g), ln_v_b=row(ln_v_b),
            w_s=w_s[l], b_s=b_s[l], w_gate=w_gate[l].astype(BF16), b_gate=row(b_gate),
            w_branch_a=w_branch_a[l].astype(BF16), w_branch_b=w_branch_b[l].astype(BF16),
            w_o=w_o[l].astype(BF16), g_post_mix=row(g_post_mix), g_pre_ffn=row(g_pre_ffn),
            w_up=w_up[l].astype(BF16), w_conv=w_conv[l], b_conv=row(b_conv),
            w_down=w_down[l].astype(BF16), g_post_ffn=row(g_post_ffn))
        zero_conv = jnp.zeros((y_p.shape[0], CONV_WIDTH - 1, w_up.shape[2]), y_p.dtype)
        y_p, kp, vp, cp, _ = _layer(y_p, None, None, zero_conv, w)
        y_s, ks, vs, cs, gvs = _layer(y_s, cache_sb_k[l], cache_sb_v[l], state_ffn_conv[l], w)
        for lst, a in zip(outs, (kp, vp, cp, ks, vs, cs, gvs)):
            lst.append(a)
    return (y_p, y_s) + tuple(jnp.stack(o) for o in outs)
```

```python
import functools
import math
from typing import Callable, NamedTuple, Optional

import jax
import jax.numpy as jnp
from jax import lax
from jax.experimental import pallas as pl
from jax.experimental.pallas import tpu as pltpu

F32 = jnp.float32
BF16 = jnp.bfloat16

EPS = 1e-6
SB_HEADS = 16
SB_HEAD_DIM = 64
GM_GROUPS = 8
GM_GROUP_DIM = 128
GM_CHUNK = 128
CONV_WIDTH = 3

LANES = 128
SUBLANES = 8
MXU_COLS = 256
HEADS_PER_LANE_TILE = LANES // SB_HEAD_DIM
HEADS_PER_MXU_TILE = MXU_COLS // SB_HEAD_DIM
SUB_BLOCKS_PER_STEP = 2
SB_PROMPT_HEADS_PER_STEP = 16
V7X_SCOPED_VMEM_BYTES = 60000 * 1024

LOG2E = math.log2(math.e)
F32_EXP2_UNDERFLOW = 150.0
SIGN_BIT = 0x80000000

_ARB = "arbitrary"
_NT = (((1,), (1,)), ((), ()))


def _rms(x, g):
    return x * lax.rsqrt(jnp.mean(x * x, axis=-1, keepdims=True) + EPS) * g


def _dot(a, b):
    return jnp.dot(a, b, preferred_element_type=F32)


def _dot_nt(a, b):
    return lax.dot_general(a, b, _NT, preferred_element_type=F32)


def _params(n_grid, vmem_bytes):
    return pltpu.CompilerParams(
        dimension_semantics=(_ARB,) * n_grid,
        vmem_limit_bytes=int(min(vmem_bytes, V7X_SCOPED_VMEM_BYTES)))


def _full(shape):
    nd = len(shape)
    return pl.BlockSpec(shape, lambda *_: (0,) * nd, pipeline_mode=pl.Buffered(1))


def _proj_kernel(x_ref, gpre_ref, win_ref, wk_ref, wv_ref, lng_ref, lnb_ref, ws_ref, bias_ref, wg_ref, bg_ref,
                 ya_ref, q_ref, k_ref, v_ref, kb_ref, vb_ref, gate_ref, *rest,
                 sub, n_sub, chunk, transposed_kv, emit_va):
    if emit_va:
        va_ref, hb_s, u_s, gv_s, vab_s = rest
    else:
        hb_s, u_s, gv_s, vab_s = rest
    gm = GM_GROUPS * GM_GROUP_DIM
    sb = SB_HEADS * SB_HEAD_DIM
    qscale = SB_HEAD_DIM ** -0.5 * LOG2E
    nch = sub // chunk

    def stages_of(i):
        base = i * sub
        tok = slice(base, base + sub)

        def norm():
            hb_s[i] = _rms(x_ref[tok, :], gpre_ref[...]).astype(BF16)

        def proj(col0, n):
            cols = slice(col0 + n * MXU_COLS, col0 + (n + 1) * MXU_COLS)
            return _dot(hb_s[i], win_ref[:, cols])

        def gating_inputs(n):
            oc = slice(n * MXU_COLS, (n + 1) * MXU_COLS)
            u_s[i, :, oc] = jax.nn.gelu(proj(0, n))
            gv_s[i, :, oc] = jax.nn.gelu(proj(gm, n))

        def layer_norm():
            gv = gv_s[i]
            mu = jnp.mean(gv, axis=-1, keepdims=True)
            dev = gv - mu
            var = jnp.mean(dev * dev, axis=-1, keepdims=True)
            va = dev * lax.rsqrt(var + EPS) * lng_ref[...] + lnb_ref[...]
            if emit_va:
                va_ref[tok, :] = va
            vab_s[i] = va.astype(BF16)

        def spatial_gating(g):
            cols = slice(g * GM_GROUP_DIM, (g + 1) * GM_GROUP_DIM)
            pieces = [vab_s[i, c * chunk:(c + 1) * chunk, cols] for c in range(nch)]
            mixed = _dot(ws_ref[g], pieces[0] if nch == 1 else jnp.concatenate(pieces, axis=1))
            for c in range(nch):
                rows = slice(c * chunk, (c + 1) * chunk)
                m = mixed[:, c * GM_GROUP_DIM:(c + 1) * GM_GROUP_DIM] + bias_ref[:, cols]
                ya_ref[base + c * chunk:base + (c + 1) * chunk, cols] = (u_s[i, rows, cols] * m).astype(BF16)

        def queries(n):
            q_ref[tok, n * MXU_COLS:(n + 1) * MXU_COLS] = (proj(2 * gm, n) * qscale).astype(BF16)

        def keys_or_values(w_ref, f32_out, bf16_out, n):
            oc = slice(n * MXU_COLS, (n + 1) * MXU_COLS)
            if transposed_kv:
                pt = _dot_nt(w_ref[oc, :], hb_s[i])
                f32_out[oc, tok] = pt
                bf16_out[oc, tok] = pt.astype(BF16)
            else:
                p = _dot(hb_s[i], w_ref[:, oc])
                bf16_out[tok, oc] = p.astype(BF16)
                for hs in range(HEADS_PER_MXU_TILE):
                    h = n * HEADS_PER_MXU_TILE + hs
                    f32_out[pl.ds(base * SB_HEADS + h, sub, stride=SB_HEADS), :] = (
                        p[:, hs * SB_HEAD_DIM:(hs + 1) * SB_HEAD_DIM])

        def gates(n):
            oc = slice(n * MXU_COLS, (n + 1) * MXU_COLS)
            gate_ref[tok, oc] = jax.nn.sigmoid(_dot(hb_s[i], wg_ref[:, oc]) + bg_ref[:, oc]).astype(BF16)

        part = functools.partial
        n_sb = sb // MXU_COLS
        return ([norm] + [part(gating_inputs, n) for n in range(gm // MXU_COLS)]
                + [part(queries, n) for n in range(n_sb)] + [layer_norm]
                + [part(keys_or_values, wk_ref, k_ref, kb_ref, n) for n in range(n_sb)]
                + [part(spatial_gating, g) for g in range(GM_GROUPS)]
                + [part(keys_or_values, wv_ref, v_ref, vb_ref, n) for n in range(n_sb)]
                + [part(gates, n) for n in range(gate_ref.shape[1] // MXU_COLS)])

    _skewed([stages_of(i) for i in range(n_sub)], skew=2)


def _proj_call(x2d, gpre, win, wk, wv, lng, lnb, ws, bias, wg, bg, *, sub, n_sub, chunk, seq_len, transposed_kv,
               emit_va):
    n_tok, d = x2d.shape
    gm = GM_GROUPS * GM_GROUP_DIM
    sb = SB_HEADS * SB_HEAD_DIM
    n_gate = wg.shape[1]
    tm = sub * n_sub

    def tok(width):
        return pl.BlockSpec((tm, width), lambda i: (i, 0))

    if transposed_kv:
        nt = seq_len // tm
        kv_spec = pl.BlockSpec((None, sb, tm), lambda i: (i // nt, 0, i % nt))
        kv_f32 = jax.ShapeDtypeStruct((n_tok // seq_len, sb, seq_len), F32)
        kv_bf16 = jax.ShapeDtypeStruct((n_tok // seq_len, sb, seq_len), BF16)
        kv_specs = [kv_spec] * 4
        kv_bytes = 2 * 2 * sb * tm * (4 + 2)
    else:
        hm_spec = pl.BlockSpec((tm * SB_HEADS, SB_HEAD_DIM), lambda i: (i, 0))
        kv_f32 = jax.ShapeDtypeStruct((n_tok * SB_HEADS, SB_HEAD_DIM), F32)
        kv_bf16 = jax.ShapeDtypeStruct((n_tok, sb), BF16)
        kv_specs = [hm_spec, hm_spec, tok(sb), tok(sb)]
        kv_bytes = 2 * 2 * tm * (SB_HEADS * LANES * 4 + sb * 2)
    out_specs = [tok(gm), tok(sb)] + kv_specs + [tok(n_gate)]
    out_shape = [jax.ShapeDtypeStruct((n_tok, gm), BF16), jax.ShapeDtypeStruct((n_tok, sb), BF16),
                 kv_f32, kv_f32, kv_bf16, kv_bf16, jax.ShapeDtypeStruct((n_tok, n_gate), BF16)]
    if emit_va:
        out_specs.append(tok(gm))
        out_shape.append(jax.ShapeDtypeStruct((n_tok, gm), F32))

    weight_bytes = (win.size + wk.size + wv.size + wg.size + ws.size) * 2
    block_bytes = 2 * tm * (d * 4 + gm * 2 + sb * 2 + n_gate * 2 + gm * 4) + kv_bytes
    scratch_bytes = tm * (d * 2 + gm * 4 + gm * 4 + gm * 2)
    return pl.pallas_call(
        functools.partial(_proj_kernel, sub=sub, n_sub=n_sub, chunk=chunk, transposed_kv=transposed_kv,
                          emit_va=emit_va),
        grid=(n_tok // tm,),
        in_specs=[tok(d), _full(gpre.shape), _full(win.shape), _full(wk.shape), _full(wv.shape),
                  _full(lng.shape), _full(lnb.shape), _full(ws.shape), _full(bias.shape), _full(wg.shape),
                  _full(bg.shape)],
        out_specs=out_specs,
        out_shape=out_shape,
        scratch_shapes=[pltpu.VMEM((n_sub, sub, d), BF16), pltpu.VMEM((n_sub, sub, gm), F32),
                        pltpu.VMEM((n_sub, sub, gm), F32), pltpu.VMEM((n_sub, sub, gm), BF16)],
        compiler_params=_params(1, weight_bytes + block_bytes + 2 * scratch_bytes),
        name="proj",
    )(x2d, gpre, win, wk, wv, lng, lnb, ws, bias, wg, bg)


class _Tile(NamedTuple):
    head: int
    scores: Callable
    values: Callable
    v_transposed: bool
    tri: jax.Array
    mask: Optional[jax.Array]
    first: bool


def _sb_tiles(tiles, c_ref, cmin_ref, acc_ref):
    n = len(tiles)
    z, sp, cum, weight = ([None] * n for _ in range(4))

    def scores(i):
        z[i] = tiles[i].scores()

    def softplus(i):
        neg_abs = pltpu.bitcast(pltpu.bitcast(z[i], jnp.uint32) | jnp.uint32(SIGN_BIT), F32)
        s = jnp.maximum(z[i], 0.0) + jnp.log2(1.0 + jnp.exp2(neg_abs))
        sp[i] = s if tiles[i].mask is None else jnp.where(tiles[i].mask, s, 0.0)

    def cumsum(i):
        cum[i] = _dot(sp[i].astype(BF16), tiles[i].tri)

    def weights(i):
        h = tiles[i].head
        total = sp[i] + cum[i]
        if not tiles[i].first:
            total = total + c_ref[h]
        a = jnp.exp2(z[i] - total)
        if tiles[i].mask is not None:
            a = jnp.where(tiles[i].mask, a, 0.0)
        weight[i] = a.astype(BF16)
        c_ref[h] = total[:, 0:1]
        tq, width = total.shape[0], min(LANES, total.shape[1])
        cmin_ref[h, :, 0:width] = jnp.min(total[:, 0:width].reshape(tq // SUBLANES, SUBLANES, width), axis=0)
        z[i] = sp[i] = cum[i] = None

    def values(i):
        v = tiles[i].values()
        contrib = _dot_nt(weight[i], v) if tiles[i].v_transposed else _dot(weight[i], v)
        h = tiles[i].head
        lanes = _pair_lanes(h)
        if tiles[i].first and h % HEADS_PER_LANE_TILE == 0:
            acc_ref[:, lanes] = contrib
        else:
            acc_ref[:, lanes] = acc_ref[:, lanes] + contrib
        weight[i] = None

    stages = (scores, softplus, cumsum, weights, values)
    for step in range(n + len(stages) - 1):
        for s, stage in enumerate(stages):
            if 0 <= step - s < n:
                stage(step - s)


def _pair_lanes(h):
    return slice(h // HEADS_PER_LANE_TILE * LANES, (h // HEADS_PER_LANE_TILE + 1) * LANES)


def _strict_tri(n):
    r = lax.broadcasted_iota(jnp.int32, (n, n), 0)
    s = lax.broadcasted_iota(jnp.int32, (n, n), 1)
    return jnp.where(r > s, 1.0, 0.0).astype(BF16)


def _causal_mask(n):
    r = lax.broadcasted_iota(jnp.int32, (n, n), 0)
    s = lax.broadcasted_iota(jnp.int32, (n, n), 1)
    return s < r


def _head_lane_masks():
    lane = lax.broadcasted_iota(jnp.int32, (1, LANES), 1)
    return [(lane >= hh * SB_HEAD_DIM) & (lane < (hh + 1) * SB_HEAD_DIM) for hh in range(HEADS_PER_LANE_TILE)]


def _head_row_masks():
    row = lax.broadcasted_iota(jnp.int32, (LANES, 1), 0)
    return [(row >= hh * SB_HEAD_DIM) & (row < (hh + 1) * SB_HEAD_DIM) for hh in range(HEADS_PER_LANE_TILE)]


def _masked_queries(q_ref, qm_s, heads):
    lane_masks = _head_lane_masks()
    for h in range(heads):
        lt = h // HEADS_PER_LANE_TILE
        qt = q_ref[:, lt * LANES:(lt + 1) * LANES]
        qm_s[h] = jnp.where(lane_masks[h % HEADS_PER_LANE_TILE], qt, jnp.zeros_like(qt))


def _sweep_is_live(cmin_ref):
    return (jnp.min(cmin_ref[:, :, 0:1]) < F32_EXP2_UNDERFLOW).astype(jnp.int32)


def _sb_prompt_kernel(q_ref, kt_ref, vt_ref, o_ref, acc_s, c_s, cmin_s, u_s, qm_s, *, tq, heads):
    i = pl.program_id(2)

    @pl.when(i == 0)
    def _():
        u_s[...] = _strict_tri(tq)

    _masked_queries(q_ref, qm_s, heads)
    row_masks = _head_row_masks()

    def tiles(j, diagonal):
        keys = pl.ds(pl.multiple_of(j * tq, tq), tq)
        mask = _causal_mask(tq) if diagonal else None

        def scores(h):
            return _dot(qm_s[h], kt_ref[_pair_lanes(h), keys])

        def values(h):
            vt = vt_ref[_pair_lanes(h), keys]
            return jnp.where(row_masks[h % HEADS_PER_LANE_TILE], vt, jnp.zeros_like(vt))

        return [_Tile(h, functools.partial(scores, h), functools.partial(values, h), True, u_s[...], mask,
                      first=diagonal) for h in range(heads)]

    @pl.when(i == 0)
    def _():
        _sb_tiles(tiles(i, True), c_s, cmin_s, acc_s)

    @pl.when(i > 0)
    def _():
        _sb_tiles(tiles(i, True) + tiles(i - 1, False), c_s, cmin_s, acc_s)

    def cond(carry):
        n, go = carry
        return jnp.logical_and(n < i - 1, go > 0)

    def body(carry):
        n, _ = carry
        _sb_tiles(tiles(i - 2 - n, False), c_s, cmin_s, acc_s)
        return n + 1, _sweep_is_live(cmin_s)

    lax.while_loop(cond, body, (jnp.int32(0), _sweep_is_live(cmin_s)))
    o_ref[...] = acc_s[...].astype(BF16)


def _sb_prompt_call(q, kt, vt, *, tq, heads):
    b, t, w = q.shape
    width = heads * SB_HEAD_DIM
    qspec = pl.BlockSpec((None, tq, width), lambda bi, hg, i: (bi, i, hg))
    kvspec = pl.BlockSpec((None, width, t), lambda bi, hg, i: (bi, hg, 0))
    vmem = (2 * 2 * t * width * 2 + 2 * 2 * tq * width * 2 + tq * width * 4 + heads * tq * LANES * 4
            + 2 * tq * tq * 2 + heads * tq * LANES * 2 + heads * 12 * tq * tq * 4)
    return pl.pallas_call(
        functools.partial(_sb_prompt_kernel, tq=tq, heads=heads),
        grid=(b, w // width, t // tq),
        in_specs=[qspec, kvspec, kvspec],
        out_specs=qspec,
        out_shape=jax.ShapeDtypeStruct((b, t, w), BF16),
        scratch_shapes=[pltpu.VMEM((tq, width), F32),
                        pltpu.VMEM((heads, tq, 1), F32),
                        pltpu.VMEM((heads, SUBLANES, LANES), F32),
                        pltpu.VMEM((tq, tq), BF16),
                        pltpu.VMEM((heads, tq, LANES), BF16)],
        compiler_params=_params(3, vmem),
        name="sb_prompt",
    )(q, kt, vt)


def _sb_sample_kernel(q_ref, kn_ref, vn_ref, kp_hbm, vp_hbm, o_ref, kbuf, vbuf, sem, acc_s, c_s, cmin_s, u_s, ud_s,
                      qm_s, *, tq, tk, n_past):
    b = pl.program_id(0)

    def copies(j, slot):
        keys = pl.ds(pl.multiple_of(j * tk, tk), tk)
        return (pltpu.make_async_copy(kp_hbm.at[b, :, keys], kbuf.at[slot], sem.at[0, slot]),
                pltpu.make_async_copy(vp_hbm.at[b, :, keys], vbuf.at[slot], sem.at[1, slot]))

    def start(j, slot):
        for cp in copies(j, slot):
            cp.start()

    def wait(j, slot):
        for cp in copies(j, slot):
            cp.wait()

    start(n_past - 1, 0)
    u_s[...] = _strict_tri(tk)
    ud_s[...] = _strict_tri(tq)
    _masked_queries(q_ref, qm_s, SB_HEADS)
    lane_masks = _head_lane_masks()
    row_masks = _head_row_masks()

    def z_new(h):
        return _dot_nt(qm_s[h], kn_ref[:, _pair_lanes(h)])

    def v_new(h):
        vn = vn_ref[:, _pair_lanes(h)]
        return jnp.where(lane_masks[h % HEADS_PER_LANE_TILE], vn, jnp.zeros_like(vn))

    mask = _causal_mask(tq)
    _sb_tiles([_Tile(h, functools.partial(z_new, h), functools.partial(v_new, h), False, ud_s[...], mask,
                     first=True) for h in range(SB_HEADS)], c_s, cmin_s, acc_s)

    def cond(carry):
        n, go = carry
        return jnp.logical_and(n < n_past, go > 0)

    def body(carry):
        n, _ = carry
        j = n_past - 1 - n
        slot = lax.rem(n, 2)
        wait(j, slot)

        @pl.when(n + 1 < n_past)
        def _():
            start(j - 1, 1 - slot)

        def z_past(h):
            return _dot(qm_s[h], kbuf[slot, _pair_lanes(h), :].astype(BF16))

        def v_past(h):
            vt = vbuf[slot, _pair_lanes(h), :].astype(BF16)
            return jnp.where(row_masks[h % HEADS_PER_LANE_TILE], vt, jnp.zeros_like(vt))

        _sb_tiles([_Tile(h, functools.partial(z_past, h), functools.partial(v_past, h), True, u_s[...], None,
                         first=False) for h in range(SB_HEADS)], c_s, cmin_s, acc_s)
        return n + 1, _sweep_is_live(cmin_s)

    n_done, _ = lax.while_loop(cond, body, (jnp.int32(0), _sweep_is_live(cmin_s)))

    @pl.when(n_done < n_past)
    def _():
        wait(n_past - 1 - n_done, lax.rem(n_done, 2))

    o_ref[...] = acc_s[...].astype(BF16)


def _sb_sample_call(q, k_new, v_new, kt_past, vt_past, *, tk):
    b, tq, w = q.shape
    p = kt_past.shape[2]
    newspec = pl.BlockSpec((None, tq, w), lambda bi: (bi, 0, 0))
    anyspec = pl.BlockSpec(memory_space=pl.ANY)
    vmem = (2 * 2 * w * tk * 4 + 2 * 4 * tq * w * 2 + tq * w * 4 + SB_HEADS * tq * LANES * 4
            + 2 * tk * tk * 2 + SB_HEADS * tq * LANES * 2 + SB_HEADS * 12 * tq * tk * 4)
    return pl.pallas_call(
        functools.partial(_sb_sample_kernel, tq=tq, tk=tk, n_past=p // tk),
        grid=(b,),
        in_specs=[newspec, newspec, newspec, anyspec, anyspec],
        out_specs=newspec,
        out_shape=jax.ShapeDtypeStruct((b, tq, w), BF16),
        scratch_shapes=[pltpu.VMEM((2, w, tk), F32),
                        pltpu.VMEM((2, w, tk), F32),
                        pltpu.SemaphoreType.DMA((2, 2)),
                        pltpu.VMEM((tq, w), F32),
                        pltpu.VMEM((SB_HEADS, tq, 1), F32),
                        pltpu.VMEM((SB_HEADS, SUBLANES, LANES), F32),
                        pltpu.VMEM((tk, tk), BF16),
                        pltpu.VMEM((tq, tq), BF16),
                        pltpu.VMEM((SB_HEADS, tq, LANES), BF16)],
        compiler_params=_params(1, vmem),
        name="sb_sample",
    )(q, k_new, v_new, kt_past, vt_past)


def _merge_kernel(x_ref, ya_ref, yb_ref, gate_ref, wa_ref, wb_ref, wo_ref, gpost_ref, o_ref, m_s, p_s,
                  *, sub, n_sub):
    d = x_ref.shape[1]

    def stages_of(i):
        tok = slice(i * sub, (i + 1) * sub)

        def merge(n):
            cols = slice(n * MXU_COLS, (n + 1) * MXU_COLS)
            gcols = slice(d + n * MXU_COLS, d + (n + 1) * MXU_COLS)
            merged = (gate_ref[tok, cols].astype(F32) * _dot(ya_ref[tok, :], wa_ref[:, cols])
                      + gate_ref[tok, gcols].astype(F32) * _dot(yb_ref[tok, :], wb_ref[:, cols]))
            m_s[i, :, cols] = merged.astype(BF16)

        def project(n):
            cols = slice(n * MXU_COLS, (n + 1) * MXU_COLS)
            p_s[i, :, cols] = _dot(m_s[i], wo_ref[:, cols])

        def norm_and_store():
            o_ref[tok, :] = x_ref[tok, :] + _rms(p_s[i], gpost_ref[...])

        n_chunks = d // MXU_COLS
        return ([functools.partial(merge, n) for n in range(n_chunks)]
                + [functools.partial(project, n) for n in range(n_chunks)] + [norm_and_store])

    _skewed([stages_of(i) for i in range(n_sub)], skew=2)


def _merge_call(x2d, ya, yb, gates, wa, wb, wo, gpost, *, sub, n_sub):
    n_tok, d = x2d.shape
    tm = sub * n_sub

    def tok(width):
        return pl.BlockSpec((tm, width), lambda i: (i, 0))

    weight_bytes = (wa.size + wb.size + wo.size) * 2
    block_bytes = 2 * tm * (d * 4 + ya.shape[1] * 2 + yb.shape[1] * 2 + gates.shape[1] * 2 + d * 4)
    scratch_bytes = tm * d * 6
    return pl.pallas_call(
        functools.partial(_merge_kernel, sub=sub, n_sub=n_sub),
        grid=(n_tok // tm,),
        in_specs=[tok(d), tok(ya.shape[1]), tok(yb.shape[1]), tok(gates.shape[1]),
                  _full(wa.shape), _full(wb.shape), _full(wo.shape), _full(gpost.shape)],
        out_specs=tok(d),
        out_shape=jax.ShapeDtypeStruct((n_tok, d), F32),
        scratch_shapes=[pltpu.VMEM((n_sub, sub, d), BF16), pltpu.VMEM((n_sub, sub, d), F32)],
        compiler_params=_params(1, weight_bytes + block_bytes + 2 * scratch_bytes),
        name="merge",
    )(x2d, ya, yb, gates, wa, wb, wo, gpost)


def _skewed(stage_lists, skew):
    n_stages = len(stage_lists[0])
    for step in range(n_stages + skew * (len(stage_lists) - 1)):
        for i, stages in enumerate(stage_lists):
            if 0 <= step - i * skew < n_stages:
                stages[step - i * skew]()


def _ffn_kernel(x_ref, prev_ref, gpre_ref, wup_ref, wc_ref, bc_ref, wdown_ref, gpost_ref,
                y_ref, newconv_ref, xp_s, hn_s, carry_s, act_s, f_s, *, sub, n_sub):
    assert CONV_WIDTH == 3
    dff = wdown_ref.shape[0]
    d = x_ref.shape[1]
    ng = sub // SUBLANES
    assert ng >= CONV_WIDTH - 1 and ng % SUBLANES == 0
    n_slabs = d // LANES
    t = pl.program_id(1)

    @pl.when(t == 0)
    def _():
        carry_s[0:CONV_WIDTH - 1, :] = prev_ref[...]

    top = lax.broadcasted_iota(jnp.int32, (SUBLANES, MXU_COLS), 0) == 0

    def permuted_rows(r):
        return pl.ds((SUBLANES * r % ng) * SUBLANES + SUBLANES * r // ng, SUBLANES, stride=SUBLANES)

    def stages_of(i):
        base = i * sub

        def load_and_norm():
            for r in range(ng):
                rows = slice(base + r * SUBLANES, base + (r + 1) * SUBLANES)
                for c in range(n_slabs):
                    xp_s[i, c, permuted_rows(r), :] = x_ref[rows, c * LANES:(c + 1) * LANES]
            xp = jnp.concatenate([xp_s[i, c] for c in range(n_slabs)], axis=1)
            hn_s[i] = _rms(xp, gpre_ref[...]).astype(BF16)

        def up_conv_act(n):
            halves = []
            for off in (0, dff):
                cols = slice(off + n * MXU_COLS, off + (n + 1) * MXU_COLS)
                up = _dot(hn_s[i], wup_ref[:, cols])
                back1 = jnp.where(top, carry_s[1:2, cols], pltpu.roll(up[sub - SUBLANES:sub], 1, axis=0))
                back2 = jnp.where(top, carry_s[0:1, cols],
                                  pltpu.roll(up[sub - 2 * SUBLANES:sub - SUBLANES], 1, axis=0))
                carry_s[0:1, cols] = up[sub - SUBLANES - 1:sub - SUBLANES]
                carry_s[1:2, cols] = up[sub - 1:sub]
                prev1 = jnp.concatenate([back1, up[:sub - SUBLANES]], axis=0)
                prev2 = jnp.concatenate([back2, back1, up[:sub - 2 * SUBLANES]], axis=0)
                halves.append(bc_ref[:, cols] + wc_ref[0:1, cols] * prev2 + wc_ref[1:2, cols] * prev1
                              + wc_ref[2:3, cols] * up)
            act_s[i, :, n * MXU_COLS:(n + 1) * MXU_COLS] = (jax.nn.gelu(halves[0]) * halves[1]).astype(BF16)

        def down(n):
            cols = slice(n * MXU_COLS, (n + 1) * MXU_COLS)
            f_s[i, :, cols] = _dot(act_s[i], wdown_ref[:, cols])

        def norm_and_store():
            normed = _rms(f_s[i], gpost_ref[...])
            for c in range(n_slabs):
                xp_s[i, c] = xp_s[i, c] + normed[:, c * LANES:(c + 1) * LANES]
            for r in range(ng):
                rows = slice(base + r * SUBLANES, base + (r + 1) * SUBLANES)
                for c in range(n_slabs):
                    y_ref[rows, c * LANES:(c + 1) * LANES] = xp_s[i, c, permuted_rows(r), :]
            if i == n_sub - 1:
                newconv_ref[...] = carry_s[0:CONV_WIDTH - 1, :]

        return ([load_and_norm] + [functools.partial(up_conv_act, n) for n in range(dff // MXU_COLS)]
                + [functools.partial(down, n) for n in range(d // MXU_COLS)] + [norm_and_store])

    _skewed([stages_of(i) for i in range(n_sub)], skew=2)


def _ffn_call(x, conv_prev, gpre, wup, wc, bc, wdown, gpost, *, sub, n_sub):
    b, t, d = x.shape
    dff = wdown.shape[0]
    tm = sub * n_sub
    xspec = pl.BlockSpec((None, tm, d), lambda bi, ti: (bi, ti, 0))
    cspec = pl.BlockSpec((None, CONV_WIDTH - 1, 2 * dff), lambda bi, ti: (bi, 0, 0))
    weight_bytes = (wup.size + wdown.size) * 2
    block_bytes = 2 * 2 * tm * d * 4
    scratch_bytes = tm * d * 4 + tm * d * 2 + SUBLANES * 2 * dff * 4 + tm * dff * 2 + tm * d * 4
    return pl.pallas_call(
        functools.partial(_ffn_kernel, sub=sub, n_sub=n_sub),
        grid=(b, t // tm),
        in_specs=[xspec, cspec, _full(gpre.shape), _full(wup.shape), _full(wc.shape), _full(bc.shape),
                  _full(wdown.shape), _full(gpost.shape)],
        out_specs=[xspec, cspec],
        out_shape=[jax.ShapeDtypeStruct((b, t, d), F32),
                   jax.ShapeDtypeStruct((b, CONV_WIDTH - 1, 2 * dff), F32)],
        scratch_shapes=[pltpu.VMEM((n_sub, d // LANES, sub, LANES), F32),
                        pltpu.VMEM((n_sub, sub, d), BF16),
                        pltpu.VMEM((SUBLANES, 2 * dff), F32),
                        pltpu.VMEM((n_sub, sub, dff), BF16),
                        pltpu.VMEM((n_sub, sub, d), F32)],
        compiler_params=_params(2, weight_bytes + block_bytes + 2 * scratch_bytes),
        name="ffn",
    )(x, conv_prev, gpre, wup, wc, bc, wdown, gpost)


def _layer(x, past_k, past_v, conv_prev, w):
    b, t, d = x.shape
    gm = GM_GROUPS * GM_GROUP_DIM
    sbw = SB_HEADS * SB_HEAD_DIM
    chunk = min(t, GM_CHUNK)
    n_tok = b * t
    decode = past_k is not None
    tm = min(256, t)
    n_sub = min(SUB_BLOCKS_PER_STEP, t // tm)
    assert t % (tm * n_sub) == 0 and tm % chunk == 0, (t, tm, n_sub, chunk)

    tri = jnp.tril(jnp.ones((chunk, chunk), dtype=bool))
    ws = jnp.where(tri[None], w["w_s"][:, :chunk, :chunk], 0.0).astype(BF16)
    bias = jnp.repeat(w["b_s"][:, :chunk].T, GM_GROUP_DIM, axis=1)
    w_in = w["w_in"]
    w_uvq = w_in[:, :2 * gm + sbw]
    w_k = w_in[:, 2 * gm + sbw:2 * gm + 2 * sbw]
    w_v = w_in[:, 2 * gm + 2 * sbw:]
    if not decode:
        w_k, w_v = w_k.T, w_v.T

    outs = _proj_call(
        x.reshape(n_tok, d), w["g_pre_mix"], w_uvq, w_k, w_v, w["ln_v_g"], w["ln_v_b"], ws, bias,
        w["w_gate"], w["b_gate"], sub=tm, n_sub=n_sub, chunk=chunk, seq_len=t, transposed_kv=not decode,
        emit_va=decode)
    ya, q, k_out, v_out, kb, vb, gates = outs[:7]
    q3 = q.reshape(b, t, sbw)

    if decode:
        p = past_k.shape[1]
        kt_past = jnp.transpose(past_k, (0, 2, 3, 1)).reshape(b, sbw, p)
        vt_past = jnp.transpose(past_v, (0, 2, 3, 1)).reshape(b, sbw, p)
        yb = _sb_sample_call(q3, kb.reshape(b, t, sbw), vb.reshape(b, t, sbw), kt_past, vt_past, tk=min(p, 256))
        k4 = k_out.reshape(b, t, SB_HEADS, SB_HEAD_DIM)
        v4 = v_out.reshape(b, t, SB_HEADS, SB_HEAD_DIM)
        va = outs[7].reshape(b, t, gm)
    else:
        yb = _sb_prompt_call(q3, kb, vb, tq=tm, heads=SB_PROMPT_HEADS_PER_STEP)
        k4 = jnp.transpose(k_out.reshape(b, SB_HEADS, SB_HEAD_DIM, t), (0, 3, 1, 2))
        v4 = jnp.transpose(v_out.reshape(b, SB_HEADS, SB_HEAD_DIM, t), (0, 3, 1, 2))
        va = None

    wide = min(2 * tm, t // n_sub)
    x1 = _merge_call(x.reshape(n_tok, d), ya, yb.reshape(n_tok, sbw), gates,
                     w["w_branch_a"], w["w_branch_b"], w["w_o"], w["g_post_mix"], sub=wide, n_sub=n_sub)
    y, conv_new = _ffn_call(x1.reshape(b, t, d), conv_prev, w["g_pre_ffn"], w["w_up"], w["w_conv"],
                            w["b_conv"], w["w_down"], w["g_post_ffn"], sub=wide, n_sub=n_sub)
    return y, k4, v4, conv_new, va


def kernel(x_prompt, x_sample, cache_sb_k, cache_sb_v, state_ffn_conv, g_pre_mix, w_in, ln_v_g, ln_v_b, w_s, b_s,
           w_gate, b_gate, w_branch_a, w_branch_b, w_o, g_post_mix, g_pre_ffn, w_up, w_conv, b_conv, w_down,
           g_post_ffn):
    depth = w_in.shape[0]
    y_p, y_s = x_prompt, x_sample
    outs = [[] for _ in range(7)]
    for l in range(depth):
        row = lambda a: a[l][None, :]
        w = dict(
            g_pre_mix=row(g_pre_mix), w_in=w_in[l].astype(BF16), ln_v_g=row(ln_v_g), ln_v_b=row(ln_v_b),
            w_s=w_s[l], b_s=b_s[l], w_gate=w_gate[l].astype(BF16), b_gate=row(b_gate),
            w_branch_a=w_branch_a[l].astype(BF16), w_branch_b=w_branch_b[l].astype(BF16),
            w_o=w_o[l].astype(BF16), g_post_mix=row(g_post_mix), g_pre_ffn=row(g_pre_ffn),
            w_up=w_up[l].astype(BF16), w_conv=w_conv[l], b_conv=row(b_conv),
            w_down=w_down[l].astype(BF16), g_post_ffn=row(g_post_ffn))
        zero_conv = jnp.zeros((y_p.shape[0], CONV_WIDTH - 1, w_up.shape[2]), y_p.dtype)
        y_p, kp, vp, cp, _ = _layer(y_p, None, None, zero_conv, w)
        y_s, ks, vs, cs, gvs = _layer(y_s, cache_sb_k[l], cache_sb_v[l], state_ffn_conv[l], w)
        for lst, a in zip(outs, (kp, vp, cp, ks, vs, cs, gvs)):
            lst.append(a)
    return (y_p, y_s) + tuple(jnp.stack(o) for o in outs)
```

```python
import functools
import math
from typing import Callable, NamedTuple, Optional

import jax
import jax.numpy as jnp
from jax import lax
from jax.experimental import pallas as pl
from jax.experimental.pallas import tpu as pltpu

F32 = jnp.float32
BF16 = jnp.bfloat16

EPS = 1e-6
SB_HEADS = 16
SB_HEAD_DIM = 64
GM_GROUPS = 8
GM_GROUP_DIM = 128
GM_CHUNK = 128
CONV_WIDTH = 3

LANES = 128
SUBLANES = 8
MXU_COLS = 256
HEADS_PER_LANE_TILE = LANES // SB_HEAD_DIM
HEADS_PER_MXU_TILE = MXU_COLS // SB_HEAD_DIM
SUB_BLOCKS_PER_STEP = 2
SB_PROMPT_HEADS_PER_STEP = 16
V7X_SCOPED_VMEM_BYTES = 60000 * 1024

LOG2E = math.log2(math.e)
F32_EXP2_UNDERFLOW = 150.0
SIGN_BIT = 0x80000000

_ARB = "arbitrary"
_NT = (((1,), (1,)), ((), ()))


def _rms(x, g):
    return x * lax.rsqrt(jnp.mean(x * x, axis=-1, keepdims=True) + EPS) * g


def _dot(a, b):
    return jnp.dot(a, b, preferred_element_type=F32)


def _dot_nt(a, b):
    return lax.dot_general(a, b, _NT, preferred_element_type=F32)


def _params(n_grid, vmem_bytes):
    return pltpu.CompilerParams(
        dimension_semantics=(_ARB,) * n_grid,
        vmem_limit_bytes=int(min(vmem_bytes, V7X_SCOPED_VMEM_BYTES)))


def _full(shape):
    nd = len(shape)
    return pl.BlockSpec(shape, lambda *_: (0,) * nd, pipeline_mode=pl.Buffered(1))


def _proj_kernel(x_ref, gpre_ref, win_ref, wk_ref, wv_ref, lng_ref, lnb_ref, ws_ref, bias_ref, wg_ref, bg_ref,
                 ya_ref, q_ref, k_ref, v_ref, kb_ref, vb_ref, gate_ref, *rest,
                 sub, n_sub, chunk, transposed_kv, emit_va):
    if emit_va:
        va_ref, hb_s, u_s, gv_s, vab_s = rest
    else:
        hb_s, u_s, gv_s, vab_s = rest
    gm = GM_GROUPS * GM_GROUP_DIM
    sb = SB_HEADS * SB_HEAD_DIM
    qscale = SB_HEAD_DIM ** -0.5 * LOG2E
    nch = sub // chunk

    def stages_of(i):
        base = i * sub
        tok = slice(base, base + sub)

        def norm():
            hb_s[i] = _rms(x_ref[tok, :], gpre_ref[...]).astype(BF16)

        def proj(col0, n):
            cols = slice(col0 + n * MXU_COLS, col0 + (n + 1) * MXU_COLS)
            return _dot(hb_s[i], win_ref[:, cols])

        def gating_inputs(n):
            oc = slice(n * MXU_COLS, (n + 1) * MXU_COLS)
            u_s[i, :, oc] = jax.nn.gelu(proj(0, n))
            gv_s[i, :, oc] = jax.nn.gelu(proj(gm, n))

        def layer_norm():
            gv = gv_s[i]
            mu = jnp.mean(gv, axis=-1, keepdims=True)
            dev = gv - mu
            var = jnp.mean(dev * dev, axis=-1, keepdims=True)
            va = dev * lax.rsqrt(var + EPS) * lng_ref[...] + lnb_ref[...]
            if emit_va:
                va_ref[tok, :] = va
            vab_s[i] = va.astype(BF16)

        def spatial_gating(g):
            cols = slice(g * GM_GROUP_DIM, (g + 1) * GM_GROUP_DIM)
            pieces = [vab_s[i, c * chunk:(c + 1) * chunk, cols] for c in range(nch)]
            mixed = _dot(ws_ref[g], pieces[0] if nch == 1 else jnp.concatenate(pieces, axis=1))
            for c in range(nch):
                rows = slice(c * chunk, (c + 1) * chunk)
                m = mixed[:, c * GM_GROUP_DIM:(c + 1) * GM_GROUP_DIM] + bias_ref[:, cols]
                ya_ref[base + c * chunk:base + (c + 1) * chunk, cols] = (u_s[i, rows, cols] * m).astype(BF16)

        def queries(n):
            q_ref[tok, n * MXU_COLS:(n + 1) * MXU_COLS] = (proj(2 * gm, n) * qscale).astype(BF16)

        def keys_or_values(w_ref, f32_out, bf16_out, n):
            oc = slice(n * MXU_COLS, (n + 1) * MXU_COLS)
            if transposed_kv:
                pt = _dot_nt(w_ref[oc, :], hb_s[i])
                f32_out[oc, tok] = pt
                bf16_out[oc, tok] = pt.astype(BF16)
            else:
                p = _dot(hb_s[i], w_ref[:, oc])
                bf16_out[tok, oc] = p.astype(BF16)
                for hs in range(HEADS_PER_MXU_TILE):
                    h = n * HEADS_PER_MXU_TILE + hs
                    f32_out[pl.ds(base * SB_HEADS + h, sub, stride=SB_HEADS), :] = (
                        p[:, hs * SB_HEAD_DIM:(hs + 1) * SB_HEAD_DIM])

        def gates(n):
            oc = slice(n * MXU_COLS, (n + 1) * MXU_COLS)
            gate_ref[tok, oc] = jax.nn.sigmoid(_dot(hb_s[i], wg_ref[:, oc]) + bg_ref[:, oc]).astype(BF16)

        part = functools.partial
        n_sb = sb // MXU_COLS
        return ([norm] + [part(gating_inputs, n) for n in range(gm // MXU_COLS)]
                + [part(queries, n) for n in range(n_sb)] + [layer_norm]
                + [part(keys_or_values, wk_ref, k_ref, kb_ref, n) for n in range(n_sb)]
                + [part(spatial_gating, g) for g in range(GM_GROUPS)]
                + [part(keys_or_values, wv_ref, v_ref, vb_ref, n) for n in range(n_sb)]
                + [part(gates, n) for n in range(gate_ref.shape[1] // MXU_COLS)])

    _skewed([stages_of(i) for i in range(n_sub)], skew=2)


def _proj_call(x2d, gpre, win, wk, wv, lng, lnb, ws, bias, wg, bg, *, sub, n_sub, chunk, seq_len, transposed_kv,
               emit_va):
    n_tok, d = x2d.shape
    gm = GM_GROUPS * GM_GROUP_DIM
    sb = SB_HEADS * SB_HEAD_DIM
    n_gate = wg.shape[1]
    tm = sub * n_sub

    def tok(width):
        return pl.BlockSpec((tm, width), lambda i: (i, 0))

    if transposed_kv:
        nt = seq_len // tm
        kv_spec = pl.BlockSpec((None, sb, tm), lambda i: (i // nt, 0, i % nt))
        kv_f32 = jax.ShapeDtypeStruct((n_tok // seq_len, sb, seq_len), F32)
        kv_bf16 = jax.ShapeDtypeStruct((n_tok // seq_len, sb, seq_len), BF16)
        kv_specs = [kv_spec] * 4
        kv_bytes = 2 * 2 * sb * tm * (4 + 2)
    else:
        hm_spec = pl.BlockSpec((tm * SB_HEADS, SB_HEAD_DIM), lambda i: (i, 0))
        kv_f32 = jax.ShapeDtypeStruct((n_tok * SB_HEADS, SB_HEAD_DIM), F32)
        kv_bf16 = jax.ShapeDtypeStruct((n_tok, sb), BF16)
        kv_specs = [hm_spec, hm_spec, tok(sb), tok(sb)]
        kv_bytes = 2 * 2 * tm * (SB_HEADS * LANES * 4 + sb * 2)
    out_specs = [tok(gm), tok(sb)] + kv_specs + [tok(n_gate)]
    out_shape = [jax.ShapeDtypeStruct((n_tok, gm), BF16), jax.ShapeDtypeStruct((n_tok, sb), BF16),
                 kv_f32, kv_f32, kv_bf16, kv_bf16, jax.ShapeDtypeStruct((n_tok, n_gate), BF16)]
    if emit_va:
        out_specs.append(tok(gm))
        out_shape.append(jax.ShapeDtypeStruct((n_tok, gm), F32))

    weight_bytes = (win.size + wk.size + wv.size + wg.size + ws.size) * 2
    block_bytes = 2 * tm * (d * 4 + gm * 2 + sb * 2 + n_gate * 2 + gm * 4) + kv_bytes
    scratch_bytes = tm * (d * 2 + gm * 4 + gm * 4 + gm * 2)
    return pl.pallas_call(
        functools.partial(_proj_kernel, sub=sub, n_sub=n_sub, chunk=chunk, transposed_kv=transposed_kv,
                          emit_va=emit_va),
        grid=(n_tok // tm,),
        in_specs=[tok(d), _full(gpre.shape), _full(win.shape), _full(wk.shape), _full(wv.shape),
                  _full(lng.shape), _full(lnb.shape), _full(ws.shape), _full(bias.shape), _full(wg.shape),
                  _full(bg.shape)],
        out_specs=out_specs,
        out_shape=out_shape,
        scratch_shapes=[pltpu.VMEM((n_sub, sub, d), BF16), pltpu.VMEM((n_sub, sub, gm), F32),
                        pltpu.VMEM((n_sub, sub, gm), F32), pltpu.VMEM((n_sub, sub, gm), BF16)],
        compiler_params=_params(1, weight_bytes + block_bytes + 2 * scratch_bytes),
        name="proj",
    )(x2d, gpre, win, wk, wv, lng, lnb, ws, bias, wg, bg)


class _Tile(NamedTuple):
    head: int
    scores: Callable
    values: Callable
    v_transposed: bool
    tri: jax.Array
    mask: Optional[jax.Array]
    first: bool


def _sb_tiles(tiles, c_ref, cmin_ref, acc_ref):
    n = len(tiles)
    z, sp, cum, weight = ([None] * n for _ in range(4))

    def scores(i):
        z[i] = tiles[i].scores()

    def softplus(i):
        neg_abs = pltpu.bitcast(pltpu.bitcast(z[i], jnp.uint32) | jnp.uint32(SIGN_BIT), F32)
        s = jnp.maximum(z[i], 0.0) + jnp.log2(1.0 + jnp.exp2(neg_abs))
        sp[i] = s if tiles[i].mask is None else jnp.where(tiles[i].mask, s, 0.0)

    def cumsum(i):
        cum[i] = _dot(sp[i].astype(BF16), tiles[i].tri)

    def weights(i):
        h = tiles[i].head
        total = sp[i] + cum[i]
        if not tiles[i].first:
            total = total + c_ref[h]
        a = jnp.exp2(z[i] - total)
        if tiles[i].mask is not None:
            a = jnp.where(tiles[i].mask, a, 0.0)
        weight[i] = a.astype(BF16)
        c_ref[h] = total[:, 0:1]
        tq, width = total.shape[0], min(LANES, total.shape[1])
        cmin_ref[h, :, 0:width] = jnp.min(total[:, 0:width].reshape(tq // SUBLANES, SUBLANES, width), axis=0)
        z[i] = sp[i] = cum[i] = None

    def values(i):
        v = tiles[i].values()
        contrib = _dot_nt(weight[i], v) if tiles[i].v_transposed else _dot(weight[i], v)
        h = tiles[i].head
        lanes = _pair_lanes(h)
        if tiles[i].first and h % HEADS_PER_LANE_TILE == 0:
            acc_ref[:, lanes] = contrib
        else:
            acc_ref[:, lanes] = acc_ref[:, lanes] + contrib
        weight[i] = None

    stages = (scores, softplus, cumsum, weights, values)
    for step in range(n + len(stages) - 1):
        for s, stage in enumerate(stages):
            if 0 <= step - s < n:
                stage(step - s)


def _pair_lanes(h):
    return slice(h // HEADS_PER_LANE_TILE * LANES, (h // HEADS_PER_LANE_TILE + 1) * LANES)


def _strict_tri(n):
    r = lax.broadcasted_iota(jnp.int32, (n, n), 0)
    s = lax.broadcasted_iota(jnp.int32, (n, n), 1)
    return jnp.where(r > s, 1.0, 0.0).astype(BF16)


def _causal_mask(n):
    r = lax.broadcasted_iota(jnp.int32, (n, n), 0)
    s = lax.broadcasted_iota(jnp.int32, (n, n), 1)
    return s < r


def _head_lane_masks():
    lane = lax.broadcasted_iota(jnp.int32, (1, LANES), 1)
    return [(lane >= hh * SB_HEAD_DIM) & (lane < (hh + 1) * SB_HEAD_DIM) for hh in range(HEADS_PER_LANE_TILE)]


def _head_row_masks():
    row = lax.broadcasted_iota(jnp.int32, (LANES, 1), 0)
    return [(row >= hh * SB_HEAD_DIM) & (row < (hh + 1) * SB_HEAD_DIM) for hh in range(HEADS_PER_LANE_TILE)]


def _masked_queries(q_ref, rows, qm_s, heads):
    lane_masks = _head_lane_masks()
    for h in range(heads):
        lt = h // HEADS_PER_LANE_TILE
        qt = q_ref[rows, lt * LANES:(lt + 1) * LANES]
        qm_s[h] = jnp.where(lane_masks[h % HEADS_PER_LANE_TILE], qt, jnp.zeros_like(qt))


def _sweep_is_live(cmin_ref):
    return (jnp.min(cmin_ref[:, :, 0:1]) < F32_EXP2_UNDERFLOW).astype(jnp.int32)


def _sb_prompt_kernel(q_ref, kt_ref, vt_ref, o_ref, acc_s, c_s, cmin_s, u_s, qm_s, *, tq, heads):
    n_blocks = q_ref.shape[0] // tq
    u_s[...] = _strict_tri(tq)
    row_masks = _head_row_masks()

    def tiles(j, diagonal):
        keys = pl.ds(pl.multiple_of(j * tq, tq), tq)
        mask = _causal_mask(tq) if diagonal else None

        def scores(h):
            return _dot(qm_s[h], kt_ref[_pair_lanes(h), keys])

        def values(h):
            vt = vt_ref[_pair_lanes(h), keys]
            return jnp.where(row_masks[h % HEADS_PER_LANE_TILE], vt, jnp.zeros_like(vt))

        return [_Tile(h, functools.partial(scores, h), functools.partial(values, h), True, u_s[...], mask,
                      first=diagonal) for h in range(heads)]

    def query_block(i, has_previous):
        rows = pl.ds(pl.multiple_of(i * tq, tq), tq)
        _masked_queries(q_ref, rows, qm_s, heads)
        if has_previous:
            _sb_tiles(tiles(i, True) + tiles(i - 1, False), c_s, cmin_s, acc_s)

            def cond(carry):
                n, go = carry
                return jnp.logical_and(n < i - 1, go > 0)

            def body(carry):
                n, _ = carry
                _sb_tiles(tiles(i - 2 - n, False), c_s, cmin_s, acc_s)
                return n + 1, _sweep_is_live(cmin_s)

            lax.while_loop(cond, body, (jnp.int32(0), _sweep_is_live(cmin_s)))
        else:
            _sb_tiles(tiles(i, True), c_s, cmin_s, acc_s)
        o_ref[rows, :] = acc_s[...].astype(BF16)

    query_block(0, False)

    def later_blocks(i, carry):
        query_block(i, True)
        return carry

    lax.fori_loop(1, n_blocks, later_blocks, 0)


def _sb_prompt_call(q, kt, vt, *, tq, heads):
    b, t, w = q.shape
    width = heads * SB_HEAD_DIM
    qspec = pl.BlockSpec((None, t, width), lambda bi, hg: (bi, 0, hg))
    kvspec = pl.BlockSpec((None, width, t), lambda bi, hg: (bi, hg, 0))
    vmem = (2 * 4 * t * width * 2 + tq * width * 4 + heads * tq * LANES * 4
            + tq * tq * 2 + heads * tq * LANES * 2 + heads * 12 * tq * tq * 4)
    return pl.pallas_call(
        functools.partial(_sb_prompt_kernel, tq=tq, heads=heads),
        grid=(b, w // width),
        in_specs=[qspec, kvspec, kvspec],
        out_specs=qspec,
        out_shape=jax.ShapeDtypeStruct((b, t, w), BF16),
        scratch_shapes=[pltpu.VMEM((tq, width), F32),
                        pltpu.VMEM((heads, tq, 1), F32),
                        pltpu.VMEM((heads, SUBLANES, LANES), F32),
                        pltpu.VMEM((tq, tq), BF16),
                        pltpu.VMEM((heads, tq, LANES), BF16)],
        compiler_params=_params(2, vmem),
        name="sb_prompt",
    )(q, kt, vt)


def _sb_sample_kernel(q_ref, kn_ref, vn_ref, kp_hbm, vp_hbm, o_ref, kbuf, vbuf, sem, acc_s, c_s, cmin_s, u_s, ud_s,
                      qm_s, *, tq, tk, n_past):
    b = pl.program_id(0)

    def copies(j, slot):
        keys = pl.ds(pl.multiple_of(j * tk, tk), tk)
        return (pltpu.make_async_copy(kp_hbm.at[b, :, keys], kbuf.at[slot], sem.at[0, slot]),
                pltpu.make_async_copy(vp_hbm.at[b, :, keys], vbuf.at[slot], sem.at[1, slot]))

    def start(j, slot):
        for cp in copies(j, slot):
            cp.start()

    def wait(j, slot):
        for cp in copies(j, slot):
            cp.wait()

    start(n_past - 1, 0)
    u_s[...] = _strict_tri(tk)
    ud_s[...] = _strict_tri(tq)
    _masked_queries(q_ref, slice(None), qm_s, SB_HEADS)
    lane_masks = _head_lane_masks()
    row_masks = _head_row_masks()

    def z_new(h):
        return _dot_nt(qm_s[h], kn_ref[:, _pair_lanes(h)])

    def v_new(h):
        vn = vn_ref[:, _pair_lanes(h)]
        return jnp.where(lane_masks[h % HEADS_PER_LANE_TILE], vn, jnp.zeros_like(vn))

    mask = _causal_mask(tq)
    _sb_tiles([_Tile(h, functools.partial(z_new, h), functools.partial(v_new, h), False, ud_s[...], mask,
                     first=True) for h in range(SB_HEADS)], c_s, cmin_s, acc_s)

    def cond(carry):
        n, go = carry
        return jnp.logical_and(n < n_past, go > 0)

    def body(carry):
        n, _ = carry
        j = n_past - 1 - n
        slot = lax.rem(n, 2)
        wait(j, slot)

        @pl.when(n + 1 < n_past)
        def _():
            start(j - 1, 1 - slot)

        def z_past(h):
            return _dot(qm_s[h], kbuf[slot, _pair_lanes(h), :].astype(BF16))

        def v_past(h):
            vt = vbuf[slot, _pair_lanes(h), :].astype(BF16)
            return jnp.where(row_masks[h % HEADS_PER_LANE_TILE], vt, jnp.zeros_like(vt))

        _sb_tiles([_Tile(h, functools.partial(z_past, h), functools.partial(v_past, h), True, u_s[...], None,
                         first=False) for h in range(SB_HEADS)], c_s, cmin_s, acc_s)
        return n + 1, _sweep_is_live(cmin_s)

    n_done, _ = lax.while_loop(cond, body, (jnp.int32(0), _sweep_is_live(cmin_s)))

    @pl.when(n_done < n_past)
    def _():
        wait(n_past - 1 - n_done, lax.rem(n_done, 2))

    o_ref[...] = acc_s[...].astype(BF16)


def _sb_sample_call(q, k_new, v_new, kt_past, vt_past, *, tk):
    b, tq, w = q.shape
    p = kt_past.shape[2]
    newspec = pl.BlockSpec((None, tq, w), lambda bi: (bi, 0, 0))
    anyspec = pl.BlockSpec(memory_space=pl.ANY)
    vmem = (2 * 2 * w * tk * 4 + 2 * 4 * tq * w * 2 + tq * w * 4 + SB_HEADS * tq * LANES * 4
            + 2 * tk * tk * 2 + SB_HEADS * tq * LANES * 2 + SB_HEADS * 12 * tq * tk * 4)
    return pl.pallas_call(
        functools.partial(_sb_sample_kernel, tq=tq, tk=tk, n_past=p // tk),
        grid=(b,),
        in_specs=[newspec, newspec, newspec, anyspec, anyspec],
        out_specs=newspec,
        out_shape=jax.ShapeDtypeStruct((b, tq, w), BF16),
        scratch_shapes=[pltpu.VMEM((2, w, tk), F32),
                        pltpu.VMEM((2, w, tk), F32),
                        pltpu.SemaphoreType.DMA((2, 2)),
                        pltpu.VMEM((tq, w), F32),
                        pltpu.VMEM((SB_HEADS, tq, 1), F32),
                        pltpu.VMEM((SB_HEADS, SUBLANES, LANES), F32),
                        pltpu.VMEM((tk, tk), BF16),
                        pltpu.VMEM((tq, tq), BF16),
                        pltpu.VMEM((SB_HEADS, tq, LANES), BF16)],
        compiler_params=_params(1, vmem),
        name="sb_sample",
    )(q, k_new, v_new, kt_past, vt_past)


def _merge_kernel(x_ref, ya_ref, yb_ref, gate_ref, wa_ref, wb_ref, wo_ref, gpost_ref, o_ref, m_s, p_s,
                  *, sub, n_sub):
    d = x_ref.shape[1]

    def stages_of(i):
        tok = slice(i * sub, (i + 1) * sub)

        def merge(n):
            cols = slice(n * MXU_COLS, (n + 1) * MXU_COLS)
            gcols = slice(d + n * MXU_COLS, d + (n + 1) * MXU_COLS)
            merged = (gate_ref[tok, cols].astype(F32) * _dot(ya_ref[tok, :], wa_ref[:, cols])
                      + gate_ref[tok, gcols].astype(F32) * _dot(yb_ref[tok, :], wb_ref[:, cols]))
            m_s[i, :, cols] = merged.astype(BF16)

        def project(n):
            cols = slice(n * MXU_COLS, (n + 1) * MXU_COLS)
            p_s[i, :, cols] = _dot(m_s[i], wo_ref[:, cols])

        def norm_and_store():
            o_ref[tok, :] = x_ref[tok, :] + _rms(p_s[i], gpost_ref[...])

        n_chunks = d // MXU_COLS
        return ([functools.partial(merge, n) for n in range(n_chunks)]
                + [functools.partial(project, n) for n in range(n_chunks)] + [norm_and_store])

    _skewed([stages_of(i) for i in range(n_sub)], skew=2)


def _merge_call(x2d, ya, yb, gates, wa, wb, wo, gpost, *, sub, n_sub):
    n_tok, d = x2d.shape
    tm = sub * n_sub

    def tok(width):
        return pl.BlockSpec((tm, width), lambda i: (i, 0))

    weight_bytes = (wa.size + wb.size + wo.size) * 2
    block_bytes = 2 * tm * (d * 4 + ya.shape[1] * 2 + yb.shape[1] * 2 + gates.shape[1] * 2 + d * 4)
    scratch_bytes = tm * d * 6
    return pl.pallas_call(
        functools.partial(_merge_kernel, sub=sub, n_sub=n_sub),
        grid=(n_tok // tm,),
        in_specs=[tok(d), tok(ya.shape[1]), tok(yb.shape[1]), tok(gates.shape[1]),
                  _full(wa.shape), _full(wb.shape), _full(wo.shape), _full(gpost.shape)],
        out_specs=tok(d),
        out_shape=jax.ShapeDtypeStruct((n_tok, d), F32),
        scratch_shapes=[pltpu.VMEM((n_sub, sub, d), BF16), pltpu.VMEM((n_sub, sub, d), F32)],
        compiler_params=_params(1, weight_bytes + block_bytes + 2 * scratch_bytes),
        name="merge",
    )(x2d, ya, yb, gates, wa, wb, wo, gpost)


def _skewed(stage_lists, skew):
    n_stages = len(stage_lists[0])
    for step in range(n_stages + skew * (len(stage_lists) - 1)):
        for i, stages in enumerate(stage_lists):
            if 0 <= step - i * skew < n_stages:
                stages[step - i * skew]()


def _ffn_kernel(x_ref, prev_ref, gpre_ref, wup_ref, wc_ref, bc_ref, wdown_ref, gpost_ref,
                y_ref, newconv_ref, xp_s, hn_s, carry_s, act_s, f_s, *, sub, n_sub):
    assert CONV_WIDTH == 3
    dff = wdown_ref.shape[0]
    d = x_ref.shape[1]
    ng = sub // SUBLANES
    assert ng >= CONV_WIDTH - 1 and ng % SUBLANES == 0
    n_slabs = d // LANES
    t = pl.program_id(1)

    @pl.when(t == 0)
    def _():
        carry_s[0:CONV_WIDTH - 1, :] = prev_ref[...]

    top = lax.broadcasted_iota(jnp.int32, (SUBLANES, MXU_COLS), 0) == 0

    def permuted_rows(r):
        return pl.ds((SUBLANES * r % ng) * SUBLANES + SUBLANES * r // ng, SUBLANES, stride=SUBLANES)

    def stages_of(i):
        base = i * sub

        def load_and_norm():
            for r in range(ng):
                rows = slice(base + r * SUBLANES, base + (r + 1) * SUBLANES)
                for c in range(n_slabs):
                    xp_s[i, c, permuted_rows(r), :] = x_ref[rows, c * LANES:(c + 1) * LANES]
            xp = jnp.concatenate([xp_s[i, c] for c in range(n_slabs)], axis=1)
            hn_s[i] = _rms(xp, gpre_ref[...]).astype(BF16)

        def up_conv_act(n):
            halves = []
            for off in (0, dff):
                cols = slice(off + n * MXU_COLS, off + (n + 1) * MXU_COLS)
                up = _dot(hn_s[i], wup_ref[:, cols])
                back1 = jnp.where(top, carry_s[1:2, cols], pltpu.roll(up[sub - SUBLANES:sub], 1, axis=0))
                back2 = jnp.where(top, carry_s[0:1, cols],
                                  pltpu.roll(up[sub - 2 * SUBLANES:sub - SUBLANES], 1, axis=0))
                carry_s[0:1, cols] = up[sub - SUBLANES - 1:sub - SUBLANES]
                carry_s[1:2, cols] = up[sub - 1:sub]
                prev1 = jnp.concatenate([back1, up[:sub - SUBLANES]], axis=0)
                prev2 = jnp.concatenate([back2, back1, up[:sub - 2 * SUBLANES]], axis=0)
                halves.append(bc_ref[:, cols] + wc_ref[0:1, cols] * prev2 + wc_ref[1:2, cols] * prev1
                              + wc_ref[2:3, cols] * up)
            act_s[i, :, n * MXU_COLS:(n + 1) * MXU_COLS] = (jax.nn.gelu(halves[0]) * halves[1]).astype(BF16)

        def down(n):
            cols = slice(n * MXU_COLS, (n + 1) * MXU_COLS)
            f_s[i, :, cols] = _dot(act_s[i], wdown_ref[:, cols])

        def norm_and_store():
            normed = _rms(f_s[i], gpost_ref[...])
            for c in range(n_slabs):
                xp_s[i, c] = xp_s[i, c] + normed[:, c * LANES:(c + 1) * LANES]
            for r in range(ng):
                rows = slice(base + r * SUBLANES, base + (r + 1) * SUBLANES)
                for c in range(n_slabs):
                    y_ref[rows, c * LANES:(c + 1) * LANES] = xp_s[i, c, permuted_rows(r), :]
            if i == n_sub - 1:
                newconv_ref[...] = carry_s[0:CONV_WIDTH - 1, :]

        return ([load_and_norm] + [functools.partial(up_conv_act, n) for n in range(dff // MXU_COLS)]
                + [functools.partial(down, n) for n in range(d // MXU_COLS)] + [norm_and_store])

    _skewed([stages_of(i) for i in range(n_sub)], skew=2)


def _ffn_call(x, conv_prev, gpre, wup, wc, bc, wdown, gpost, *, sub, n_sub):
    b, t, d = x.shape
    dff = wdown.shape[0]
    tm = sub * n_sub
    xspec = pl.BlockSpec((None, tm, d), lambda bi, ti: (bi, ti, 0))
    cspec = pl.BlockSpec((None, CONV_WIDTH - 1, 2 * dff), lambda bi, ti: (bi, 0, 0))
    weight_bytes = (wup.size + wdown.size) * 2
    block_bytes = 2 * 2 * tm * d * 4
    scratch_bytes = tm * d * 4 + tm * d * 2 + SUBLANES * 2 * dff * 4 + tm * dff * 2 + tm * d * 4
    return pl.pallas_call(
        functools.partial(_ffn_kernel, sub=sub, n_sub=n_sub),
        grid=(b, t // tm),
        in_specs=[xspec, cspec, _full(gpre.shape), _full(wup.shape), _full(wc.shape), _full(bc.shape),
                  _full(wdown.shape), _full(gpost.shape)],
        out_specs=[xspec, cspec],
        out_shape=[jax.ShapeDtypeStruct((b, t, d), F32),
                   jax.ShapeDtypeStruct((b, CONV_WIDTH - 1, 2 * dff), F32)],
        scratch_shapes=[pltpu.VMEM((n_sub, d // LANES, sub, LANES), F32),
                        pltpu.VMEM((n_sub, sub, d), BF16),
                        pltpu.VMEM((SUBLANES, 2 * dff), F32),
                        pltpu.VMEM((n_sub, sub, dff), BF16),
                        pltpu.VMEM((n_sub, sub, d), F32)],
        compiler_params=_params(2, weight_bytes + block_bytes + 2 * scratch_bytes),
        name="ffn",
    )(x, conv_prev, gpre, wup, wc, bc, wdown, gpost)


def _layer(x, past_k, past_v, conv_prev, w):
    b, t, d = x.shape
    gm = GM_GROUPS * GM_GROUP_DIM
    sbw = SB_HEADS * SB_HEAD_DIM
    chunk = min(t, GM_CHUNK)
    n_tok = b * t
    decode = past_k is not None
    tm = min(256, t)
    n_sub = min(SUB_BLOCKS_PER_STEP, t // tm)
    assert t % (tm * n_sub) == 0 and tm % chunk == 0, (t, tm, n_sub, chunk)

    tri = jnp.tril(jnp.ones((chunk, chunk), dtype=bool))
    ws = jnp.where(tri[None], w["w_s"][:, :chunk, :chunk], 0.0).astype(BF16)
    bias = jnp.repeat(w["b_s"][:, :chunk].T, GM_GROUP_DIM, axis=1)
    w_in = w["w_in"]
    w_uvq = w_in[:, :2 * gm + sbw]
    w_k = w_in[:, 2 * gm + sbw:2 * gm + 2 * sbw]
    w_v = w_in[:, 2 * gm + 2 * sbw:]
    if not decode:
        w_k, w_v = w_k.T, w_v.T

    outs = _proj_call(
        x.reshape(n_tok, d), w["g_pre_mix"], w_uvq, w_k, w_v, w["ln_v_g"], w["ln_v_b"], ws, bias,
        w["w_gate"], w["b_gate"], sub=tm, n_sub=n_sub, chunk=chunk, seq_len=t, transposed_kv=not decode,
        emit_va=decode)
    ya, q, k_out, v_out, kb, vb, gates = outs[:7]
    q3 = q.reshape(b, t, sbw)

    if decode:
        p = past_k.shape[1]
        kt_past = jnp.transpose(past_k, (0, 2, 3, 1)).reshape(b, sbw, p)
        vt_past = jnp.transpose(past_v, (0, 2, 3, 1)).reshape(b, sbw, p)
        yb = _sb_sample_call(q3, kb.reshape(b, t, sbw), vb.reshape(b, t, sbw), kt_past, vt_past, tk=min(p, 256))
        k4 = k_out.reshape(b, t, SB_HEADS, SB_HEAD_DIM)
        v4 = v_out.reshape(b, t, SB_HEADS, SB_HEAD_DIM)
        va = outs[7].reshape(b, t, gm)
    else:
        yb = _sb_prompt_call(q3, kb, vb, tq=tm, heads=SB_PROMPT_HEADS_PER_STEP)
        k4 = jnp.transpose(k_out.reshape(b, SB_HEADS, SB_HEAD_DIM, t), (0, 3, 1, 2))
        v4 = jnp.transpose(v_out.reshape(b, SB_HEADS, SB_HEAD_DIM, t), (0, 3, 1, 2))
        va = None

    wide = min(2 * tm, t // n_sub)
    x1 = _merge_call(x.reshape(n_tok, d), ya, yb.reshape(n_tok, sbw), gates,
                     w["w_branch_a"], w["w_branch_b"], w["w_o"], w["g_post_mix"], sub=wide, n_sub=n_sub)
    y, conv_new = _ffn_call(x1.reshape(b, t, d), conv_prev, w["g_pre_ffn"], w["w_up"], w["w_conv"],
                            w["b_conv"], w["w_down"], w["g_post_ffn"], sub=wide, n_sub=n_sub)
    return y, k4, v4, conv_new, va


def kernel(x_prompt, x_sample, cache_sb_k, cache_sb_v, state_ffn_conv, g_pre_mix, w_in, ln_v_g, ln_v_b, w_s, b_s,
           w_gate, b_gate, w_branch_a, w_branch_b, w_o, g_post_mix, g_pre_ffn, w_up, w_conv, b_conv, w_down,
           g_post_ffn):
    depth = w_in.shape[0]
    y_p, y_s = x_prompt, x_sample
    outs = [[] for _ in range(7)]
    for l in range(depth):
        row = lambda a: a[l][None, :]
        w = dict(
            g_pre_mix=row(g_pre_mix), w_in=w_in[l].astype(BF16), ln_v_g=row(ln_v_g), ln_v_b=row(ln_v_b),
            w_s=w_s[l], b_s=b_s[l], w_gate=w_gate[l].astype(BF16), b_gate=row(b_gate),
            w_branch_a=w_branch_a[l].astype(BF16), w_branch_b=w_branch_b[l].astype(BF16),
            w_o=w_o[l].astype(BF16), g_post_mix=row(g_post_mix), g_pre_ffn=row(g_pre_ffn),
            w_up=w_up[l].astype(BF16), w_conv=w_conv[l], b_conv=row(b_conv),
            w_down=w_down[l].astype(BF16), g_post_ffn=row(g_post_ffn))
        zero_conv = jnp.zeros((y_p.shape[0], CONV_WIDTH - 1, w_up.shape[2]), y_p.dtype)
        y_p, kp, vp, cp, _ = _layer(y_p, None, None, zero_conv, w)
        y_s, ks, vs, cs, gvs = _layer(y_s, cache_sb_k[l], cache_sb_v[l], state_ffn_conv[l], w)
        for lst, a in zip(outs, (kp, vp, cp, ks, vs, cs, gvs)):
            lst.append(a)
    return (y_p, y_s) + tuple(jnp.stack(o) for o in outs)
```

```python
import functools
import math
from typing import Callable, NamedTuple, Optional

import jax
import jax.numpy as jnp
from jax import lax
from jax.experimental import pallas as pl
from jax.experimental.pallas import tpu as pltpu

F32 = jnp.float32
BF16 = jnp.bfloat16

EPS = 1e-6
SB_HEADS = 16
SB_HEAD_DIM = 64
GM_GROUPS = 8
GM_GROUP_DIM = 128
GM_CHUNK = 128
CONV_WIDTH = 3

LANES = 128
SUBLANES = 8
MXU_COLS = 256
HEADS_PER_LANE_TILE = LANES // SB_HEAD_DIM
HEADS_PER_MXU_TILE = MXU_COLS // SB_HEAD_DIM
SUB_BLOCKS_PER_STEP = 2
SB_PROMPT_HEADS_PER_STEP = 16
V7X_SCOPED_VMEM_BYTES = 60000 * 1024

LOG2E = math.log2(math.e)
F32_EXP2_UNDERFLOW = 150.0
SIGN_BIT = 0x80000000

_ARB = "arbitrary"
_NT = (((1,), (1,)), ((), ()))


def _rms(x, g):
    return x * lax.rsqrt(jnp.mean(x * x, axis=-1, keepdims=True) + EPS) * g


def _dot(a, b):
    return jnp.dot(a, b, preferred_element_type=F32)


def _dot_nt(a, b):
    return lax.dot_general(a, b, _NT, preferred_element_type=F32)


def _params(n_grid, vmem_bytes):
    return pltpu.CompilerParams(
        dimension_semantics=(_ARB,) * n_grid,
        vmem_limit_bytes=int(min(vmem_bytes, V7X_SCOPED_VMEM_BYTES)))


def _full(shape):
    nd = len(shape)
    return pl.BlockSpec(shape, lambda *_: (0,) * nd, pipeline_mode=pl.Buffered(1))


def _proj_kernel(x_ref, gpre_ref, win_ref, wk_ref, wv_ref, lng_ref, lnb_ref, ws_ref, bias_ref, wg_ref, bg_ref,
                 ya_ref, q_ref, k_ref, v_ref, kb_ref, vb_ref, gate_ref, *rest,
                 sub, n_sub, chunk, transposed_kv, emit_va):
    if emit_va:
        va_ref, hb_s, u_s, gv_s, vab_s = rest
    else:
        hb_s, u_s, gv_s, vab_s = rest
    gm = GM_GROUPS * GM_GROUP_DIM
    sb = SB_HEADS * SB_HEAD_DIM
    qscale = SB_HEAD_DIM ** -0.5 * LOG2E
    nch = sub // chunk

    def stages_of(i):
        base = i * sub
        tok = slice(base, base + sub)

        def norm():
            hb_s[i] = _rms(x_ref[tok, :], gpre_ref[...]).astype(BF16)

        def proj(col0, n):
            cols = slice(col0 + n * MXU_COLS, col0 + (n + 1) * MXU_COLS)
            return _dot(hb_s[i], win_ref[:, cols])

        def gating_inputs(n):
            oc = slice(n * MXU_COLS, (n + 1) * MXU_COLS)
            u_s[i, :, oc] = jax.nn.gelu(proj(0, n))
            gv_s[i, :, oc] = jax.nn.gelu(proj(gm, n))

        def layer_norm():
            gv = gv_s[i]
            mu = jnp.mean(gv, axis=-1, keepdims=True)
            dev = gv - mu
            var = jnp.mean(dev * dev, axis=-1, keepdims=True)
            va = dev * lax.rsqrt(var + EPS) * lng_ref[...] + lnb_ref[...]
            if emit_va:
                va_ref[tok, :] = va
            vab_s[i] = va.astype(BF16)

        def spatial_gating(g):
            cols = slice(g * GM_GROUP_DIM, (g + 1) * GM_GROUP_DIM)
            pieces = [vab_s[i, c * chunk:(c + 1) * chunk, cols] for c in range(nch)]
            mixed = _dot(ws_ref[g], pieces[0] if nch == 1 else jnp.concatenate(pieces, axis=1))
            for c in range(nch):
                rows = slice(c * chunk, (c + 1) * chunk)
                m = mixed[:, c * GM_GROUP_DIM:(c + 1) * GM_GROUP_DIM] + bias_ref[:, cols]
                ya_ref[base + c * chunk:base + (c + 1) * chunk, cols] = (u_s[i, rows, cols] * m).astype(BF16)

        def queries(n):
            q_ref[tok, n * MXU_COLS:(n + 1) * MXU_COLS] = (proj(2 * gm, n) * qscale).astype(BF16)

        def keys_or_values(w_ref, f32_out, bf16_out, n):
            oc = slice(n * MXU_COLS, (n + 1) * MXU_COLS)
            if transposed_kv:
                pt = _dot_nt(w_ref[oc, :], hb_s[i])
                f32_out[oc, tok] = pt
                bf16_out[oc, tok] = pt.astype(BF16)
            else:
                p = _dot(hb_s[i], w_ref[:, oc])
                bf16_out[tok, oc] = p.astype(BF16)
                for hs in range(HEADS_PER_MXU_TILE):
                    h = n * HEADS_PER_MXU_TILE + hs
                    f32_out[pl.ds(base * SB_HEADS + h, sub, stride=SB_HEADS), :] = (
                        p[:, hs * SB_HEAD_DIM:(hs + 1) * SB_HEAD_DIM])

        def gates(n):
            oc = slice(n * MXU_COLS, (n + 1) * MXU_COLS)
            gate_ref[tok, oc] = jax.nn.sigmoid(_dot(hb_s[i], wg_ref[:, oc]) + bg_ref[:, oc]).astype(BF16)

        part = functools.partial
        n_sb = sb // MXU_COLS
        return ([norm] + [part(gating_inputs, n) for n in range(gm // MXU_COLS)]
                + [part(queries, n) for n in range(n_sb)] + [layer_norm]
                + [part(keys_or_values, wk_ref, k_ref, kb_ref, n) for n in range(n_sb)]
                + [part(spatial_gating, g) for g in range(GM_GROUPS)]
                + [part(keys_or_values, wv_ref, v_ref, vb_ref, n) for n in range(n_sb)]
                + [part(gates, n) for n in range(gate_ref.shape[1] // MXU_COLS)])

    _skewed([stages_of(i) for i in range(n_sub)], skew=2)


def _proj_call(x2d, gpre, win, wk, wv, lng, lnb, ws, bias, wg, bg, *, sub, n_sub, chunk, seq_len, transposed_kv,
               emit_va):
    n_tok, d = x2d.shape
    gm = GM_GROUPS * GM_GROUP_DIM
    sb = SB_HEADS * SB_HEAD_DIM
    n_gate = wg.shape[1]
    tm = sub * n_sub

    def tok(width):
        return pl.BlockSpec((tm, width), lambda i: (i, 0))

    if transposed_kv:
        nt = seq_len // tm
        kv_spec = pl.BlockSpec((None, sb, tm), lambda i: (i // nt, 0, i % nt))
        kv_f32 = jax.ShapeDtypeStruct((n_tok // seq_len, sb, seq_len), F32)
        kv_bf16 = jax.ShapeDtypeStruct((n_tok // seq_len, sb, seq_len), BF16)
        kv_specs = [kv_spec] * 4
        kv_bytes = 2 * 2 * sb * tm * (4 + 2)
    else:
        hm_spec = pl.BlockSpec((tm * SB_HEADS, SB_HEAD_DIM), lambda i: (i, 0))
        kv_f32 = jax.ShapeDtypeStruct((n_tok * SB_HEADS, SB_HEAD_DIM), F32)
        kv_bf16 = jax.ShapeDtypeStruct((n_tok, sb), BF16)
        kv_specs = [hm_spec, hm_spec, tok(sb), tok(sb)]
        kv_bytes = 2 * 2 * tm * (SB_HEADS * LANES * 4 + sb * 2)
    out_specs = [tok(gm), tok(sb)] + kv_specs + [tok(n_gate)]
    out_shape = [jax.ShapeDtypeStruct((n_tok, gm), BF16), jax.ShapeDtypeStruct((n_tok, sb), BF16),
                 kv_f32, kv_f32, kv_bf16, kv_bf16, jax.ShapeDtypeStruct((n_tok, n_gate), BF16)]
    if emit_va:
        out_specs.append(tok(gm))
        out_shape.append(jax.ShapeDtypeStruct((n_tok, gm), F32))

    weight_bytes = (win.size + wk.size + wv.size + wg.size + ws.size) * 2
    block_bytes = 2 * tm * (d * 4 + gm * 2 + sb * 2 + n_gate * 2 + gm * 4) + kv_bytes
    scratch_bytes = tm * (d * 2 + gm * 4 + gm * 4 + gm * 2)
    return pl.pallas_call(
        functools.partial(_proj_kernel, sub=sub, n_sub=n_sub, chunk=chunk, transposed_kv=transposed_kv,
                          emit_va=emit_va),
        grid=(n_tok // tm,),
        in_specs=[tok(d), _full(gpre.shape), _full(win.shape), _full(wk.shape), _full(wv.shape),
                  _full(lng.shape), _full(lnb.shape), _full(ws.shape), _full(bias.shape), _full(wg.shape),
                  _full(bg.shape)],
        out_specs=out_specs,
        out_shape=out_shape,
        scratch_shapes=[pltpu.VMEM((n_sub, sub, d), BF16), pltpu.VMEM((n_sub, sub, gm), F32),
                        pltpu.VMEM((n_sub, sub, gm), F32), pltpu.VMEM((n_sub, sub, gm), BF16)],
        compiler_params=_params(1, weight_bytes + block_bytes + 2 * scratch_bytes),
        name="proj",
    )(x2d, gpre, win, wk, wv, lng, lnb, ws, bias, wg, bg)


class _Tile(NamedTuple):
    head: int
    scores: Callable
    values: Callable
    v_transposed: bool
    tri: jax.Array
    mask: Optional[jax.Array]
    first: bool


def _sb_tiles(tiles, c_ref, cmin_ref, acc_ref):
    n = len(tiles)
    z, sp, cum, weight = ([None] * n for _ in range(4))

    def scores(i):
        z[i] = tiles[i].scores()

    def softplus(i):
        neg_abs = pltpu.bitcast(pltpu.bitcast(z[i], jnp.uint32) | jnp.uint32(SIGN_BIT), F32)
        s = jnp.maximum(z[i], 0.0) + jnp.log2(1.0 + jnp.exp2(neg_abs))
        sp[i] = s if tiles[i].mask is None else jnp.where(tiles[i].mask, s, 0.0)

    def cumsum(i):
        cum[i] = _dot(sp[i].astype(BF16), tiles[i].tri)

    def weights(i):
        h = tiles[i].head
        total = sp[i] + cum[i]
        if not tiles[i].first:
            total = total + c_ref[h]
        a = jnp.exp2(z[i] - total)
        if tiles[i].mask is not None:
            a = jnp.where(tiles[i].mask, a, 0.0)
        weight[i] = a.astype(BF16)
        c_ref[h] = total[:, 0:1]
        tq, width = total.shape[0], min(LANES, total.shape[1])
        cmin_ref[h, :, 0:width] = jnp.min(total[:, 0:width].reshape(tq // SUBLANES, SUBLANES, width), axis=0)
        z[i] = sp[i] = cum[i] = None

    def values(i):
        v = tiles[i].values()
        contrib = _dot_nt(weight[i], v) if tiles[i].v_transposed else _dot(weight[i], v)
        h = tiles[i].head
        lanes = _pair_lanes(h)
        if tiles[i].first and h % HEADS_PER_LANE_TILE == 0:
            acc_ref[:, lanes] = contrib
        else:
            acc_ref[:, lanes] = acc_ref[:, lanes] + contrib
        weight[i] = None

    stages = (scores, softplus, cumsum, weights, values)
    for step in range(n + len(stages) - 1):
        for s, stage in enumerate(stages):
            if 0 <= step - s < n:
                stage(step - s)


def _pair_lanes(h):
    return slice(h // HEADS_PER_LANE_TILE * LANES, (h // HEADS_PER_LANE_TILE + 1) * LANES)


def _strict_tri(n):
    r = lax.broadcasted_iota(jnp.int32, (n, n), 0)
    s = lax.broadcasted_iota(jnp.int32, (n, n), 1)
    return jnp.where(r > s, 1.0, 0.0).astype(BF16)


def _causal_mask(n):
    r = lax.broadcasted_iota(jnp.int32, (n, n), 0)
    s = lax.broadcasted_iota(jnp.int32, (n, n), 1)
    return s < r


def _head_lane_masks():
    lane = lax.broadcasted_iota(jnp.int32, (1, LANES), 1)
    return [(lane >= hh * SB_HEAD_DIM) & (lane < (hh + 1) * SB_HEAD_DIM) for hh in range(HEADS_PER_LANE_TILE)]


def _head_row_masks():
    row = lax.broadcasted_iota(jnp.int32, (LANES, 1), 0)
    return [(row >= hh * SB_HEAD_DIM) & (row < (hh + 1) * SB_HEAD_DIM) for hh in range(HEADS_PER_LANE_TILE)]


def _masked_queries(q_ref, rows, qm_s, heads):
    lane_masks = _head_lane_masks()
    for h in range(heads):
        lt = h // HEADS_PER_LANE_TILE
        qt = q_ref[rows, lt * LANES:(lt + 1) * LANES]
        qm_s[h] = jnp.where(lane_masks[h % HEADS_PER_LANE_TILE], qt, jnp.zeros_like(qt))


def _sweep_is_live(cmin_ref):
    return (jnp.min(cmin_ref[:, :, 0:1]) < F32_EXP2_UNDERFLOW).astype(jnp.int32)


def _sb_prompt_kernel(q_ref, kt_ref, vt_ref, o_ref, acc_s, c_s, cmin_s, u_s, qm_s, *, tq, heads):
    n_blocks = q_ref.shape[0] // tq
    u_s[...] = _strict_tri(tq)
    row_masks = _head_row_masks()

    def tiles(j, diagonal):
        keys = pl.ds(pl.multiple_of(j * tq, tq), tq)
        mask = _causal_mask(tq) if diagonal else None

        def scores(h):
            return _dot(qm_s[h], kt_ref[_pair_lanes(h), keys])

        def values(h):
            vt = vt_ref[_pair_lanes(h), keys]
            return jnp.where(row_masks[h % HEADS_PER_LANE_TILE], vt, jnp.zeros_like(vt))

        return [_Tile(h, functools.partial(scores, h), functools.partial(values, h), True, u_s[...], mask,
                      first=diagonal) for h in range(heads)]

    def query_block(i, has_previous):
        rows = pl.ds(pl.multiple_of(i * tq, tq), tq)
        _masked_queries(q_ref, rows, qm_s, heads)
        if has_previous:
            _sb_tiles(tiles(i, True) + tiles(i - 1, False), c_s, cmin_s, acc_s)

            def cond(carry):
                n, go = carry
                return jnp.logical_and(n < i - 1, go > 0)

            def body(carry):
                n, _ = carry
                _sb_tiles(tiles(i - 2 - n, False), c_s, cmin_s, acc_s)
                return n + 1, _sweep_is_live(cmin_s)

            lax.while_loop(cond, body, (jnp.int32(0), _sweep_is_live(cmin_s)))
        else:
            _sb_tiles(tiles(i, True), c_s, cmin_s, acc_s)
        o_ref[rows, :] = acc_s[...].astype(BF16)

    query_block(0, False)

    def later_blocks(i, carry):
        query_block(i, True)
        return carry

    lax.fori_loop(1, n_blocks, later_blocks, 0)


def _sb_prompt_call(q, kt, vt, *, tq, heads):
    b, t, w = q.shape
    width = heads * SB_HEAD_DIM
    qspec = pl.BlockSpec((None, t, width), lambda bi, hg: (bi, 0, hg))
    kvspec = pl.BlockSpec((None, width, t), lambda bi, hg: (bi, hg, 0))
    vmem = (2 * 4 * t * width * 2 + tq * width * 4 + heads * tq * LANES * 4
            + tq * tq * 2 + heads * tq * LANES * 2 + heads * 12 * tq * tq * 4)
    return pl.pallas_call(
        functools.partial(_sb_prompt_kernel, tq=tq, heads=heads),
        grid=(b, w // width),
        in_specs=[qspec, kvspec, kvspec],
        out_specs=qspec,
        out_shape=jax.ShapeDtypeStruct((b, t, w), BF16),
        scratch_shapes=[pltpu.VMEM((tq, width), F32),
                        pltpu.VMEM((heads, tq, 1), F32),
                        pltpu.VMEM((heads, SUBLANES, LANES), F32),
                        pltpu.VMEM((tq, tq), BF16),
                        pltpu.VMEM((heads, tq, LANES), BF16)],
        compiler_params=_params(2, vmem),
        name="sb_prompt",
    )(q, kt, vt)


def _sb_sample_kernel(q_ref, kn_ref, vn_ref, kp_hbm, vp_hbm, o_ref, kbuf, vbuf, sem, acc_s, c_s, cmin_s, u_s, ud_s,
                      qm_s, *, tq, tk, n_past):
    b = pl.program_id(0)

    def copies(j, slot):
        keys = pl.ds(pl.multiple_of(j * tk, tk), tk)
        return (pltpu.make_async_copy(kp_hbm.at[b, :, keys], kbuf.at[slot], sem.at[0, slot]),
                pltpu.make_async_copy(vp_hbm.at[b, :, keys], vbuf.at[slot], sem.at[1, slot]))

    def start(j, slot):
        for cp in copies(j, slot):
            cp.start()

    def wait(j, slot):
        for cp in copies(j, slot):
            cp.wait()

    start(n_past - 1, 0)
    u_s[...] = _strict_tri(tk)
    ud_s[...] = _strict_tri(tq)
    _masked_queries(q_ref, slice(None), qm_s, SB_HEADS)
    lane_masks = _head_lane_masks()
    row_masks = _head_row_masks()

    def z_new(h):
        return _dot_nt(qm_s[h], kn_ref[:, _pair_lanes(h)])

    def v_new(h):
        vn = vn_ref[:, _pair_lanes(h)]
        return jnp.where(lane_masks[h % HEADS_PER_LANE_TILE], vn, jnp.zeros_like(vn))

    mask = _causal_mask(tq)
    _sb_tiles([_Tile(h, functools.partial(z_new, h), functools.partial(v_new, h), False, ud_s[...], mask,
                     first=True) for h in range(SB_HEADS)], c_s, cmin_s, acc_s)

    def cond(carry):
        n, go = carry
        return jnp.logical_and(n < n_past, go > 0)

    def body(carry):
        n, _ = carry
        j = n_past - 1 - n
        slot = lax.rem(n, 2)
        wait(j, slot)

        @pl.when(n + 1 < n_past)
        def _():
            start(j - 1, 1 - slot)

        def z_past(h):
            return _dot(qm_s[h], kbuf[slot, _pair_lanes(h), :].astype(BF16))

        def v_past(h):
            vt = vbuf[slot, _pair_lanes(h), :].astype(BF16)
            return jnp.where(row_masks[h % HEADS_PER_LANE_TILE], vt, jnp.zeros_like(vt))

        _sb_tiles([_Tile(h, functools.partial(z_past, h), functools.partial(v_past, h), True, u_s[...], None,
                         first=False) for h in range(SB_HEADS)], c_s, cmin_s, acc_s)
        return n + 1, _sweep_is_live(cmin_s)

    n_done, _ = lax.while_loop(cond, body, (jnp.int32(0), _sweep_is_live(cmin_s)))

    @pl.when(n_done < n_past)
    def _():
        wait(n_past - 1 - n_done, lax.rem(n_done, 2))

    o_ref[...] = acc_s[...].astype(BF16)


def _sb_sample_call(q, k_new, v_new, kt_past, vt_past, *, tk):
    b, tq, w = q.shape
    p = kt_past.shape[2]
    newspec = pl.BlockSpec((None, tq, w), lambda bi: (bi, 0, 0))
    anyspec = pl.BlockSpec(memory_space=pl.ANY)
    vmem = (2 * 2 * w * tk * 4 + 2 * 4 * tq * w * 2 + tq * w * 4 + SB_HEADS * tq * LANES * 4
            + 2 * tk * tk * 2 + SB_HEADS * tq * LANES * 2 + SB_HEADS * 12 * tq * tk * 4)
    return pl.pallas_call(
        functools.partial(_sb_sample_kernel, tq=tq, tk=tk, n_past=p // tk),
        grid=(b,),
        in_specs=[newspec, newspec, newspec, anyspec, anyspec],
        out_specs=newspec,
        out_shape=jax.ShapeDtypeStruct((b, tq, w), BF16),
        scratch_shapes=[pltpu.VMEM((2, w, tk), F32),
                        pltpu.VMEM((2, w, tk), F32),
                        pltpu.SemaphoreType.DMA((2, 2)),
                        pltpu.VMEM((tq, w), F32),
                        pltpu.VMEM((SB_HEADS, tq, 1), F32),
                        pltpu.VMEM((SB_HEADS, SUBLANES, LANES), F32),
                        pltpu.VMEM((tk, tk), BF16),
                        pltpu.VMEM((tq, tq), BF16),
                        pltpu.VMEM((SB_HEADS, tq, LANES), BF16)],
        compiler_params=_params(1, vmem),
        name="sb_sample",
    )(q, k_new, v_new, kt_past, vt_past)


def _merge_kernel(x_ref, ya_ref, yb_ref, gate_ref, wa_ref, wb_ref, wo_ref, gpost_ref, o_ref, m_s, p_s,
                  *, sub, n_sub):
    d = x_ref.shape[1]

    def stages_of(i):
        tok = slice(i * sub, (i + 1) * sub)

        def merge(n):
            cols = slice(n * MXU_COLS, (n + 1) * MXU_COLS)
            gcols = slice(d + n * MXU_COLS, d + (n + 1) * MXU_COLS)
            merged = (gate_ref[tok, cols].astype(F32) * _dot(ya_ref[tok, :], wa_ref[:, cols])
                      + gate_ref[tok, gcols].astype(F32) * _dot(yb_ref[tok, :], wb_ref[:, cols]))
            m_s[i, :, cols] = merged.astype(BF16)

        def project(n):
            cols = slice(n * MXU_COLS, (n + 1) * MXU_COLS)
            p_s[i, :, cols] = _dot(m_s[i], wo_ref[:, cols])

        def norm_and_store():
            o_ref[tok, :] = x_ref[tok, :] + _rms(p_s[i], gpost_ref[...])

        n_chunks = d // MXU_COLS
        return ([functools.partial(merge, n) for n in range(n_chunks)]
                + [functools.partial(project, n) for n in range(n_chunks)] + [norm_and_store])

    _skewed([stages_of(i) for i in range(n_sub)], skew=2)


def _merge_call(x2d, ya, yb, gates, wa, wb, wo, gpost, *, sub, n_sub):
    n_tok, d = x2d.shape
    tm = sub * n_sub

    def tok(width):
        return pl.BlockSpec((tm, width), lambda i: (i, 0))

    weight_bytes = (wa.size + wb.size + wo.size) * 2
    block_bytes = 2 * tm * (d * 4 + ya.shape[1] * 2 + yb.shape[1] * 2 + gates.shape[1] * 2 + d * 4)
    scratch_bytes = tm * d * 6
    return pl.pallas_call(
        functools.partial(_merge_kernel, sub=sub, n_sub=n_sub),
        grid=(n_tok // tm,),
        in_specs=[tok(d), tok(ya.shape[1]), tok(yb.shape[1]), tok(gates.shape[1]),
                  _full(wa.shape), _full(wb.shape), _full(wo.shape), _full(gpost.shape)],
        out_specs=tok(d),
        out_shape=jax.ShapeDtypeStruct((n_tok, d), F32),
        scratch_shapes=[pltpu.VMEM((n_sub, sub, d), BF16), pltpu.VMEM((n_sub, sub, d), F32)],
        compiler_params=_params(1, weight_bytes + block_bytes + 2 * scratch_bytes),
        name="merge",
    )(x2d, ya, yb, gates, wa, wb, wo, gpost)


def _skewed(stage_lists, skew):
    n_stages = len(stage_lists[0])
    for step in range(n_stages + skew * (len(stage_lists) - 1)):
        for i, stages in enumerate(stage_lists):
            if 0 <= step - i * skew < n_stages:
                stages[step - i * skew]()


def _ffn_kernel(x_ref, prev_ref, gpre_ref, wup_ref, wc_ref, bc_ref, wdown_ref, gpost_ref,
                y_ref, newconv_ref, xp_s, hn_s, carry_s, act_s, f_s, *, sub, n_sub):
    assert CONV_WIDTH == 3
    dff = wdown_ref.shape[0]
    d = x_ref.shape[1]
    ng = sub // SUBLANES
    assert ng >= CONV_WIDTH - 1 and ng % SUBLANES == 0
    n_slabs = d // LANES
    t = pl.program_id(1)

    @pl.when(t == 0)
    def _():
        carry_s[0:CONV_WIDTH - 1, :] = prev_ref[...]

    top = lax.broadcasted_iota(jnp.int32, (SUBLANES, MXU_COLS), 0) == 0

    def permuted_rows(r):
        return pl.ds((SUBLANES * r % ng) * SUBLANES + SUBLANES * r // ng, SUBLANES, stride=SUBLANES)

    def stages_of(i):
        base = i * sub

        def load_and_norm():
            for r in range(ng):
                rows = slice(base + r * SUBLANES, base + (r + 1) * SUBLANES)
                for c in range(n_slabs):
                    xp_s[i, c, permuted_rows(r), :] = x_ref[rows, c * LANES:(c + 1) * LANES]
            xp = jnp.concatenate([xp_s[i, c] for c in range(n_slabs)], axis=1)
            hn_s[i] = _rms(xp, gpre_ref[...]).astype(BF16)

        def up_conv_act(n):
            halves = []
            for off in (0, dff):
                cols = slice(off + n * MXU_COLS, off + (n + 1) * MXU_COLS)
                up = _dot(hn_s[i], wup_ref[:, cols])
                back1 = jnp.where(top, carry_s[1:2, cols], pltpu.roll(up[sub - SUBLANES:sub], 1, axis=0))
                back2 = jnp.where(top, carry_s[0:1, cols],
                                  pltpu.roll(up[sub - 2 * SUBLANES:sub - SUBLANES], 1, axis=0))
                carry_s[0:1, cols] = up[sub - SUBLANES - 1:sub - SUBLANES]
                carry_s[1:2, cols] = up[sub - 1:sub]
                prev1 = jnp.concatenate([back1, up[:sub - SUBLANES]], axis=0)
                prev2 = jnp.concatenate([back2, back1, up[:sub - 2 * SUBLANES]], axis=0)
                halves.append(bc_ref[:, cols] + wc_ref[0:1, cols] * prev2 + wc_ref[1:2, cols] * prev1
                              + wc_ref[2:3, cols] * up)
            act_s[i, :, n * MXU_COLS:(n + 1) * MXU_COLS] = (jax.nn.gelu(halves[0]) * halves[1]).astype(BF16)

        def down(n):
            cols = slice(n * MXU_COLS, (n + 1) * MXU_COLS)
            f_s[i, :, cols] = _dot(act_s[i], wdown_ref[:, cols])

        def norm_and_store():
            normed = _rms(f_s[i], gpost_ref[...])
            for c in range(n_slabs):
                xp_s[i, c] = xp_s[i, c] + normed[:, c * LANES:(c + 1) * LANES]
            for r in range(ng):
                rows = slice(base + r * SUBLANES, base + (r + 1) * SUBLANES)
                for c in range(n_slabs):
                    y_ref[rows, c * LANES:(c + 1) * LANES] = xp_s[i, c, permuted_rows(r), :]
            if i == n_sub - 1:
                newconv_ref[...] = carry_s[0:CONV_WIDTH - 1, :]

        return ([load_and_norm] + [functools.partial(up_conv_act, n) for n in range(dff // MXU_COLS)]
                + [functools.partial(down, n) for n in range(d // MXU_COLS)] + [norm_and_store])

    _skewed([stages_of(i) for i in range(n_sub)], skew=2)


def _ffn_call(x, conv_prev, gpre, wup, wc, bc, wdown, gpost, *, sub, n_sub):
    b, t, d = x.shape
    dff = wdown.shape[0]
    tm = sub * n_sub
    xspec = pl.BlockSpec((None, tm, d), lambda bi, ti: (bi, ti, 0))
    cspec = pl.BlockSpec((None, CONV_WIDTH - 1, 2 * dff), lambda bi, ti: (bi, 0, 0))
    weight_bytes = (wup.size + wdown.size) * 2
    block_bytes = 2 * 2 * tm * d * 4
    scratch_bytes = tm * d * 4 + tm * d * 2 + SUBLANES * 2 * dff * 4 + tm * dff * 2 + tm * d * 4
    return pl.pallas_call(
        functools.partial(_ffn_kernel, sub=sub, n_sub=n_sub),
        grid=(b, t // tm),
        in_specs=[xspec, cspec, _full(gpre.shape), _full(wup.shape), _full(wc.shape), _full(bc.shape),
                  _full(wdown.shape), _full(gpost.shape)],
        out_specs=[xspec, cspec],
        out_shape=[jax.ShapeDtypeStruct((b, t, d), F32),
                   jax.ShapeDtypeStruct((b, CONV_WIDTH - 1, 2 * dff), F32)],
        scratch_shapes=[pltpu.VMEM((n_sub, d // LANES, sub, LANES), F32),
                        pltpu.VMEM((n_sub, sub, d), BF16),
                        pltpu.VMEM((SUBLANES, 2 * dff), F32),
                        pltpu.VMEM((n_sub, sub, dff), BF16),
                        pltpu.VMEM((n_sub, sub, d), F32)],
        compiler_params=_params(2, weight_bytes + block_bytes + 2 * scratch_bytes),
        name="ffn",
    )(x, conv_prev, gpre, wup, wc, bc, wdown, gpost)


def _layer(x, past_k, past_v, conv_prev, w):
    b, t, d = x.shape
    gm = GM_GROUPS * GM_GROUP_DIM
    sbw = SB_HEADS * SB_HEAD_DIM
    chunk = min(t, GM_CHUNK)
    n_tok = b * t
    decode = past_k is not None
    tm = min(256, t)
    n_sub = min(SUB_BLOCKS_PER_STEP, t // tm)
    assert t % (tm * n_sub) == 0 and tm % chunk == 0, (t, tm, n_sub, chunk)

    tri = jnp.tril(jnp.ones((chunk, chunk), dtype=bool))
    ws = jnp.where(tri[None], w["w_s"][:, :chunk, :chunk], 0.0).astype(BF16)
    bias = jnp.repeat(w["b_s"][:, :chunk].T, GM_GROUP_DIM, axis=1)
    w_in = w["w_in"]
    w_uvq = w_in[:, :2 * gm + sbw]
    w_k = w_in[:, 2 * gm + sbw:2 * gm + 2 * sbw]
    w_v = w_in[:, 2 * gm + 2 * sbw:]
    if not decode:
        w_k, w_v = w_k.T, w_v.T

    flat_n_sub = min(SUB_BLOCKS_PER_STEP, n_tok // tm)
    flat_sub = max(tm, min(256, n_tok // flat_n_sub))
    assert n_tok % (flat_sub * flat_n_sub) == 0 and flat_sub % chunk == 0, (n_tok, flat_sub, flat_n_sub)
    outs = _proj_call(
        x.reshape(n_tok, d), w["g_pre_mix"], w_uvq, w_k, w_v, w["ln_v_g"], w["ln_v_b"], ws, bias,
        w["w_gate"], w["b_gate"], sub=flat_sub if decode else tm, n_sub=flat_n_sub if decode else n_sub,
        chunk=chunk, seq_len=t, transposed_kv=not decode, emit_va=decode)
    ya, q, k_out, v_out, kb, vb, gates = outs[:7]
    q3 = q.reshape(b, t, sbw)

    if decode:
        p = past_k.shape[1]
        kt_past = jnp.transpose(past_k, (0, 2, 3, 1)).reshape(b, sbw, p)
        vt_past = jnp.transpose(past_v, (0, 2, 3, 1)).reshape(b, sbw, p)
        yb = _sb_sample_call(q3, kb.reshape(b, t, sbw), vb.reshape(b, t, sbw), kt_past, vt_past, tk=min(p, 256))
        k4 = k_out.reshape(b, t, SB_HEADS, SB_HEAD_DIM)
        v4 = v_out.reshape(b, t, SB_HEADS, SB_HEAD_DIM)
        va = outs[7].reshape(b, t, gm)
    else:
        yb = _sb_prompt_call(q3, kb, vb, tq=tm, heads=SB_PROMPT_HEADS_PER_STEP)
        k4 = jnp.transpose(k_out.reshape(b, SB_HEADS, SB_HEAD_DIM, t), (0, 3, 1, 2))
        v4 = jnp.transpose(v_out.reshape(b, SB_HEADS, SB_HEAD_DIM, t), (0, 3, 1, 2))
        va = None

    wide = min(2 * tm, t // n_sub)
    flat_wide = max(wide, min(2 * flat_sub, n_tok // flat_n_sub))
    x1 = _merge_call(x.reshape(n_tok, d), ya, yb.reshape(n_tok, sbw), gates,
                     w["w_branch_a"], w["w_branch_b"], w["w_o"], w["g_post_mix"], sub=flat_wide, n_sub=flat_n_sub)
    y, conv_new = _ffn_call(x1.reshape(b, t, d), conv_prev, w["g_pre_ffn"], w["w_up"], w["w_conv"],
                            w["b_conv"], w["w_down"], w["g_post_ffn"], sub=wide, n_sub=n_sub)
    return y, k4, v4, conv_new, va


def kernel(x_prompt, x_sample, cache_sb_k, cache_sb_v, state_ffn_conv, g_pre_mix, w_in, ln_v_g, ln_v_b, w_s, b_s,
           w_gate, b_gate, w_branch_a, w_branch_b, w_o, g_post_mix, g_pre_ffn, w_up, w_conv, b_conv, w_down,
           g_post_ffn):
    depth = w_in.shape[0]
    y_p, y_s = x_prompt, x_sample
    outs = [[] for _ in range(7)]
    for l in range(depth):
        row = lambda a: a[l][None, :]
        w = dict(
            g_pre_mix=row(g_pre_mix), w_in=w_in[l].astype(BF16), ln_v_g=row(ln_v_g), ln_v_b=row(ln_v_b),
            w_s=w_s[l], b_s=b_s[l], w_gate=w_gate[l].astype(BF16), b_gate=row(b_gate),
            w_branch_a=w_branch_a[l].astype(BF16), w_branch_b=w_branch_b[l].astype(BF16),
            w_o=w_o[l].astype(BF16), g_post_mix=row(g_post_mix), g_pre_ffn=row(g_pre_ffn),
            w_up=w_up[l].astype(BF16), w_conv=w_conv[l], b_conv=row(b_conv),
            w_down=w_down[l].astype(BF16), g_post_ffn=row(g_post_ffn))
        zero_conv = jnp.zeros((y_p.shape[0], CONV_WIDTH - 1, w_up.shape[2]), y_p.dtype)
        y_p, kp, vp, cp, _ = _layer(y_p, None, None, zero_conv, w)
        y_s, ks, vs, cs, gvs = _layer(y_s, cache_sb_k[l], cache_sb_v[l], state_ffn_conv[l], w)
        for lst, a in zip(outs, (kp, vp, cp, ks, vs, cs, gvs)):
            lst.append(a)
    return (y_p, y_s) + tuple(jnp.stack(o) for o in outs)
```

```python
import functools
import math
from typing import Callable, NamedTuple, Optional

import jax
import jax.numpy as jnp
from jax import lax
from jax.experimental import pallas as pl
from jax.experimental.pallas import tpu as pltpu

F32 = jnp.float32
BF16 = jnp.bfloat16

EPS = 1e-6
SB_HEADS = 16
SB_HEAD_DIM = 64
GM_GROUPS = 8
GM_GROUP_DIM = 128
GM_CHUNK = 128
CONV_WIDTH = 3

LANES = 128
SUBLANES = 8
MXU_COLS = 256
HEADS_PER_LANE_TILE = LANES // SB_HEAD_DIM
HEADS_PER_MXU_TILE = MXU_COLS // SB_HEAD_DIM
SEQ_TILE = MXU_COLS
SUB_BLOCKS_PER_STEP = 2
SUB_BLOCK_SKEW = 2
SCRATCH_AND_TEMPORARIES = 2
SB_PROMPT_HEADS_PER_STEP = 16
SB_TILE_VALUES_IN_FLIGHT = 12
V7X_SCOPED_VMEM_BYTES = 60000 * 1024

LOG2E = math.log2(math.e)
F32_EXP2_UNDERFLOW = 150.0
SIGN_BIT = 0x80000000

_ARB = "arbitrary"
_NT = (((1,), (1,)), ((), ()))


def _rms(x, g):
    return x * lax.rsqrt(jnp.mean(x * x, axis=-1, keepdims=True) + EPS) * g


def _dot(a, b):
    return jnp.dot(a, b, preferred_element_type=F32)


def _dot_nt(a, b):
    return lax.dot_general(a, b, _NT, preferred_element_type=F32)


def _params(n_grid, vmem_bytes):
    return pltpu.CompilerParams(
        dimension_semantics=(_ARB,) * n_grid,
        vmem_limit_bytes=int(min(vmem_bytes, V7X_SCOPED_VMEM_BYTES)))


def _full(shape):
    nd = len(shape)
    return pl.BlockSpec(shape, lambda *_: (0,) * nd, pipeline_mode=pl.Buffered(1))


def _proj_kernel(x_ref, gpre_ref, win_ref, wk_ref, wv_ref, lng_ref, lnb_ref, ws_ref, bias_ref, wg_ref, bg_ref,
                 ya_ref, q_ref, k_ref, v_ref, kb_ref, vb_ref, gate_ref, *rest,
                 sub, n_sub, chunk, transposed_kv, emit_va):
    if emit_va:
        va_ref, hb_s, u_s, gv_s, vab_s = rest
    else:
        hb_s, u_s, gv_s, vab_s = rest
    gm = GM_GROUPS * GM_GROUP_DIM
    sb = SB_HEADS * SB_HEAD_DIM
    qscale = SB_HEAD_DIM ** -0.5 * LOG2E
    nch = sub // chunk

    def stages_of(i):
        base = i * sub
        tok = slice(base, base + sub)

        def norm():
            hb_s[i] = _rms(x_ref[tok, :], gpre_ref[...]).astype(BF16)

        def proj(col0, n):
            cols = slice(col0 + n * MXU_COLS, col0 + (n + 1) * MXU_COLS)
            return _dot(hb_s[i], win_ref[:, cols])

        def gating_inputs(n):
            oc = slice(n * MXU_COLS, (n + 1) * MXU_COLS)
            u_s[i, :, oc] = jax.nn.gelu(proj(0, n))
            gv_s[i, :, oc] = jax.nn.gelu(proj(gm, n))

        def layer_norm():
            gv = gv_s[i]
            mu = jnp.mean(gv, axis=-1, keepdims=True)
            dev = gv - mu
            var = jnp.mean(dev * dev, axis=-1, keepdims=True)
            va = dev * lax.rsqrt(var + EPS) * lng_ref[...] + lnb_ref[...]
            if emit_va:
                va_ref[tok, :] = va
            vab_s[i] = va.astype(BF16)

        def spatial_gating(g):
            cols = slice(g * GM_GROUP_DIM, (g + 1) * GM_GROUP_DIM)
            pieces = [vab_s[i, c * chunk:(c + 1) * chunk, cols] for c in range(nch)]
            mixed = _dot(ws_ref[g], pieces[0] if nch == 1 else jnp.concatenate(pieces, axis=1))
            for c in range(nch):
                rows = slice(c * chunk, (c + 1) * chunk)
                m = mixed[:, c * GM_GROUP_DIM:(c + 1) * GM_GROUP_DIM] + bias_ref[:, cols]
                ya_ref[base + c * chunk:base + (c + 1) * chunk, cols] = (u_s[i, rows, cols] * m).astype(BF16)

        def queries(n):
            q_ref[tok, n * MXU_COLS:(n + 1) * MXU_COLS] = (proj(2 * gm, n) * qscale).astype(BF16)

        def keys_or_values(w_ref, f32_out, bf16_out, n):
            oc = slice(n * MXU_COLS, (n + 1) * MXU_COLS)
            if transposed_kv:
                pt = _dot_nt(w_ref[oc, :], hb_s[i])
                f32_out[oc, tok] = pt
                bf16_out[oc, tok] = pt.astype(BF16)
            else:
                p = _dot(hb_s[i], w_ref[:, oc])
                bf16_out[tok, oc] = p.astype(BF16)
                for hs in range(HEADS_PER_MXU_TILE):
                    h = n * HEADS_PER_MXU_TILE + hs
                    f32_out[pl.ds(base * SB_HEADS + h, sub, stride=SB_HEADS), :] = (
                        p[:, hs * SB_HEAD_DIM:(hs + 1) * SB_HEAD_DIM])

        def gates(n):
            oc = slice(n * MXU_COLS, (n + 1) * MXU_COLS)
            gate_ref[tok, oc] = jax.nn.sigmoid(_dot(hb_s[i], wg_ref[:, oc]) + bg_ref[:, oc]).astype(BF16)

        part = functools.partial
        n_sb = sb // MXU_COLS
        return ([norm] + [part(gating_inputs, n) for n in range(gm // MXU_COLS)]
                + [part(queries, n) for n in range(n_sb)] + [layer_norm]
                + [part(keys_or_values, wk_ref, k_ref, kb_ref, n) for n in range(n_sb)]
                + [part(spatial_gating, g) for g in range(GM_GROUPS)]
                + [part(keys_or_values, wv_ref, v_ref, vb_ref, n) for n in range(n_sb)]
                + [part(gates, n) for n in range(gate_ref.shape[1] // MXU_COLS)])

    _skewed([stages_of(i) for i in range(n_sub)], SUB_BLOCK_SKEW)


def _proj_call(x2d, gpre, win, wk, wv, lng, lnb, ws, bias, wg, bg, *, sub, n_sub, chunk, seq_len, transposed_kv,
               emit_va):
    n_tok, d = x2d.shape
    gm = GM_GROUPS * GM_GROUP_DIM
    sb = SB_HEADS * SB_HEAD_DIM
    n_gate = wg.shape[1]
    tm = sub * n_sub

    def tok(width):
        return pl.BlockSpec((tm, width), lambda i: (i, 0))

    if transposed_kv:
        nt = seq_len // tm
        kv_spec = pl.BlockSpec((None, sb, tm), lambda i: (i // nt, 0, i % nt))
        kv_f32 = jax.ShapeDtypeStruct((n_tok // seq_len, sb, seq_len), F32)
        kv_bf16 = jax.ShapeDtypeStruct((n_tok // seq_len, sb, seq_len), BF16)
        kv_specs = [kv_spec] * 4
        kv_bytes = 2 * 2 * sb * tm * (4 + 2)
    else:
        hm_spec = pl.BlockSpec((tm * SB_HEADS, SB_HEAD_DIM), lambda i: (i, 0))
        kv_f32 = jax.ShapeDtypeStruct((n_tok * SB_HEADS, SB_HEAD_DIM), F32)
        kv_bf16 = jax.ShapeDtypeStruct((n_tok, sb), BF16)
        kv_specs = [hm_spec, hm_spec, tok(sb), tok(sb)]
        kv_bytes = 2 * 2 * tm * (SB_HEADS * LANES * 4 + sb * 2)
    out_specs = [tok(gm), tok(sb)] + kv_specs + [tok(n_gate)]
    out_shape = [jax.ShapeDtypeStruct((n_tok, gm), BF16), jax.ShapeDtypeStruct((n_tok, sb), BF16),
                 kv_f32, kv_f32, kv_bf16, kv_bf16, jax.ShapeDtypeStruct((n_tok, n_gate), BF16)]
    if emit_va:
        out_specs.append(tok(gm))
        out_shape.append(jax.ShapeDtypeStruct((n_tok, gm), F32))

    weight_bytes = (win.size + wk.size + wv.size + wg.size + ws.size) * 2
    block_bytes = 2 * tm * (d * 4 + gm * 2 + sb * 2 + n_gate * 2 + gm * 4) + kv_bytes
    scratch_bytes = tm * (d * 2 + gm * 4 + gm * 4 + gm * 2)
    return pl.pallas_call(
        functools.partial(_proj_kernel, sub=sub, n_sub=n_sub, chunk=chunk, transposed_kv=transposed_kv,
                          emit_va=emit_va),
        grid=(n_tok // tm,),
        in_specs=[tok(d), _full(gpre.shape), _full(win.shape), _full(wk.shape), _full(wv.shape),
                  _full(lng.shape), _full(lnb.shape), _full(ws.shape), _full(bias.shape), _full(wg.shape),
                  _full(bg.shape)],
        out_specs=out_specs,
        out_shape=out_shape,
        scratch_shapes=[pltpu.VMEM((n_sub, sub, d), BF16), pltpu.VMEM((n_sub, sub, gm), F32),
                        pltpu.VMEM((n_sub, sub, gm), F32), pltpu.VMEM((n_sub, sub, gm), BF16)],
        compiler_params=_params(1, weight_bytes + block_bytes + SCRATCH_AND_TEMPORARIES * scratch_bytes),
        name="proj",
    )(x2d, gpre, win, wk, wv, lng, lnb, ws, bias, wg, bg)


class _Tile(NamedTuple):
    head: int
    scores: Callable
    values: Callable
    v_transposed: bool
    tri: jax.Array
    mask: Optional[jax.Array]
    first: bool


def _sb_tiles(tiles, c_ref, cmin_ref, acc_ref):
    n = len(tiles)
    z, sp, cum, weight = ([None] * n for _ in range(4))

    def scores(i):
        z[i] = tiles[i].scores()

    def softplus(i):
        neg_abs = pltpu.bitcast(pltpu.bitcast(z[i], jnp.uint32) | jnp.uint32(SIGN_BIT), F32)
        s = jnp.maximum(z[i], 0.0) + jnp.log2(1.0 + jnp.exp2(neg_abs))
        sp[i] = s if tiles[i].mask is None else jnp.where(tiles[i].mask, s, 0.0)

    def cumsum(i):
        cum[i] = _dot(sp[i].astype(BF16), tiles[i].tri)

    def weights(i):
        h = tiles[i].head
        total = sp[i] + cum[i]
        if not tiles[i].first:
            total = total + c_ref[h]
        a = jnp.exp2(z[i] - total)
        if tiles[i].mask is not None:
            a = jnp.where(tiles[i].mask, a, 0.0)
        weight[i] = a.astype(BF16)
        c_ref[h] = total[:, 0:1]
        tq, width = total.shape[0], min(LANES, total.shape[1])
        cmin_ref[h, :, 0:width] = jnp.min(total[:, 0:width].reshape(tq // SUBLANES, SUBLANES, width), axis=0)
        z[i] = sp[i] = cum[i] = None

    def values(i):
        v = tiles[i].values()
        contrib = _dot_nt(weight[i], v) if tiles[i].v_transposed else _dot(weight[i], v)
        h = tiles[i].head
        lanes = _pair_lanes(h)
        if tiles[i].first and h % HEADS_PER_LANE_TILE == 0:
            acc_ref[:, lanes] = contrib
        else:
            acc_ref[:, lanes] = acc_ref[:, lanes] + contrib
        weight[i] = None

    stages = (scores, softplus, cumsum, weights, values)
    for step in range(n + len(stages) - 1):
        for s, stage in enumerate(stages):
            if 0 <= step - s < n:
                stage(step - s)


def _pair_lanes(h):
    return slice(h // HEADS_PER_LANE_TILE * LANES, (h // HEADS_PER_LANE_TILE + 1) * LANES)


def _strict_tri(n):
    r = lax.broadcasted_iota(jnp.int32, (n, n), 0)
    s = lax.broadcasted_iota(jnp.int32, (n, n), 1)
    return jnp.where(r > s, 1.0, 0.0).astype(BF16)


def _causal_mask(n):
    r = lax.broadcasted_iota(jnp.int32, (n, n), 0)
    s = lax.broadcasted_iota(jnp.int32, (n, n), 1)
    return s < r


def _head_lane_masks():
    lane = lax.broadcasted_iota(jnp.int32, (1, LANES), 1)
    return [(lane >= hh * SB_HEAD_DIM) & (lane < (hh + 1) * SB_HEAD_DIM) for hh in range(HEADS_PER_LANE_TILE)]


def _head_row_masks():
    row = lax.broadcasted_iota(jnp.int32, (LANES, 1), 0)
    return [(row >= hh * SB_HEAD_DIM) & (row < (hh + 1) * SB_HEAD_DIM) for hh in range(HEADS_PER_LANE_TILE)]


def _masked_queries(q_ref, rows, qm_s, heads):
    lane_masks = _head_lane_masks()
    for h in range(heads):
        lt = h // HEADS_PER_LANE_TILE
        qt = q_ref[rows, lt * LANES:(lt + 1) * LANES]
        qm_s[h] = jnp.where(lane_masks[h % HEADS_PER_LANE_TILE], qt, jnp.zeros_like(qt))


def _sweep_is_live(cmin_ref):
    return (jnp.min(cmin_ref[:, :, 0:1]) < F32_EXP2_UNDERFLOW).astype(jnp.int32)


def _sb_prompt_kernel(q_ref, kt_ref, vt_ref, o_ref, acc_s, c_s, cmin_s, u_s, qm_s, *, tq, heads):
    n_blocks = q_ref.shape[0] // tq
    u_s[...] = _strict_tri(tq)
    row_masks = _head_row_masks()

    def tiles(j, diagonal):
        keys = pl.ds(pl.multiple_of(j * tq, tq), tq)
        mask = _causal_mask(tq) if diagonal else None

        def scores(h):
            return _dot(qm_s[h], kt_ref[_pair_lanes(h), keys])

        def values(h):
            vt = vt_ref[_pair_lanes(h), keys]
            return jnp.where(row_masks[h % HEADS_PER_LANE_TILE], vt, jnp.zeros_like(vt))

        return [_Tile(h, functools.partial(scores, h), functools.partial(values, h), True, u_s[...], mask,
                      first=diagonal) for h in range(heads)]

    def query_block(i, has_previous):
        rows = pl.ds(pl.multiple_of(i * tq, tq), tq)
        _masked_queries(q_ref, rows, qm_s, heads)
        if has_previous:
            _sb_tiles(tiles(i, True) + tiles(i - 1, False), c_s, cmin_s, acc_s)

            def cond(carry):
                n, go = carry
                return jnp.logical_and(n < i - 1, go > 0)

            def body(carry):
                n, _ = carry
                _sb_tiles(tiles(i - 2 - n, False), c_s, cmin_s, acc_s)
                return n + 1, _sweep_is_live(cmin_s)

            lax.while_loop(cond, body, (jnp.int32(0), _sweep_is_live(cmin_s)))
        else:
            _sb_tiles(tiles(i, True), c_s, cmin_s, acc_s)
        o_ref[rows, :] = acc_s[...].astype(BF16)

    query_block(0, False)

    def later_blocks(i, carry):
        query_block(i, True)
        return carry

    lax.fori_loop(1, n_blocks, later_blocks, 0)


def _sb_prompt_call(q, kt, vt, *, tq, heads):
    b, t, w = q.shape
    width = heads * SB_HEAD_DIM
    qspec = pl.BlockSpec((None, t, width), lambda bi, hg: (bi, 0, hg))
    kvspec = pl.BlockSpec((None, width, t), lambda bi, hg: (bi, hg, 0))
    vmem = (2 * 4 * t * width * 2 + tq * width * 4 + heads * tq * LANES * 4
            + tq * tq * 2 + heads * tq * LANES * 2 + heads * SB_TILE_VALUES_IN_FLIGHT * tq * tq * 4)
    return pl.pallas_call(
        functools.partial(_sb_prompt_kernel, tq=tq, heads=heads),
        grid=(b, w // width),
        in_specs=[qspec, kvspec, kvspec],
        out_specs=qspec,
        out_shape=jax.ShapeDtypeStruct((b, t, w), BF16),
        scratch_shapes=[pltpu.VMEM((tq, width), F32),
                        pltpu.VMEM((heads, tq, 1), F32),
                        pltpu.VMEM((heads, SUBLANES, LANES), F32),
                        pltpu.VMEM((tq, tq), BF16),
                        pltpu.VMEM((heads, tq, LANES), BF16)],
        compiler_params=_params(2, vmem),
        name="sb_prompt",
    )(q, kt, vt)


def _sb_sample_kernel(q_ref, kn_ref, vn_ref, kp_hbm, vp_hbm, o_ref, kbuf, vbuf, sem, acc_s, c_s, cmin_s, u_s, ud_s,
                      qm_s, *, tq, tk, n_past):
    b = pl.program_id(0)

    def copies(j, slot):
        keys = pl.ds(pl.multiple_of(j * tk, tk), tk)
        return (pltpu.make_async_copy(kp_hbm.at[b, :, keys], kbuf.at[slot], sem.at[0, slot]),
                pltpu.make_async_copy(vp_hbm.at[b, :, keys], vbuf.at[slot], sem.at[1, slot]))

    def start(j, slot):
        for cp in copies(j, slot):
            cp.start()

    def wait(j, slot):
        for cp in copies(j, slot):
            cp.wait()

    start(n_past - 1, 0)
    u_s[...] = _strict_tri(tk)
    ud_s[...] = _strict_tri(tq)
    _masked_queries(q_ref, slice(None), qm_s, SB_HEADS)
    lane_masks = _head_lane_masks()
    row_masks = _head_row_masks()

    def z_new(h):
        return _dot_nt(qm_s[h], kn_ref[:, _pair_lanes(h)])

    def v_new(h):
        vn = vn_ref[:, _pair_lanes(h)]
        return jnp.where(lane_masks[h % HEADS_PER_LANE_TILE], vn, jnp.zeros_like(vn))

    mask = _causal_mask(tq)
    _sb_tiles([_Tile(h, functools.partial(z_new, h), functools.partial(v_new, h), False, ud_s[...], mask,
                     first=True) for h in range(SB_HEADS)], c_s, cmin_s, acc_s)

    def cond(carry):
        n, go = carry
        return jnp.logical_and(n < n_past, go > 0)

    def body(carry):
        n, _ = carry
        j = n_past - 1 - n
        slot = lax.rem(n, 2)
        wait(j, slot)

        @pl.when(n + 1 < n_past)
        def _():
            start(j - 1, 1 - slot)

        def z_past(h):
            return _dot(qm_s[h], kbuf[slot, _pair_lanes(h), :].astype(BF16))

        def v_past(h):
            vt = vbuf[slot, _pair_lanes(h), :].astype(BF16)
            return jnp.where(row_masks[h % HEADS_PER_LANE_TILE], vt, jnp.zeros_like(vt))

        _sb_tiles([_Tile(h, functools.partial(z_past, h), functools.partial(v_past, h), True, u_s[...], None,
                         first=False) for h in range(SB_HEADS)], c_s, cmin_s, acc_s)
        return n + 1, _sweep_is_live(cmin_s)

    n_done, _ = lax.while_loop(cond, body, (jnp.int32(0), _sweep_is_live(cmin_s)))

    @pl.when(n_done < n_past)
    def _():
        wait(n_past - 1 - n_done, lax.rem(n_done, 2))

    o_ref[...] = acc_s[...].astype(BF16)


def _sb_sample_call(q, k_new, v_new, kt_past, vt_past, *, tk):
    b, tq, w = q.shape
    p = kt_past.shape[2]
    newspec = pl.BlockSpec((None, tq, w), lambda bi: (bi, 0, 0))
    anyspec = pl.BlockSpec(memory_space=pl.ANY)
    vmem = (2 * 2 * w * tk * 4 + 2 * 4 * tq * w * 2 + tq * w * 4 + SB_HEADS * tq * LANES * 4
            + 2 * tk * tk * 2 + SB_HEADS * tq * LANES * 2 + SB_HEADS * SB_TILE_VALUES_IN_FLIGHT * tq * tk * 4)
    return pl.pallas_call(
        functools.partial(_sb_sample_kernel, tq=tq, tk=tk, n_past=p // tk),
        grid=(b,),
        in_specs=[newspec, newspec, newspec, anyspec, anyspec],
        out_specs=newspec,
        out_shape=jax.ShapeDtypeStruct((b, tq, w), BF16),
        scratch_shapes=[pltpu.VMEM((2, w, tk), F32),
                        pltpu.VMEM((2, w, tk), F32),
                        pltpu.SemaphoreType.DMA((2, 2)),
                        pltpu.VMEM((tq, w), F32),
                        pltpu.VMEM((SB_HEADS, tq, 1), F32),
                        pltpu.VMEM((SB_HEADS, SUBLANES, LANES), F32),
                        pltpu.VMEM((tk, tk), BF16),
                        pltpu.VMEM((tq, tq), BF16),
                        pltpu.VMEM((SB_HEADS, tq, LANES), BF16)],
        compiler_params=_params(1, vmem),
        name="sb_sample",
    )(q, k_new, v_new, kt_past, vt_past)


def _merge_kernel(x_ref, ya_ref, yb_ref, gate_ref, wa_ref, wb_ref, wo_ref, gpost_ref, o_ref, m_s, p_s,
                  *, sub, n_sub):
    d = x_ref.shape[1]

    def stages_of(i):
        tok = slice(i * sub, (i + 1) * sub)

        def merge(n):
            cols = slice(n * MXU_COLS, (n + 1) * MXU_COLS)
            gcols = slice(d + n * MXU_COLS, d + (n + 1) * MXU_COLS)
            merged = (gate_ref[tok, cols].astype(F32) * _dot(ya_ref[tok, :], wa_ref[:, cols])
                      + gate_ref[tok, gcols].astype(F32) * _dot(yb_ref[tok, :], wb_ref[:, cols]))
            m_s[i, :, cols] = merged.astype(BF16)

        def project(n):
            cols = slice(n * MXU_COLS, (n + 1) * MXU_COLS)
            p_s[i, :, cols] = _dot(m_s[i], wo_ref[:, cols])

        def norm_and_store():
            o_ref[tok, :] = x_ref[tok, :] + _rms(p_s[i], gpost_ref[...])

        n_chunks = d // MXU_COLS
        return ([functools.partial(merge, n) for n in range(n_chunks)]
                + [functools.partial(project, n) for n in range(n_chunks)] + [norm_and_store])

    _skewed([stages_of(i) for i in range(n_sub)], SUB_BLOCK_SKEW)


def _merge_call(x2d, ya, yb, gates, wa, wb, wo, gpost, *, sub, n_sub):
    n_tok, d = x2d.shape
    tm = sub * n_sub

    def tok(width):
        return pl.BlockSpec((tm, width), lambda i: (i, 0))

    weight_bytes = (wa.size + wb.size + wo.size) * 2
    block_bytes = 2 * tm * (d * 4 + ya.shape[1] * 2 + yb.shape[1] * 2 + gates.shape[1] * 2 + d * 4)
    scratch_bytes = tm * d * 6
    return pl.pallas_call(
        functools.partial(_merge_kernel, sub=sub, n_sub=n_sub),
        grid=(n_tok // tm,),
        in_specs=[tok(d), tok(ya.shape[1]), tok(yb.shape[1]), tok(gates.shape[1]),
                  _full(wa.shape), _full(wb.shape), _full(wo.shape), _full(gpost.shape)],
        out_specs=tok(d),
        out_shape=jax.ShapeDtypeStruct((n_tok, d), F32),
        scratch_shapes=[pltpu.VMEM((n_sub, sub, d), BF16), pltpu.VMEM((n_sub, sub, d), F32)],
        compiler_params=_params(1, weight_bytes + block_bytes + SCRATCH_AND_TEMPORARIES * scratch_bytes),
        name="merge",
    )(x2d, ya, yb, gates, wa, wb, wo, gpost)


def _skewed(stage_lists, skew):
    n_stages = len(stage_lists[0])
    for step in range(n_stages + skew * (len(stage_lists) - 1)):
        for i, stages in enumerate(stage_lists):
            if 0 <= step - i * skew < n_stages:
                stages[step - i * skew]()


def _ffn_kernel(x_ref, prev_ref, gpre_ref, wup_ref, wc_ref, bc_ref, wdown_ref, gpost_ref,
                y_ref, newconv_ref, xp_s, hn_s, carry_s, act_s, f_s, *, sub, n_sub):
    assert CONV_WIDTH == 3
    dff = wdown_ref.shape[0]
    d = x_ref.shape[1]
    ng = sub // SUBLANES
    assert ng >= CONV_WIDTH - 1 and ng % SUBLANES == 0
    n_slabs = d // LANES
    t = pl.program_id(1)

    @pl.when(t == 0)
    def _():
        carry_s[0:CONV_WIDTH - 1, :] = prev_ref[...]

    top = lax.broadcasted_iota(jnp.int32, (SUBLANES, MXU_COLS), 0) == 0

    def permuted_rows(r):
        return pl.ds((SUBLANES * r % ng) * SUBLANES + SUBLANES * r // ng, SUBLANES, stride=SUBLANES)

    def stages_of(i):
        base = i * sub

        def load_and_norm():
            for r in range(ng):
                rows = slice(base + r * SUBLANES, base + (r + 1) * SUBLANES)
                for c in range(n_slabs):
                    xp_s[i, c, permuted_rows(r), :] = x_ref[rows, c * LANES:(c + 1) * LANES]
            xp = jnp.concatenate([xp_s[i, c] for c in range(n_slabs)], axis=1)
            hn_s[i] = _rms(xp, gpre_ref[...]).astype(BF16)

        def up_conv_act(n):
            halves = []
            for off in (0, dff):
                cols = slice(off + n * MXU_COLS, off + (n + 1) * MXU_COLS)
                up = _dot(hn_s[i], wup_ref[:, cols])
                back1 = jnp.where(top, carry_s[1:2, cols], pltpu.roll(up[sub - SUBLANES:sub], 1, axis=0))
                back2 = jnp.where(top, carry_s[0:1, cols],
                                  pltpu.roll(up[sub - 2 * SUBLANES:sub - SUBLANES], 1, axis=0))
                carry_s[0:1, cols] = up[sub - SUBLANES - 1:sub - SUBLANES]
                carry_s[1:2, cols] = up[sub - 1:sub]
                prev1 = jnp.concatenate([back1, up[:sub - SUBLANES]], axis=0)
                prev2 = jnp.concatenate([back2, back1, up[:sub - 2 * SUBLANES]], axis=0)
                halves.append(bc_ref[:, cols] + wc_ref[0:1, cols] * prev2 + wc_ref[1:2, cols] * prev1
                              + wc_ref[2:3, cols] * up)
            act_s[i, :, n * MXU_COLS:(n + 1) * MXU_COLS] = (jax.nn.gelu(halves[0]) * halves[1]).astype(BF16)

        def down(n):
            cols = slice(n * MXU_COLS, (n + 1) * MXU_COLS)
            f_s[i, :, cols] = _dot(act_s[i], wdown_ref[:, cols])

        def norm_and_store():
            normed = _rms(f_s[i], gpost_ref[...])
            for c in range(n_slabs):
                xp_s[i, c] = xp_s[i, c] + normed[:, c * LANES:(c + 1) * LANES]
            for r in range(ng):
                rows = slice(base + r * SUBLANES, base + (r + 1) * SUBLANES)
                for c in range(n_slabs):
                    y_ref[rows, c * LANES:(c + 1) * LANES] = xp_s[i, c, permuted_rows(r), :]
            if i == n_sub - 1:
                newconv_ref[...] = carry_s[0:CONV_WIDTH - 1, :]

        return ([load_and_norm] + [functools.partial(up_conv_act, n) for n in range(dff // MXU_COLS)]
                + [functools.partial(down, n) for n in range(d // MXU_COLS)] + [norm_and_store])

    _skewed([stages_of(i) for i in range(n_sub)], SUB_BLOCK_SKEW)


def _ffn_call(x, conv_prev, gpre, wup, wc, bc, wdown, gpost, *, sub, n_sub):
    b, t, d = x.shape
    dff = wdown.shape[0]
    tm = sub * n_sub
    xspec = pl.BlockSpec((None, tm, d), lambda bi, ti: (bi, ti, 0))
    cspec = pl.BlockSpec((None, CONV_WIDTH - 1, 2 * dff), lambda bi, ti: (bi, 0, 0))
    weight_bytes = (wup.size + wdown.size) * 2
    block_bytes = 2 * 2 * tm * d * 4
    scratch_bytes = tm * d * 4 + tm * d * 2 + SUBLANES * 2 * dff * 4 + tm * dff * 2 + tm * d * 4
    return pl.pallas_call(
        functools.partial(_ffn_kernel, sub=sub, n_sub=n_sub),
        grid=(b, t // tm),
        in_specs=[xspec, cspec, _full(gpre.shape), _full(wup.shape), _full(wc.shape), _full(bc.shape),
                  _full(wdown.shape), _full(gpost.shape)],
        out_specs=[xspec, cspec],
        out_shape=[jax.ShapeDtypeStruct((b, t, d), F32),
                   jax.ShapeDtypeStruct((b, CONV_WIDTH - 1, 2 * dff), F32)],
        scratch_shapes=[pltpu.VMEM((n_sub, d // LANES, sub, LANES), F32),
                        pltpu.VMEM((n_sub, sub, d), BF16),
                        pltpu.VMEM((SUBLANES, 2 * dff), F32),
                        pltpu.VMEM((n_sub, sub, dff), BF16),
                        pltpu.VMEM((n_sub, sub, d), F32)],
        compiler_params=_params(2, weight_bytes + block_bytes + SCRATCH_AND_TEMPORARIES * scratch_bytes),
        name="ffn",
    )(x, conv_prev, gpre, wup, wc, bc, wdown, gpost)


def _layer(x, past_k, past_v, conv_prev, w):
    b, t, d = x.shape
    gm = GM_GROUPS * GM_GROUP_DIM
    sbw = SB_HEADS * SB_HEAD_DIM
    chunk = min(t, GM_CHUNK)
    n_tok = b * t
    decode = past_k is not None
    tm = min(SEQ_TILE, t)
    n_sub = min(SUB_BLOCKS_PER_STEP, t // tm)
    assert t % (tm * n_sub) == 0 and tm % chunk == 0, (t, tm, n_sub, chunk)

    tri = jnp.tril(jnp.ones((chunk, chunk), dtype=bool))
    ws = jnp.where(tri[None], w["w_s"][:, :chunk, :chunk], 0.0).astype(BF16)
    bias = jnp.repeat(w["b_s"][:, :chunk].T, GM_GROUP_DIM, axis=1)
    w_in = w["w_in"]
    w_uvq = w_in[:, :2 * gm + sbw]
    w_k = w_in[:, 2 * gm + sbw:2 * gm + 2 * sbw]
    w_v = w_in[:, 2 * gm + 2 * sbw:]
    if not decode:
        w_k, w_v = w_k.T, w_v.T

    flat_n_sub = min(SUB_BLOCKS_PER_STEP, n_tok // tm)
    flat_sub = max(tm, min(SEQ_TILE, n_tok // flat_n_sub))
    assert n_tok % (flat_sub * flat_n_sub) == 0 and flat_sub % chunk == 0, (n_tok, flat_sub, flat_n_sub)
    outs = _proj_call(
        x.reshape(n_tok, d), w["g_pre_mix"], w_uvq, w_k, w_v, w["ln_v_g"], w["ln_v_b"], ws, bias,
        w["w_gate"], w["b_gate"], sub=flat_sub if decode else tm, n_sub=flat_n_sub if decode else n_sub,
        chunk=chunk, seq_len=t, transposed_kv=not decode, emit_va=decode)
    ya, q, k_out, v_out, kb, vb, gates = outs[:7]
    q3 = q.reshape(b, t, sbw)

    if decode:
        p = past_k.shape[1]
        kt_past = jnp.transpose(past_k, (0, 2, 3, 1)).reshape(b, sbw, p)
        vt_past = jnp.transpose(past_v, (0, 2, 3, 1)).reshape(b, sbw, p)
        yb = _sb_sample_call(q3, kb.reshape(b, t, sbw), vb.reshape(b, t, sbw), kt_past, vt_past,
                             tk=min(p, SEQ_TILE))
        k4 = k_out.reshape(b, t, SB_HEADS, SB_HEAD_DIM)
        v4 = v_out.reshape(b, t, SB_HEADS, SB_HEAD_DIM)
        va = outs[7].reshape(b, t, gm)
    else:
        yb = _sb_prompt_call(q3, kb, vb, tq=tm, heads=SB_PROMPT_HEADS_PER_STEP)
        k4 = jnp.transpose(k_out.reshape(b, SB_HEADS, SB_HEAD_DIM, t), (0, 3, 1, 2))
        v4 = jnp.transpose(v_out.reshape(b, SB_HEADS, SB_HEAD_DIM, t), (0, 3, 1, 2))
        va = None

    wide = min(2 * tm, t // n_sub)
    flat_wide = max(wide, min(2 * flat_sub, n_tok // flat_n_sub))
    x1 = _merge_call(x.reshape(n_tok, d), ya, yb.reshape(n_tok, sbw), gates,
                     w["w_branch_a"], w["w_branch_b"], w["w_o"], w["g_post_mix"], sub=flat_wide, n_sub=flat_n_sub)
    y, conv_new = _ffn_call(x1.reshape(b, t, d), conv_prev, w["g_pre_ffn"], w["w_up"], w["w_conv"],
                            w["b_conv"], w["w_down"], w["g_post_ffn"], sub=wide, n_sub=n_sub)
    return y, k4, v4, conv_new, va


def kernel(x_prompt, x_sample, cache_sb_k, cache_sb_v, state_ffn_conv, g_pre_mix, w_in, ln_v_g, ln_v_b, w_s, b_s,
           w_gate, b_gate, w_branch_a, w_branch_b, w_o, g_post_mix, g_pre_ffn, w_up, w_conv, b_conv, w_down,
           g_post_ffn):
    depth = w_in.shape[0]
    y_p, y_s = x_prompt, x_sample
    outs = [[] for _ in range(7)]
    for l in range(depth):
        row = lambda a: a[l][None, :]
        w = dict(
            g_pre_mix=row(g_pre_mix), w_in=w_in[l].astype(BF16), ln_v_g=row(ln_v_g), ln_v_b=row(ln_v_b),
            w_s=w_s[l], b_s=b_s[l], w_gate=w_gate[l].astype(BF16), b_gate=row(b_gate),
            w_branch_a=w_branch_a[l].astype(BF16), w_branch_b=w_branch_b[l].astype(BF16),
            w_o=w_o[l].astype(BF16), g_post_mix=row(g_post_mix), g_pre_ffn=row(g_pre_ffn),
            w_up=w_up[l].astype(BF16), w_conv=w_conv[l], b_conv=row(b_conv),
            w_down=w_down[l].astype(BF16), g_post_ffn=row(g_post_ffn))
        zero_conv = jnp.zeros((y_p.shape[0], CONV_WIDTH - 1, w_up.shape[2]), y_p.dtype)
        y_p, kp, vp, cp, _ = _layer(y_p, None, None, zero_conv, w)
        y_s, ks, vs, cs, gvs = _layer(y_s, cache_sb_k[l], cache_sb_v[l], state_ffn_conv[l], w)
        for lst, a in zip(outs, (kp, vp, cp, ks, vs, cs, gvs)):
            lst.append(a)
    return (y_p, y_s) + tuple(jnp.stack(o) for o in outs)
```

```python
import functools
import math
from typing import Callable, NamedTuple, Optional

import jax
import jax.numpy as jnp
from jax import lax
from jax.experimental import pallas as pl
from jax.experimental.pallas import tpu as pltpu

F32 = jnp.float32
BF16 = jnp.bfloat16

EPS = 1e-6
SB_HEADS = 16
SB_HEAD_DIM = 64
GM_GROUPS = 8
GM_GROUP_DIM = 128
GM_CHUNK = 128
CONV_WIDTH = 3

LANES = 128
SUBLANES = 8
MXU_COLS = 256
HEADS_PER_LANE_TILE = LANES // SB_HEAD_DIM
HEADS_PER_MXU_TILE = MXU_COLS // SB_HEAD_DIM
SEQ_TILE = MXU_COLS
SUB_BLOCKS_PER_STEP = 2
SUB_BLOCK_SKEW = 1
SCRATCH_AND_TEMPORARIES = 2
SB_PROMPT_HEADS_PER_STEP = 16
SB_TILE_VALUES_IN_FLIGHT = 12
V7X_SCOPED_VMEM_BYTES = 60000 * 1024

LOG2E = math.log2(math.e)
F32_EXP2_UNDERFLOW = 150.0
SIGN_BIT = 0x80000000

_ARB = "arbitrary"
_NT = (((1,), (1,)), ((), ()))


def _rms(x, g):
    return x * lax.rsqrt(jnp.mean(x * x, axis=-1, keepdims=True) + EPS) * g


def _dot(a, b):
    return jnp.dot(a, b, preferred_element_type=F32)


def _dot_nt(a, b):
    return lax.dot_general(a, b, _NT, preferred_element_type=F32)


def _params(n_grid, vmem_bytes):
    return pltpu.CompilerParams(
        dimension_semantics=(_ARB,) * n_grid,
        vmem_limit_bytes=int(min(vmem_bytes, V7X_SCOPED_VMEM_BYTES)))


def _full(shape):
    nd = len(shape)
    return pl.BlockSpec(shape, lambda *_: (0,) * nd, pipeline_mode=pl.Buffered(1))


def _proj_kernel(x_ref, gpre_ref, win_ref, wk_ref, wv_ref, lng_ref, lnb_ref, ws_ref, bias_ref, wg_ref, bg_ref,
                 ya_ref, q_ref, k_ref, v_ref, kb_ref, vb_ref, gate_ref, *rest,
                 sub, n_sub, chunk, transposed_kv, emit_va):
    if emit_va:
        va_ref, hb_s, u_s, gv_s, vab_s = rest
    else:
        hb_s, u_s, gv_s, vab_s = rest
    gm = GM_GROUPS * GM_GROUP_DIM
    sb = SB_HEADS * SB_HEAD_DIM
    qscale = SB_HEAD_DIM ** -0.5 * LOG2E
    nch = sub // chunk

    def stages_of(i):
        base = i * sub
        tok = slice(base, base + sub)

        def norm():
            hb_s[i] = _rms(x_ref[tok, :], gpre_ref[...]).astype(BF16)

        def proj(col0, n):
            cols = slice(col0 + n * MXU_COLS, col0 + (n + 1) * MXU_COLS)
            return _dot(hb_s[i], win_ref[:, cols])

        def gating_inputs(n):
            oc = slice(n * MXU_COLS, (n + 1) * MXU_COLS)
            u_s[i, :, oc] = jax.nn.gelu(proj(0, n))
            gv_s[i, :, oc] = jax.nn.gelu(proj(gm, n))

        def layer_norm():
            gv = gv_s[i]
            mu = jnp.mean(gv, axis=-1, keepdims=True)
            dev = gv - mu
            var = jnp.mean(dev * dev, axis=-1, keepdims=True)
            va = dev * lax.rsqrt(var + EPS) * lng_ref[...] + lnb_ref[...]
            if emit_va:
                va_ref[tok, :] = va
            vab_s[i] = va.astype(BF16)

        def spatial_gating(g):
            cols = slice(g * GM_GROUP_DIM, (g + 1) * GM_GROUP_DIM)
            pieces = [vab_s[i, c * chunk:(c + 1) * chunk, cols] for c in range(nch)]
            mixed = _dot(ws_ref[g], pieces[0] if nch == 1 else jnp.concatenate(pieces, axis=1))
            for c in range(nch):
                rows = slice(c * chunk, (c + 1) * chunk)
                m = mixed[:, c * GM_GROUP_DIM:(c + 1) * GM_GROUP_DIM] + bias_ref[:, cols]
                ya_ref[base + c * chunk:base + (c + 1) * chunk, cols] = (u_s[i, rows, cols] * m).astype(BF16)

        def queries(n):
            q_ref[tok, n * MXU_COLS:(n + 1) * MXU_COLS] = (proj(2 * gm, n) * qscale).astype(BF16)

        def keys_or_values(w_ref, f32_out, bf16_out, n):
            oc = slice(n * MXU_COLS, (n + 1) * MXU_COLS)
            if transposed_kv:
                pt = _dot_nt(w_ref[oc, :], hb_s[i])
                f32_out[oc, tok] = pt
                bf16_out[oc, tok] = pt.astype(BF16)
            else:
                p = _dot(hb_s[i], w_ref[:, oc])
                bf16_out[tok, oc] = p.astype(BF16)
                for hs in range(HEADS_PER_MXU_TILE):
                    h = n * HEADS_PER_MXU_TILE + hs
                    f32_out[pl.ds(base * SB_HEADS + h, sub, stride=SB_HEADS), :] = (
                        p[:, hs * SB_HEAD_DIM:(hs + 1) * SB_HEAD_DIM])

        def gates(n):
            oc = slice(n * MXU_COLS, (n + 1) * MXU_COLS)
            gate_ref[tok, oc] = jax.nn.sigmoid(_dot(hb_s[i], wg_ref[:, oc]) + bg_ref[:, oc]).astype(BF16)

        part = functools.partial
        n_sb = sb // MXU_COLS
        return ([norm] + [part(gating_inputs, n) for n in range(gm // MXU_COLS)]
                + [part(queries, n) for n in range(n_sb)] + [layer_norm]
                + [part(keys_or_values, wk_ref, k_ref, kb_ref, n) for n in range(n_sb)]
                + [part(spatial_gating, g) for g in range(GM_GROUPS)]
                + [part(keys_or_values, wv_ref, v_ref, vb_ref, n) for n in range(n_sb)]
                + [part(gates, n) for n in range(gate_ref.shape[1] // MXU_COLS)])

    _skewed([stages_of(i) for i in range(n_sub)], SUB_BLOCK_SKEW)


def _proj_call(x2d, gpre, win, wk, wv, lng, lnb, ws, bias, wg, bg, *, sub, n_sub, chunk, seq_len, transposed_kv,
               emit_va):
    n_tok, d = x2d.shape
    gm = GM_GROUPS * GM_GROUP_DIM
    sb = SB_HEADS * SB_HEAD_DIM
    n_gate = wg.shape[1]
    tm = sub * n_sub

    def tok(width):
        return pl.BlockSpec((tm, width), lambda i: (i, 0))

    if transposed_kv:
        nt = seq_len // tm
        kv_spec = pl.BlockSpec((None, sb, tm), lambda i: (i // nt, 0, i % nt))
        kv_f32 = jax.ShapeDtypeStruct((n_tok // seq_len, sb, seq_len), F32)
        kv_bf16 = jax.ShapeDtypeStruct((n_tok // seq_len, sb, seq_len), BF16)
        kv_specs = [kv_spec] * 4
        kv_bytes = 2 * 2 * sb * tm * (4 + 2)
    else:
        hm_spec = pl.BlockSpec((tm * SB_HEADS, SB_HEAD_DIM), lambda i: (i, 0))
        kv_f32 = jax.ShapeDtypeStruct((n_tok * SB_HEADS, SB_HEAD_DIM), F32)
        kv_bf16 = jax.ShapeDtypeStruct((n_tok, sb), BF16)
        kv_specs = [hm_spec, hm_spec, tok(sb), tok(sb)]
        kv_bytes = 2 * 2 * tm * (SB_HEADS * LANES * 4 + sb * 2)
    out_specs = [tok(gm), tok(sb)] + kv_specs + [tok(n_gate)]
    out_shape = [jax.ShapeDtypeStruct((n_tok, gm), BF16), jax.ShapeDtypeStruct((n_tok, sb), BF16),
                 kv_f32, kv_f32, kv_bf16, kv_bf16, jax.ShapeDtypeStruct((n_tok, n_gate), BF16)]
    if emit_va:
        out_specs.append(tok(gm))
        out_shape.append(jax.ShapeDtypeStruct((n_tok, gm), F32))

    weight_bytes = (win.size + wk.size + wv.size + wg.size + ws.size) * 2
    block_bytes = 2 * tm * (d * 4 + gm * 2 + sb * 2 + n_gate * 2 + gm * 4) + kv_bytes
    scratch_bytes = tm * (d * 2 + gm * 4 + gm * 4 + gm * 2)
    return pl.pallas_call(
        functools.partial(_proj_kernel, sub=sub, n_sub=n_sub, chunk=chunk, transposed_kv=transposed_kv,
                          emit_va=emit_va),
        grid=(n_tok // tm,),
        in_specs=[tok(d), _full(gpre.shape), _full(win.shape), _full(wk.shape), _full(wv.shape),
                  _full(lng.shape), _full(lnb.shape), _full(ws.shape), _full(bias.shape), _full(wg.shape),
                  _full(bg.shape)],
        out_specs=out_specs,
        out_shape=out_shape,
        scratch_shapes=[pltpu.VMEM((n_sub, sub, d), BF16), pltpu.VMEM((n_sub, sub, gm), F32),
                        pltpu.VMEM((n_sub, sub, gm), F32), pltpu.VMEM((n_sub, sub, gm), BF16)],
        compiler_params=_params(1, weight_bytes + block_bytes + SCRATCH_AND_TEMPORARIES * scratch_bytes),
        name="proj",
    )(x2d, gpre, win, wk, wv, lng, lnb, ws, bias, wg, bg)


class _Tile(NamedTuple):
    head: int
    scores: Callable
    values: Callable
    v_transposed: bool
    tri: jax.Array
    mask: Optional[jax.Array]
    first: bool


def _sb_tiles(tiles, c_ref, cmin_ref, acc_ref):
    n = len(tiles)
    z, sp, cum, weight = ([None] * n for _ in range(4))

    def scores(i):
        z[i] = tiles[i].scores()

    def softplus(i):
        neg_abs = pltpu.bitcast(pltpu.bitcast(z[i], jnp.uint32) | jnp.uint32(SIGN_BIT), F32)
        s = jnp.maximum(z[i], 0.0) + jnp.log2(1.0 + jnp.exp2(neg_abs))
        sp[i] = s if tiles[i].mask is None else jnp.where(tiles[i].mask, s, 0.0)

    def cumsum(i):
        cum[i] = _dot(sp[i].astype(BF16), tiles[i].tri)

    def weights(i):
        h = tiles[i].head
        total = sp[i] + cum[i]
        if not tiles[i].first:
            total = total + c_ref[h]
        a = jnp.exp2(z[i] - total)
        if tiles[i].mask is not None:
            a = jnp.where(tiles[i].mask, a, 0.0)
        weight[i] = a.astype(BF16)
        c_ref[h] = total[:, 0:1]
        tq, width = total.shape[0], min(LANES, total.shape[1])
        cmin_ref[h, :, 0:width] = jnp.min(total[:, 0:width].reshape(tq // SUBLANES, SUBLANES, width), axis=0)
        z[i] = sp[i] = cum[i] = None

    def values(i):
        v = tiles[i].values()
        contrib = _dot_nt(weight[i], v) if tiles[i].v_transposed else _dot(weight[i], v)
        h = tiles[i].head
        lanes = _pair_lanes(h)
        if tiles[i].first and h % HEADS_PER_LANE_TILE == 0:
            acc_ref[:, lanes] = contrib
        else:
            acc_ref[:, lanes] = acc_ref[:, lanes] + contrib
        weight[i] = None

    stages = (scores, softplus, cumsum, weights, values)
    for step in range(n + len(stages) - 1):
        for s, stage in enumerate(stages):
            if 0 <= step - s < n:
                stage(step - s)


def _pair_lanes(h):
    return slice(h // HEADS_PER_LANE_TILE * LANES, (h // HEADS_PER_LANE_TILE + 1) * LANES)


def _strict_tri(n):
    r = lax.broadcasted_iota(jnp.int32, (n, n), 0)
    s = lax.broadcasted_iota(jnp.int32, (n, n), 1)
    return jnp.where(r > s, 1.0, 0.0).astype(BF16)


def _causal_mask(n):
    r = lax.broadcasted_iota(jnp.int32, (n, n), 0)
    s = lax.broadcasted_iota(jnp.int32, (n, n), 1)
    return s < r


def _head_lane_masks():
    lane = lax.broadcasted_iota(jnp.int32, (1, LANES), 1)
    return [(lane >= hh * SB_HEAD_DIM) & (lane < (hh + 1) * SB_HEAD_DIM) for hh in range(HEADS_PER_LANE_TILE)]


def _head_row_masks():
    row = lax.broadcasted_iota(jnp.int32, (LANES, 1), 0)
    return [(row >= hh * SB_HEAD_DIM) & (row < (hh + 1) * SB_HEAD_DIM) for hh in range(HEADS_PER_LANE_TILE)]


def _masked_queries(q_ref, rows, qm_s, heads):
    lane_masks = _head_lane_masks()
    for h in range(heads):
        lt = h // HEADS_PER_LANE_TILE
        qt = q_ref[rows, lt * LANES:(lt + 1) * LANES]
        qm_s[h] = jnp.where(lane_masks[h % HEADS_PER_LANE_TILE], qt, jnp.zeros_like(qt))


def _sweep_is_live(cmin_ref):
    return (jnp.min(cmin_ref[:, :, 0:1]) < F32_EXP2_UNDERFLOW).astype(jnp.int32)


def _sb_prompt_kernel(q_ref, kt_ref, vt_ref, o_ref, acc_s, c_s, cmin_s, u_s, qm_s, *, tq, heads):
    n_blocks = q_ref.shape[0] // tq
    u_s[...] = _strict_tri(tq)
    row_masks = _head_row_masks()

    def tiles(j, diagonal):
        keys = pl.ds(pl.multiple_of(j * tq, tq), tq)
        mask = _causal_mask(tq) if diagonal else None

        def scores(h):
            return _dot(qm_s[h], kt_ref[_pair_lanes(h), keys])

        def values(h):
            vt = vt_ref[_pair_lanes(h), keys]
            return jnp.where(row_masks[h % HEADS_PER_LANE_TILE], vt, jnp.zeros_like(vt))

        return [_Tile(h, functools.partial(scores, h), functools.partial(values, h), True, u_s[...], mask,
                      first=diagonal) for h in range(heads)]

    def query_block(i, has_previous):
        rows = pl.ds(pl.multiple_of(i * tq, tq), tq)
        _masked_queries(q_ref, rows, qm_s, heads)
        if has_previous:
            _sb_tiles(tiles(i, True) + tiles(i - 1, False), c_s, cmin_s, acc_s)

            def cond(carry):
                n, go = carry
                return jnp.logical_and(n < i - 1, go > 0)

            def body(carry):
                n, _ = carry
                _sb_tiles(tiles(i - 2 - n, False), c_s, cmin_s, acc_s)
                return n + 1, _sweep_is_live(cmin_s)

            lax.while_loop(cond, body, (jnp.int32(0), _sweep_is_live(cmin_s)))
        else:
            _sb_tiles(tiles(i, True), c_s, cmin_s, acc_s)
        o_ref[rows, :] = acc_s[...].astype(BF16)

    query_block(0, False)

    def later_blocks(i, carry):
        query_block(i, True)
        return carry

    lax.fori_loop(1, n_blocks, later_blocks, 0)


def _sb_prompt_call(q, kt, vt, *, tq, heads):
    b, t, w = q.shape
    width = heads * SB_HEAD_DIM
    qspec = pl.BlockSpec((None, t, width), lambda bi, hg: (bi, 0, hg))
    kvspec = pl.BlockSpec((None, width, t), lambda bi, hg: (bi, hg, 0))
    vmem = (2 * 4 * t * width * 2 + tq * width * 4 + heads * tq * LANES * 4
            + tq * tq * 2 + heads * tq * LANES * 2 + heads * SB_TILE_VALUES_IN_FLIGHT * tq * tq * 4)
    return pl.pallas_call(
        functools.partial(_sb_prompt_kernel, tq=tq, heads=heads),
        grid=(b, w // width),
        in_specs=[qspec, kvspec, kvspec],
        out_specs=qspec,
        out_shape=jax.ShapeDtypeStruct((b, t, w), BF16),
        scratch_shapes=[pltpu.VMEM((tq, width), F32),
                        pltpu.VMEM((heads, tq, 1), F32),
                        pltpu.VMEM((heads, SUBLANES, LANES), F32),
                        pltpu.VMEM((tq, tq), BF16),
                        pltpu.VMEM((heads, tq, LANES), BF16)],
        compiler_params=_params(2, vmem),
        name="sb_prompt",
    )(q, kt, vt)


def _sb_sample_kernel(q_ref, kn_ref, vn_ref, kp_hbm, vp_hbm, o_ref, kbuf, vbuf, sem, acc_s, c_s, cmin_s, u_s, ud_s,
                      qm_s, *, tq, tk, n_past):
    b = pl.program_id(0)

    def copies(j, slot):
        keys = pl.ds(pl.multiple_of(j * tk, tk), tk)
        return (pltpu.make_async_copy(kp_hbm.at[b, :, keys], kbuf.at[slot], sem.at[0, slot]),
                pltpu.make_async_copy(vp_hbm.at[b, :, keys], vbuf.at[slot], sem.at[1, slot]))

    def start(j, slot):
        for cp in copies(j, slot):
            cp.start()

    def wait(j, slot):
        for cp in copies(j, slot):
            cp.wait()

    start(n_past - 1, 0)
    u_s[...] = _strict_tri(tk)
    ud_s[...] = _strict_tri(tq)
    _masked_queries(q_ref, slice(None), qm_s, SB_HEADS)
    lane_masks = _head_lane_masks()
    row_masks = _head_row_masks()

    def z_new(h):
        return _dot_nt(qm_s[h], kn_ref[:, _pair_lanes(h)])

    def v_new(h):
        vn = vn_ref[:, _pair_lanes(h)]
        return jnp.where(lane_masks[h % HEADS_PER_LANE_TILE], vn, jnp.zeros_like(vn))

    mask = _causal_mask(tq)
    _sb_tiles([_Tile(h, functools.partial(z_new, h), functools.partial(v_new, h), False, ud_s[...], mask,
                     first=True) for h in range(SB_HEADS)], c_s, cmin_s, acc_s)

    def cond(carry):
        n, go = carry
        return jnp.logical_and(n < n_past, go > 0)

    def body(carry):
        n, _ = carry
        j = n_past - 1 - n
        slot = lax.rem(n, 2)
        wait(j, slot)

        @pl.when(n + 1 < n_past)
        def _():
            start(j - 1, 1 - slot)

        def z_past(h):
            return _dot(qm_s[h], kbuf[slot, _pair_lanes(h), :].astype(BF16))

        def v_past(h):
            vt = vbuf[slot, _pair_lanes(h), :].astype(BF16)
            return jnp.where(row_masks[h % HEADS_PER_LANE_TILE], vt, jnp.zeros_like(vt))

        _sb_tiles([_Tile(h, functools.partial(z_past, h), functools.partial(v_past, h), True, u_s[...], None,
                         first=False) for h in range(SB_HEADS)], c_s, cmin_s, acc_s)
        return n + 1, _sweep_is_live(cmin_s)

    n_done, _ = lax.while_loop(cond, body, (jnp.int32(0), _sweep_is_live(cmin_s)))

    @pl.when(n_done < n_past)
    def _():
        wait(n_past - 1 - n_done, lax.rem(n_done, 2))

    o_ref[...] = acc_s[...].astype(BF16)


def _sb_sample_call(q, k_new, v_new, kt_past, vt_past, *, tk):
    b, tq, w = q.shape
    p = kt_past.shape[2]
    newspec = pl.BlockSpec((None, tq, w), lambda bi: (bi, 0, 0))
    anyspec = pl.BlockSpec(memory_space=pl.ANY)
    vmem = (2 * 2 * w * tk * 4 + 2 * 4 * tq * w * 2 + tq * w * 4 + SB_HEADS * tq * LANES * 4
            + 2 * tk * tk * 2 + SB_HEADS * tq * LANES * 2 + SB_HEADS * SB_TILE_VALUES_IN_FLIGHT * tq * tk * 4)
    return pl.pallas_call(
        functools.partial(_sb_sample_kernel, tq=tq, tk=tk, n_past=p // tk),
        grid=(b,),
        in_specs=[newspec, newspec, newspec, anyspec, anyspec],
        out_specs=newspec,
        out_shape=jax.ShapeDtypeStruct((b, tq, w), BF16),
        scratch_shapes=[pltpu.VMEM((2, w, tk), F32),
                        pltpu.VMEM((2, w, tk), F32),
                        pltpu.SemaphoreType.DMA((2, 2)),
                        pltpu.VMEM((tq, w), F32),
                        pltpu.VMEM((SB_HEADS, tq, 1), F32),
                        pltpu.VMEM((SB_HEADS, SUBLANES, LANES), F32),
                        pltpu.VMEM((tk, tk), BF16),
                        pltpu.VMEM((tq, tq), BF16),
                        pltpu.VMEM((SB_HEADS, tq, LANES), BF16)],
        compiler_params=_params(1, vmem),
        name="sb_sample",
    )(q, k_new, v_new, kt_past, vt_past)


def _merge_kernel(x_ref, ya_ref, yb_ref, gate_ref, wa_ref, wb_ref, wo_ref, gpost_ref, o_ref, m_s, p_s,
                  *, sub, n_sub):
    d = x_ref.shape[1]

    def stages_of(i):
        tok = slice(i * sub, (i + 1) * sub)

        def merge(n):
            cols = slice(n * MXU_COLS, (n + 1) * MXU_COLS)
            gcols = slice(d + n * MXU_COLS, d + (n + 1) * MXU_COLS)
            merged = (gate_ref[tok, cols].astype(F32) * _dot(ya_ref[tok, :], wa_ref[:, cols])
                      + gate_ref[tok, gcols].astype(F32) * _dot(yb_ref[tok, :], wb_ref[:, cols]))
            m_s[i, :, cols] = merged.astype(BF16)

        def project(n):
            cols = slice(n * MXU_COLS, (n + 1) * MXU_COLS)
            p_s[i, :, cols] = _dot(m_s[i], wo_ref[:, cols])

        def norm_and_store():
            o_ref[tok, :] = x_ref[tok, :] + _rms(p_s[i], gpost_ref[...])

        n_chunks = d // MXU_COLS
        return ([functools.partial(merge, n) for n in range(n_chunks)]
                + [functools.partial(project, n) for n in range(n_chunks)] + [norm_and_store])

    _skewed([stages_of(i) for i in range(n_sub)], SUB_BLOCK_SKEW)


def _merge_call(x2d, ya, yb, gates, wa, wb, wo, gpost, *, sub, n_sub):
    n_tok, d = x2d.shape
    tm = sub * n_sub

    def tok(width):
        return pl.BlockSpec((tm, width), lambda i: (i, 0))

    weight_bytes = (wa.size + wb.size + wo.size) * 2
    block_bytes = 2 * tm * (d * 4 + ya.shape[1] * 2 + yb.shape[1] * 2 + gates.shape[1] * 2 + d * 4)
    scratch_bytes = tm * d * 6
    return pl.pallas_call(
        functools.partial(_merge_kernel, sub=sub, n_sub=n_sub),
        grid=(n_tok // tm,),
        in_specs=[tok(d), tok(ya.shape[1]), tok(yb.shape[1]), tok(gates.shape[1]),
                  _full(wa.shape), _full(wb.shape), _full(wo.shape), _full(gpost.shape)],
        out_specs=tok(d),
        out_shape=jax.ShapeDtypeStruct((n_tok, d), F32),
        scratch_shapes=[pltpu.VMEM((n_sub, sub, d), BF16), pltpu.VMEM((n_sub, sub, d), F32)],
        compiler_params=_params(1, weight_bytes + block_bytes + SCRATCH_AND_TEMPORARIES * scratch_bytes),
        name="merge",
    )(x2d, ya, yb, gates, wa, wb, wo, gpost)


def _skewed(stage_lists, skew):
    n_stages = len(stage_lists[0])
    for step in range(n_stages + skew * (len(stage_lists) - 1)):
        for i, stages in enumerate(stage_lists):
            if 0 <= step - i * skew < n_stages:
                stages[step - i * skew]()


def _ffn_kernel(x_ref, prev_ref, gpre_ref, wup_ref, wc_ref, bc_ref, wdown_ref, gpost_ref,
                y_ref, newconv_ref, xp_s, hn_s, carry_s, act_s, f_s, *, sub, n_sub):
    assert CONV_WIDTH == 3
    dff = wdown_ref.shape[0]
    d = x_ref.shape[1]
    ng = sub // SUBLANES
    assert ng >= CONV_WIDTH - 1 and ng % SUBLANES == 0
    n_slabs = d // LANES
    t = pl.program_id(1)

    @pl.when(t == 0)
    def _():
        carry_s[0:CONV_WIDTH - 1, :] = prev_ref[...]

    top = lax.broadcasted_iota(jnp.int32, (SUBLANES, MXU_COLS), 0) == 0

    def permuted_rows(r):
        return pl.ds((SUBLANES * r % ng) * SUBLANES + SUBLANES * r // ng, SUBLANES, stride=SUBLANES)

    def stages_of(i):
        base = i * sub

        def load_and_norm():
            for r in range(ng):
                rows = slice(base + r * SUBLANES, base + (r + 1) * SUBLANES)
                for c in range(n_slabs):
                    xp_s[i, c, permuted_rows(r), :] = x_ref[rows, c * LANES:(c + 1) * LANES]
            xp = jnp.concatenate([xp_s[i, c] for c in range(n_slabs)], axis=1)
            hn_s[i] = _rms(xp, gpre_ref[...]).astype(BF16)

        def up_conv_act(n):
            halves = []
            for off in (0, dff):
                cols = slice(off + n * MXU_COLS, off + (n + 1) * MXU_COLS)
                up = _dot(hn_s[i], wup_ref[:, cols])
                back1 = jnp.where(top, carry_s[1:2, cols], pltpu.roll(up[sub - SUBLANES:sub], 1, axis=0))
                back2 = jnp.where(top, carry_s[0:1, cols],
                                  pltpu.roll(up[sub - 2 * SUBLANES:sub - SUBLANES], 1, axis=0))
                carry_s[0:1, cols] = up[sub - SUBLANES - 1:sub - SUBLANES]
                carry_s[1:2, cols] = up[sub - 1:sub]
                prev1 = jnp.concatenate([back1, up[:sub - SUBLANES]], axis=0)
                prev2 = jnp.concatenate([back2, back1, up[:sub - 2 * SUBLANES]], axis=0)
                halves.append(bc_ref[:, cols] + wc_ref[0:1, cols] * prev2 + wc_ref[1:2, cols] * prev1
                              + wc_ref[2:3, cols] * up)
            act_s[i, :, n * MXU_COLS:(n + 1) * MXU_COLS] = (jax.nn.gelu(halves[0]) * halves[1]).astype(BF16)

        def down(n):
            cols = slice(n * MXU_COLS, (n + 1) * MXU_COLS)
            f_s[i, :, cols] = _dot(act_s[i], wdown_ref[:, cols])

        def norm_and_store():
            normed = _rms(f_s[i], gpost_ref[...])
            for c in range(n_slabs):
                xp_s[i, c] = xp_s[i, c] + normed[:, c * LANES:(c + 1) * LANES]
            for r in range(ng):
                rows = slice(base + r * SUBLANES, base + (r + 1) * SUBLANES)
                for c in range(n_slabs):
                    y_ref[rows, c * LANES:(c + 1) * LANES] = xp_s[i, c, permuted_rows(r), :]
            if i == n_sub - 1:
                newconv_ref[...] = carry_s[0:CONV_WIDTH - 1, :]

        return ([load_and_norm] + [functools.partial(up_conv_act, n) for n in range(dff // MXU_COLS)]
                + [functools.partial(down, n) for n in range(d // MXU_COLS)] + [norm_and_store])

    _skewed([stages_of(i) for i in range(n_sub)], SUB_BLOCK_SKEW)


def _ffn_call(x, conv_prev, gpre, wup, wc, bc, wdown, gpost, *, sub, n_sub):
    b, t, d = x.shape
    dff = wdown.shape[0]
    tm = sub * n_sub
    xspec = pl.BlockSpec((None, tm, d), lambda bi, ti: (bi, ti, 0))
    cspec = pl.BlockSpec((None, CONV_WIDTH - 1, 2 * dff), lambda bi, ti: (bi, 0, 0))
    weight_bytes = (wup.size + wdown.size) * 2
    block_bytes = 2 * 2 * tm * d * 4
    scratch_bytes = tm * d * 4 + tm * d * 2 + SUBLANES * 2 * dff * 4 + tm * dff * 2 + tm * d * 4
    return pl.pallas_call(
        functools.partial(_ffn_kernel, sub=sub, n_sub=n_sub),
        grid=(b, t // tm),
        in_specs=[xspec, cspec, _full(gpre.shape), _full(wup.shape), _full(wc.shape), _full(bc.shape),
                  _full(wdown.shape), _full(gpost.shape)],
        out_specs=[xspec, cspec],
        out_shape=[jax.ShapeDtypeStruct((b, t, d), F32),
                   jax.ShapeDtypeStruct((b, CONV_WIDTH - 1, 2 * dff), F32)],
        scratch_shapes=[pltpu.VMEM((n_sub, d // LANES, sub, LANES), F32),
                        pltpu.VMEM((n_sub, sub, d), BF16),
                        pltpu.VMEM((SUBLANES, 2 * dff), F32),
                        pltpu.VMEM((n_sub, sub, dff), BF16),
                        pltpu.VMEM((n_sub, sub, d), F32)],
        compiler_params=_params(2, weight_bytes + block_bytes + SCRATCH_AND_TEMPORARIES * scratch_bytes),
        name="ffn",
    )(x, conv_prev, gpre, wup, wc, bc, wdown, gpost)


def _layer(x, past_k, past_v, conv_prev, w):
    b, t, d = x.shape
    gm = GM_GROUPS * GM_GROUP_DIM
    sbw = SB_HEADS * SB_HEAD_DIM
    chunk = min(t, GM_CHUNK)
    n_tok = b * t
    decode = past_k is not None
    tm = min(SEQ_TILE, t)
    n_sub = min(SUB_BLOCKS_PER_STEP, t // tm)
    assert t % (tm * n_sub) == 0 and tm % chunk == 0, (t, tm, n_sub, chunk)

    tri = jnp.tril(jnp.ones((chunk, chunk), dtype=bool))
    ws = jnp.where(tri[None], w["w_s"][:, :chunk, :chunk], 0.0).astype(BF16)
    bias = jnp.repeat(w["b_s"][:, :chunk].T, GM_GROUP_DIM, axis=1)
    w_in = w["w_in"]
    w_uvq = w_in[:, :2 * gm + sbw]
    w_k = w_in[:, 2 * gm + sbw:2 * gm + 2 * sbw]
    w_v = w_in[:, 2 * gm + 2 * sbw:]
    if not decode:
        w_k, w_v = w_k.T, w_v.T

    flat_n_sub = min(SUB_BLOCKS_PER_STEP, n_tok // tm)
    flat_sub = max(tm, min(SEQ_TILE, n_tok // flat_n_sub))
    assert n_tok % (flat_sub * flat_n_sub) == 0 and flat_sub % chunk == 0, (n_tok, flat_sub, flat_n_sub)
    outs = _proj_call(
        x.reshape(n_tok, d), w["g_pre_mix"], w_uvq, w_k, w_v, w["ln_v_g"], w["ln_v_b"], ws, bias,
        w["w_gate"], w["b_gate"], sub=flat_sub if decode else tm, n_sub=flat_n_sub if decode else n_sub,
        chunk=chunk, seq_len=t, transposed_kv=not decode, emit_va=decode)
    ya, q, k_out, v_out, kb, vb, gates = outs[:7]
    q3 = q.reshape(b, t, sbw)

    if decode:
        p = past_k.shape[1]
        kt_past = jnp.transpose(past_k, (0, 2, 3, 1)).reshape(b, sbw, p)
        vt_past = jnp.transpose(past_v, (0, 2, 3, 1)).reshape(b, sbw, p)
        yb = _sb_sample_call(q3, kb.reshape(b, t, sbw), vb.reshape(b, t, sbw), kt_past, vt_past,
                             tk=min(p, SEQ_TILE))
        k4 = k_out.reshape(b, t, SB_HEADS, SB_HEAD_DIM)
        v4 = v_out.reshape(b, t, SB_HEADS, SB_HEAD_DIM)
        va = outs[7].reshape(b, t, gm)
    else:
        yb = _sb_prompt_call(q3, kb, vb, tq=tm, heads=SB_PROMPT_HEADS_PER_STEP)
        k4 = jnp.transpose(k_out.reshape(b, SB_HEADS, SB_HEAD_DIM, t), (0, 3, 1, 2))
        v4 = jnp.transpose(v_out.reshape(b, SB_HEADS, SB_HEAD_DIM, t), (0, 3, 1, 2))
        va = None

    wide = min(2 * tm, t // n_sub)
    flat_wide = max(wide, min(2 * flat_sub, n_tok // flat_n_sub))
    x1 = _merge_call(x.reshape(n_tok, d), ya, yb.reshape(n_tok, sbw), gates,
                     w["w_branch_a"], w["w_branch_b"], w["w_o"], w["g_post_mix"], sub=flat_wide, n_sub=flat_n_sub)
    y, conv_new = _ffn_call(x1.reshape(b, t, d), conv_prev, w["g_pre_ffn"], w["w_up"], w["w_conv"],
                            w["b_conv"], w["w_down"], w["g_post_ffn"], sub=wide, n_sub=n_sub)
    return y, k4, v4, conv_new, va


def kernel(x_prompt, x_sample, cache_sb_k, cache_sb_v, state_ffn_conv, g_pre_mix, w_in, ln_v_g, ln_v_b, w_s, b_s,
           w_gate, b_gate, w_branch_a, w_branch_b, w_o, g_post_mix, g_pre_ffn, w_up, w_conv, b_conv, w_down,
           g_post_ffn):
    depth = w_in.shape[0]
    y_p, y_s = x_prompt, x_sample
    outs = [[] for _ in range(7)]
    for l in range(depth):
        row = lambda a: a[l][None, :]
        w = dict(
            g_pre_mix=row(g_pre_mix), w_in=w_in[l].astype(BF16), ln_v_g=row(ln_v_g), ln_v_b=row(ln_v_b),
            w_s=w_s[l], b_s=b_s[l], w_gate=w_gate[l].astype(BF16), b_gate=row(b_gate),
            w_branch_a=w_branch_a[l].astype(BF16), w_branch_b=w_branch_b[l].astype(BF16),
            w_o=w_o[l].astype(BF16), g_post_mix=row(g_post_mix), g_pre_ffn=row(g_pre_ffn),
            w_up=w_up[l].astype(BF16), w_conv=w_conv[l], b_conv=row(b_conv),
            w_down=w_down[l].astype(BF16), g_post_ffn=row(g_post_ffn))
        zero_conv = jnp.zeros((y_p.shape[0], CONV_WIDTH - 1, w_up.shape[2]), y_p.dtype)
        y_p, kp, vp, cp, _ = _layer(y_p, None, None, zero_conv, w)
        y_s, ks, vs, cs, gvs = _layer(y_s, cache_sb_k[l], cache_sb_v[l], state_ffn_conv[l], w)
        for lst, a in zip(outs, (kp, vp, cp, ks, vs, cs, gvs)):
            lst.append(a)
    return (y_p, y_s) + tuple(jnp.stack(o) for o in outs)
```

```python
import functools
import math
from typing import Callable, NamedTuple, Optional

import jax
import jax.numpy as jnp
from jax import lax
from jax.experimental import pallas as pl
from jax.experimental.pallas import tpu as pltpu

F32 = jnp.float32
BF16 = jnp.bfloat16

EPS = 1e-6
SB_HEADS = 16
SB_HEAD_DIM = 64
GM_GROUPS = 8
GM_GROUP_DIM = 128
GM_CHUNK = 128
CONV_WIDTH = 3

LANES = 128
SUBLANES = 8
MXU_COLS = 256
HEADS_PER_LANE_TILE = LANES // SB_HEAD_DIM
HEADS_PER_MXU_TILE = MXU_COLS // SB_HEAD_DIM
SEQ_TILE = MXU_COLS
SUB_BLOCKS_PER_STEP = 2
SUB_BLOCK_SKEW = 1
SCRATCH_AND_TEMPORARIES = 2
SB_PROMPT_HEADS_PER_STEP = 16
SB_TILE_VALUES_IN_FLIGHT = 12
V7X_SCOPED_VMEM_BYTES = 60000 * 1024

LOG2E = math.log2(math.e)
F32_EXP2_UNDERFLOW = 150.0

_ARB = "arbitrary"
_NT = (((1,), (1,)), ((), ()))


def _rms(x, g):
    return x * lax.rsqrt(jnp.mean(x * x, axis=-1, keepdims=True) + EPS) * g


def _dot(a, b):
    return jnp.dot(a, b, preferred_element_type=F32)


def _dot_nt(a, b):
    return lax.dot_general(a, b, _NT, preferred_element_type=F32)


def _params(n_grid, vmem_bytes):
    return pltpu.CompilerParams(
        dimension_semantics=(_ARB,) * n_grid,
        vmem_limit_bytes=int(min(vmem_bytes, V7X_SCOPED_VMEM_BYTES)))


def _full(shape):
    nd = len(shape)
    return pl.BlockSpec(shape, lambda *_: (0,) * nd, pipeline_mode=pl.Buffered(1))


def _proj_kernel(x_ref, gpre_ref, win_ref, wk_ref, wv_ref, lng_ref, lnb_ref, ws_ref, bias_ref, wg_ref, bg_ref,
                 ya_ref, q_ref, k_ref, v_ref, kb_ref, vb_ref, gate_ref, *rest,
                 sub, n_sub, chunk, transposed_kv, emit_va):
    if emit_va:
        va_ref, hb_s, u_s, gv_s, vab_s = rest
    else:
        hb_s, u_s, gv_s, vab_s = rest
    gm = GM_GROUPS * GM_GROUP_DIM
    sb = SB_HEADS * SB_HEAD_DIM
    qscale = SB_HEAD_DIM ** -0.5 * LOG2E
    nch = sub // chunk

    def stages_of(i):
        base = i * sub
        tok = slice(base, base + sub)

        def norm():
            hb_s[i] = _rms(x_ref[tok, :], gpre_ref[...]).astype(BF16)

        def proj(col0, n):
            cols = slice(col0 + n * MXU_COLS, col0 + (n + 1) * MXU_COLS)
            return _dot(hb_s[i], win_ref[:, cols])

        def gating_inputs(n):
            oc = slice(n * MXU_COLS, (n + 1) * MXU_COLS)
            u_s[i, :, oc] = jax.nn.gelu(proj(0, n))
            gv_s[i, :, oc] = jax.nn.gelu(proj(gm, n))

        def layer_norm():
            gv = gv_s[i]
            mu = jnp.mean(gv, axis=-1, keepdims=True)
            dev = gv - mu
            var = jnp.mean(dev * dev, axis=-1, keepdims=True)
            va = dev * lax.rsqrt(var + EPS) * lng_ref[...] + lnb_ref[...]
            if emit_va:
                va_ref[tok, :] = va
            vab_s[i] = va.astype(BF16)

        def spatial_gating(g):
            cols = slice(g * GM_GROUP_DIM, (g + 1) * GM_GROUP_DIM)
            pieces = [vab_s[i, c * chunk:(c + 1) * chunk, cols] for c in range(nch)]
            mixed = _dot(ws_ref[g], pieces[0] if nch == 1 else jnp.concatenate(pieces, axis=1))
            for c in range(nch):
                rows = slice(c * chunk, (c + 1) * chunk)
                m = mixed[:, c * GM_GROUP_DIM:(c + 1) * GM_GROUP_DIM] + bias_ref[:, cols]
                ya_ref[base + c * chunk:base + (c + 1) * chunk, cols] = (u_s[i, rows, cols] * m).astype(BF16)

        def queries(n):
            q_ref[tok, n * MXU_COLS:(n + 1) * MXU_COLS] = (proj(2 * gm, n) * qscale).astype(BF16)

        def keys_or_values(w_ref, f32_out, bf16_out, n):
            oc = slice(n * MXU_COLS, (n + 1) * MXU_COLS)
            if transposed_kv:
                pt = _dot_nt(w_ref[oc, :], hb_s[i])
                f32_out[oc, tok] = pt
                bf16_out[oc, tok] = pt.astype(BF16)
            else:
                p = _dot(hb_s[i], w_ref[:, oc])
                bf16_out[tok, oc] = p.astype(BF16)
                for hs in range(HEADS_PER_MXU_TILE):
                    h = n * HEADS_PER_MXU_TILE + hs
                    f32_out[pl.ds(base * SB_HEADS + h, sub, stride=SB_HEADS), :] = (
                        p[:, hs * SB_HEAD_DIM:(hs + 1) * SB_HEAD_DIM])

        def gates(n):
            oc = slice(n * MXU_COLS, (n + 1) * MXU_COLS)
            gate_ref[tok, oc] = jax.nn.sigmoid(_dot(hb_s[i], wg_ref[:, oc]) + bg_ref[:, oc]).astype(BF16)

        part = functools.partial
        n_sb = sb // MXU_COLS
        return ([norm] + [part(gating_inputs, n) for n in range(gm // MXU_COLS)]
                + [part(queries, n) for n in range(n_sb)] + [layer_norm]
                + [part(keys_or_values, wk_ref, k_ref, kb_ref, n) for n in range(n_sb)]
                + [part(spatial_gating, g) for g in range(GM_GROUPS)]
                + [part(keys_or_values, wv_ref, v_ref, vb_ref, n) for n in range(n_sb)]
                + [part(gates, n) for n in range(gate_ref.shape[1] // MXU_COLS)])

    _skewed([stages_of(i) for i in range(n_sub)], SUB_BLOCK_SKEW)


def _proj_call(x2d, gpre, win, wk, wv, lng, lnb, ws, bias, wg, bg, *, sub, n_sub, chunk, seq_len, transposed_kv,
               emit_va):
    n_tok, d = x2d.shape
    gm = GM_GROUPS * GM_GROUP_DIM
    sb = SB_HEADS * SB_HEAD_DIM
    n_gate = wg.shape[1]
    tm = sub * n_sub

    def tok(width):
        return pl.BlockSpec((tm, width), lambda i: (i, 0))

    if transposed_kv:
        nt = seq_len // tm
        kv_spec = pl.BlockSpec((None, sb, tm), lambda i: (i // nt, 0, i % nt))
        kv_f32 = jax.ShapeDtypeStruct((n_tok // seq_len, sb, seq_len), F32)
        kv_bf16 = jax.ShapeDtypeStruct((n_tok // seq_len, sb, seq_len), BF16)
        kv_specs = [kv_spec] * 4
        kv_bytes = 2 * 2 * sb * tm * (4 + 2)
    else:
        hm_spec = pl.BlockSpec((tm * SB_HEADS, SB_HEAD_DIM), lambda i: (i, 0))
        kv_f32 = jax.ShapeDtypeStruct((n_tok * SB_HEADS, SB_HEAD_DIM), F32)
        kv_bf16 = jax.ShapeDtypeStruct((n_tok, sb), BF16)
        kv_specs = [hm_spec, hm_spec, tok(sb), tok(sb)]
        kv_bytes = 2 * 2 * tm * (SB_HEADS * LANES * 4 + sb * 2)
    out_specs = [tok(gm), tok(sb)] + kv_specs + [tok(n_gate)]
    out_shape = [jax.ShapeDtypeStruct((n_tok, gm), BF16), jax.ShapeDtypeStruct((n_tok, sb), BF16),
                 kv_f32, kv_f32, kv_bf16, kv_bf16, jax.ShapeDtypeStruct((n_tok, n_gate), BF16)]
    if emit_va:
        out_specs.append(tok(gm))
        out_shape.append(jax.ShapeDtypeStruct((n_tok, gm), F32))

    weight_bytes = (win.size + wk.size + wv.size + wg.size + ws.size) * 2
    block_bytes = 2 * tm * (d * 4 + gm * 2 + sb * 2 + n_gate * 2 + gm * 4) + kv_bytes
    scratch_bytes = tm * (d * 2 + gm * 4 + gm * 4 + gm * 2)
    return pl.pallas_call(
        functools.partial(_proj_kernel, sub=sub, n_sub=n_sub, chunk=chunk, transposed_kv=transposed_kv,
                          emit_va=emit_va),
        grid=(n_tok // tm,),
        in_specs=[tok(d), _full(gpre.shape), _full(win.shape), _full(wk.shape), _full(wv.shape),
                  _full(lng.shape), _full(lnb.shape), _full(ws.shape), _full(bias.shape), _full(wg.shape),
                  _full(bg.shape)],
        out_specs=out_specs,
        out_shape=out_shape,
        scratch_shapes=[pltpu.VMEM((n_sub, sub, d), BF16), pltpu.VMEM((n_sub, sub, gm), F32),
                        pltpu.VMEM((n_sub, sub, gm), F32), pltpu.VMEM((n_sub, sub, gm), BF16)],
        compiler_params=_params(1, weight_bytes + block_bytes + SCRATCH_AND_TEMPORARIES * scratch_bytes),
        name="proj",
    )(x2d, gpre, win, wk, wv, lng, lnb, ws, bias, wg, bg)


class _Tile(NamedTuple):
    head: int
    scores: Callable
    values: Callable
    v_transposed: bool
    tri: jax.Array
    mask: Optional[jax.Array]
    first: bool


def _sb_tiles(tiles, c_ref, cmin_ref, acc_ref):
    n = len(tiles)
    z, sp, cum, weight = ([None] * n for _ in range(4))

    def scores(i):
        z[i] = tiles[i].scores()

    def softplus(i):
        s = jnp.maximum(z[i], 0.0) + jnp.log2(1.0 + jnp.exp2(-jnp.abs(z[i])))
        sp[i] = s if tiles[i].mask is None else jnp.where(tiles[i].mask, s, 0.0)

    def cumsum(i):
        cum[i] = _dot(sp[i].astype(BF16), tiles[i].tri)

    def weights(i):
        h = tiles[i].head
        total = sp[i] + cum[i]
        if not tiles[i].first:
            total = total + c_ref[h]
        a = jnp.exp2(z[i] - total)
        if tiles[i].mask is not None:
            a = jnp.where(tiles[i].mask, a, 0.0)
        weight[i] = a.astype(BF16)
        c_ref[h] = total[:, 0:1]
        tq, width = total.shape[0], min(LANES, total.shape[1])
        cmin_ref[h, :, 0:width] = jnp.min(total[:, 0:width].reshape(tq // SUBLANES, SUBLANES, width), axis=0)
        z[i] = sp[i] = cum[i] = None

    def values(i):
        v = tiles[i].values()
        contrib = _dot_nt(weight[i], v) if tiles[i].v_transposed else _dot(weight[i], v)
        h = tiles[i].head
        lanes = _pair_lanes(h)
        if tiles[i].first and h % HEADS_PER_LANE_TILE == 0:
            acc_ref[:, lanes] = contrib
        else:
            acc_ref[:, lanes] = acc_ref[:, lanes] + contrib
        weight[i] = None

    stages = (scores, softplus, cumsum, weights, values)
    for step in range(n + len(stages) - 1):
        for s, stage in enumerate(stages):
            if 0 <= step - s < n:
                stage(step - s)


def _pair_lanes(h):
    return slice(h // HEADS_PER_LANE_TILE * LANES, (h // HEADS_PER_LANE_TILE + 1) * LANES)


def _strict_tri(n):
    r = lax.broadcasted_iota(jnp.int32, (n, n), 0)
    s = lax.broadcasted_iota(jnp.int32, (n, n), 1)
    return jnp.where(r > s, 1.0, 0.0).astype(BF16)


def _causal_mask(n):
    r = lax.broadcasted_iota(jnp.int32, (n, n), 0)
    s = lax.broadcasted_iota(jnp.int32, (n, n), 1)
    return s < r


def _head_lane_masks():
    lane = lax.broadcasted_iota(jnp.int32, (1, LANES), 1)
    return [(lane >= hh * SB_HEAD_DIM) & (lane < (hh + 1) * SB_HEAD_DIM) for hh in range(HEADS_PER_LANE_TILE)]


def _head_row_masks():
    row = lax.broadcasted_iota(jnp.int32, (LANES, 1), 0)
    return [(row >= hh * SB_HEAD_DIM) & (row < (hh + 1) * SB_HEAD_DIM) for hh in range(HEADS_PER_LANE_TILE)]


def _masked_queries(q_ref, rows, qm_s, heads):
    lane_masks = _head_lane_masks()
    for h in range(heads):
        lt = h // HEADS_PER_LANE_TILE
        qt = q_ref[rows, lt * LANES:(lt + 1) * LANES]
        qm_s[h] = jnp.where(lane_masks[h % HEADS_PER_LANE_TILE], qt, jnp.zeros_like(qt))


def _sweep_is_live(cmin_ref):
    return (jnp.min(cmin_ref[:, :, 0:1]) < F32_EXP2_UNDERFLOW).astype(jnp.int32)


def _sb_prompt_kernel(q_ref, kt_ref, vt_ref, o_ref, acc_s, c_s, cmin_s, u_s, qm_s, *, tq, heads):
    n_blocks = q_ref.shape[0] // tq
    u_s[...] = _strict_tri(tq)
    row_masks = _head_row_masks()

    def tiles(j, diagonal):
        keys = pl.ds(pl.multiple_of(j * tq, tq), tq)
        mask = _causal_mask(tq) if diagonal else None

        def scores(h):
            return _dot(qm_s[h], kt_ref[_pair_lanes(h), keys])

        def values(h):
            vt = vt_ref[_pair_lanes(h), keys]
            return jnp.where(row_masks[h % HEADS_PER_LANE_TILE], vt, jnp.zeros_like(vt))

        return [_Tile(h, functools.partial(scores, h), functools.partial(values, h), True, u_s[...], mask,
                      first=diagonal) for h in range(heads)]

    def query_block(i, has_previous):
        rows = pl.ds(pl.multiple_of(i * tq, tq), tq)
        _masked_queries(q_ref, rows, qm_s, heads)
        if has_previous:
            _sb_tiles(tiles(i, True) + tiles(i - 1, False), c_s, cmin_s, acc_s)

            def cond(carry):
                n, go = carry
                return jnp.logical_and(n < i - 1, go > 0)

            def body(carry):
                n, _ = carry
                _sb_tiles(tiles(i - 2 - n, False), c_s, cmin_s, acc_s)
                return n + 1, _sweep_is_live(cmin_s)

            lax.while_loop(cond, body, (jnp.int32(0), _sweep_is_live(cmin_s)))
        else:
            _sb_tiles(tiles(i, True), c_s, cmin_s, acc_s)
        o_ref[rows, :] = acc_s[...].astype(BF16)

    query_block(0, False)

    def later_blocks(i, carry):
        query_block(i, True)
        return carry

    lax.fori_loop(1, n_blocks, later_blocks, 0)


def _sb_prompt_call(q, kt, vt, *, tq, heads):
    b, t, w = q.shape
    width = heads * SB_HEAD_DIM
    qspec = pl.BlockSpec((None, t, width), lambda bi, hg: (bi, 0, hg))
    kvspec = pl.BlockSpec((None, width, t), lambda bi, hg: (bi, hg, 0))
    vmem = (2 * 4 * t * width * 2 + tq * width * 4 + heads * tq * LANES * 4
            + tq * tq * 2 + heads * tq * LANES * 2 + heads * SB_TILE_VALUES_IN_FLIGHT * tq * tq * 4)
    return pl.pallas_call(
        functools.partial(_sb_prompt_kernel, tq=tq, heads=heads),
        grid=(b, w // width),
        in_specs=[qspec, kvspec, kvspec],
        out_specs=qspec,
        out_shape=jax.ShapeDtypeStruct((b, t, w), BF16),
        scratch_shapes=[pltpu.VMEM((tq, width), F32),
                        pltpu.VMEM((heads, tq, 1), F32),
                        pltpu.VMEM((heads, SUBLANES, LANES), F32),
                        pltpu.VMEM((tq, tq), BF16),
                        pltpu.VMEM((heads, tq, LANES), BF16)],
        compiler_params=_params(2, vmem),
        name="sb_prompt",
    )(q, kt, vt)


def _sb_sample_kernel(q_ref, kn_ref, vn_ref, kp_hbm, vp_hbm, o_ref, kbuf, vbuf, sem, acc_s, c_s, cmin_s, u_s, ud_s,
                      qm_s, *, tq, tk, n_past):
    b = pl.program_id(0)

    def copies(j, slot):
        keys = pl.ds(pl.multiple_of(j * tk, tk), tk)
        return (pltpu.make_async_copy(kp_hbm.at[b, :, keys], kbuf.at[slot], sem.at[0, slot]),
                pltpu.make_async_copy(vp_hbm.at[b, :, keys], vbuf.at[slot], sem.at[1, slot]))

    def start(j, slot):
        for cp in copies(j, slot):
            cp.start()

    def wait(j, slot):
        for cp in copies(j, slot):
            cp.wait()

    start(n_past - 1, 0)
    u_s[...] = _strict_tri(tk)
    ud_s[...] = _strict_tri(tq)
    _masked_queries(q_ref, slice(None), qm_s, SB_HEADS)
    lane_masks = _head_lane_masks()
    row_masks = _head_row_masks()

    def z_new(h):
        return _dot_nt(qm_s[h], kn_ref[:, _pair_lanes(h)])

    def v_new(h):
        vn = vn_ref[:, _pair_lanes(h)]
        return jnp.where(lane_masks[h % HEADS_PER_LANE_TILE], vn, jnp.zeros_like(vn))

    mask = _causal_mask(tq)
    _sb_tiles([_Tile(h, functools.partial(z_new, h), functools.partial(v_new, h), False, ud_s[...], mask,
                     first=True) for h in range(SB_HEADS)], c_s, cmin_s, acc_s)

    def cond(carry):
        n, go = carry
        return jnp.logical_and(n < n_past, go > 0)

    def body(carry):
        n, _ = carry
        j = n_past - 1 - n
        slot = lax.rem(n, 2)
        wait(j, slot)

        @pl.when(n + 1 < n_past)
        def _():
            start(j - 1, 1 - slot)

        def z_past(h):
            return _dot(qm_s[h], kbuf[slot, _pair_lanes(h), :].astype(BF16))

        def v_past(h):
            vt = vbuf[slot, _pair_lanes(h), :].astype(BF16)
            return jnp.where(row_masks[h % HEADS_PER_LANE_TILE], vt, jnp.zeros_like(vt))

        _sb_tiles([_Tile(h, functools.partial(z_past, h), functools.partial(v_past, h), True, u_s[...], None,
                         first=False) for h in range(SB_HEADS)], c_s, cmin_s, acc_s)
        return n + 1, _sweep_is_live(cmin_s)

    n_done, _ = lax.while_loop(cond, body, (jnp.int32(0), _sweep_is_live(cmin_s)))

    @pl.when(n_done < n_past)
    def _():
        wait(n_past - 1 - n_done, lax.rem(n_done, 2))

    o_ref[...] = acc_s[...].astype(BF16)


def _sb_sample_call(q, k_new, v_new, kt_past, vt_past, *, tk):
    b, tq, w = q.shape
    p = kt_past.shape[2]
    newspec = pl.BlockSpec((None, tq, w), lambda bi: (bi, 0, 0))
    anyspec = pl.BlockSpec(memory_space=pl.ANY)
    vmem = (2 * 2 * w * tk * 4 + 2 * 4 * tq * w * 2 + tq * w * 4 + SB_HEADS * tq * LANES * 4
            + 2 * tk * tk * 2 + SB_HEADS * tq * LANES * 2 + SB_HEADS * SB_TILE_VALUES_IN_FLIGHT * tq * tk * 4)
    return pl.pallas_call(
        functools.partial(_sb_sample_kernel, tq=tq, tk=tk, n_past=p // tk),
        grid=(b,),
        in_specs=[newspec, newspec, newspec, anyspec, anyspec],
        out_specs=newspec,
        out_shape=jax.ShapeDtypeStruct((b, tq, w), BF16),
        scratch_shapes=[pltpu.VMEM((2, w, tk), F32),
                        pltpu.VMEM((2, w, tk), F32),
                        pltpu.SemaphoreType.DMA((2, 2)),
                        pltpu.VMEM((tq, w), F32),
                        pltpu.VMEM((SB_HEADS, tq, 1), F32),
                        pltpu.VMEM((SB_HEADS, SUBLANES, LANES), F32),
                        pltpu.VMEM((tk, tk), BF16),
                        pltpu.VMEM((tq, tq), BF16),
                        pltpu.VMEM((SB_HEADS, tq, LANES), BF16)],
        compiler_params=_params(1, vmem),
        name="sb_sample",
    )(q, k_new, v_new, kt_past, vt_past)


def _merge_kernel(x_ref, ya_ref, yb_ref, gate_ref, wa_ref, wb_ref, wo_ref, gpost_ref, o_ref, m_s, p_s,
                  *, sub, n_sub):
    d = x_ref.shape[1]

    def stages_of(i):
        tok = slice(i * sub, (i + 1) * sub)

        def merge(n):
            cols = slice(n * MXU_COLS, (n + 1) * MXU_COLS)
            gcols = slice(d + n * MXU_COLS, d + (n + 1) * MXU_COLS)
            merged = (gate_ref[tok, cols].astype(F32) * _dot(ya_ref[tok, :], wa_ref[:, cols])
                      + gate_ref[tok, gcols].astype(F32) * _dot(yb_ref[tok, :], wb_ref[:, cols]))
            m_s[i, :, cols] = merged.astype(BF16)

        def project(n):
            cols = slice(n * MXU_COLS, (n + 1) * MXU_COLS)
            p_s[i, :, cols] = _dot(m_s[i], wo_ref[:, cols])

        def norm_and_store():
            o_ref[tok, :] = x_ref[tok, :] + _rms(p_s[i], gpost_ref[...])

        n_chunks = d // MXU_COLS
        return ([functools.partial(merge, n) for n in range(n_chunks)]
                + [functools.partial(project, n) for n in range(n_chunks)] + [norm_and_store])

    _skewed([stages_of(i) for i in range(n_sub)], SUB_BLOCK_SKEW)


def _merge_call(x2d, ya, yb, gates, wa, wb, wo, gpost, *, sub, n_sub):
    n_tok, d = x2d.shape
    tm = sub * n_sub

    def tok(width):
        return pl.BlockSpec((tm, width), lambda i: (i, 0))

    weight_bytes = (wa.size + wb.size + wo.size) * 2
    block_bytes = 2 * tm * (d * 4 + ya.shape[1] * 2 + yb.shape[1] * 2 + gates.shape[1] * 2 + d * 4)
    scratch_bytes = tm * d * 6
    return pl.pallas_call(
        functools.partial(_merge_kernel, sub=sub, n_sub=n_sub),
        grid=(n_tok // tm,),
        in_specs=[tok(d), tok(ya.shape[1]), tok(yb.shape[1]), tok(gates.shape[1]),
                  _full(wa.shape), _full(wb.shape), _full(wo.shape), _full(gpost.shape)],
        out_specs=tok(d),
        out_shape=jax.ShapeDtypeStruct((n_tok, d), F32),
        scratch_shapes=[pltpu.VMEM((n_sub, sub, d), BF16), pltpu.VMEM((n_sub, sub, d), F32)],
        compiler_params=_params(1, weight_bytes + block_bytes + SCRATCH_AND_TEMPORARIES * scratch_bytes),
        name="merge",
    )(x2d, ya, yb, gates, wa, wb, wo, gpost)


def _skewed(stage_lists, skew):
    n_stages = len(stage_lists[0])
    for step in range(n_stages + skew * (len(stage_lists) - 1)):
        for i, stages in enumerate(stage_lists):
            if 0 <= step - i * skew < n_stages:
                stages[step - i * skew]()


def _ffn_kernel(x_ref, prev_ref, gpre_ref, wup_ref, wc_ref, bc_ref, wdown_ref, gpost_ref,
                y_ref, newconv_ref, xp_s, hn_s, carry_s, act_s, f_s, *, sub, n_sub):
    assert CONV_WIDTH == 3
    dff = wdown_ref.shape[0]
    d = x_ref.shape[1]
    ng = sub // SUBLANES
    assert ng >= CONV_WIDTH - 1 and ng % SUBLANES == 0
    n_slabs = d // LANES
    t = pl.program_id(1)

    @pl.when(t == 0)
    def _():
        carry_s[0:CONV_WIDTH - 1, :] = prev_ref[...]

    top = lax.broadcasted_iota(jnp.int32, (SUBLANES, MXU_COLS), 0) == 0

    def permuted_rows(r):
        return pl.ds((SUBLANES * r % ng) * SUBLANES + SUBLANES * r // ng, SUBLANES, stride=SUBLANES)

    def stages_of(i):
        base = i * sub

        def load_and_norm():
            for r in range(ng):
                rows = slice(base + r * SUBLANES, base + (r + 1) * SUBLANES)
                for c in range(n_slabs):
                    xp_s[i, c, permuted_rows(r), :] = x_ref[rows, c * LANES:(c + 1) * LANES]
            xp = jnp.concatenate([xp_s[i, c] for c in range(n_slabs)], axis=1)
            hn_s[i] = _rms(xp, gpre_ref[...]).astype(BF16)

        def up_conv_act(n):
            halves = []
            for off in (0, dff):
                cols = slice(off + n * MXU_COLS, off + (n + 1) * MXU_COLS)
                up = _dot(hn_s[i], wup_ref[:, cols])
                back1 = jnp.where(top, carry_s[1:2, cols], pltpu.roll(up[sub - SUBLANES:sub], 1, axis=0))
                back2 = jnp.where(top, carry_s[0:1, cols],
                                  pltpu.roll(up[sub - 2 * SUBLANES:sub - SUBLANES], 1, axis=0))
                carry_s[0:1, cols] = up[sub - SUBLANES - 1:sub - SUBLANES]
                carry_s[1:2, cols] = up[sub - 1:sub]
                prev1 = jnp.concatenate([back1, up[:sub - SUBLANES]], axis=0)
                prev2 = jnp.concatenate([back2, back1, up[:sub - 2 * SUBLANES]], axis=0)
                halves.append(bc_ref[:, cols] + wc_ref[0:1, cols] * prev2 + wc_ref[1:2, cols] * prev1
                              + wc_ref[2:3, cols] * up)
            act_s[i, :, n * MXU_COLS:(n + 1) * MXU_COLS] = (jax.nn.gelu(halves[0]) * halves[1]).astype(BF16)

        def down(n):
            cols = slice(n * MXU_COLS, (n + 1) * MXU_COLS)
            f_s[i, :, cols] = _dot(act_s[i], wdown_ref[:, cols])

        def norm_and_store():
            normed = _rms(f_s[i], gpost_ref[...])
            for c in range(n_slabs):
                xp_s[i, c] = xp_s[i, c] + normed[:, c * LANES:(c + 1) * LANES]
            for r in range(ng):
                rows = slice(base + r * SUBLANES, base + (r + 1) * SUBLANES)
                for c in range(n_slabs):
                    y_ref[rows, c * LANES:(c + 1) * LANES] = xp_s[i, c, permuted_rows(r), :]
            if i == n_sub - 1:
                newconv_ref[...] = carry_s[0:CONV_WIDTH - 1, :]

        return ([load_and_norm] + [functools.partial(up_conv_act, n) for n in range(dff // MXU_COLS)]
                + [functools.partial(down, n) for n in range(d // MXU_COLS)] + [norm_and_store])

    _skewed([stages_of(i) for i in range(n_sub)], SUB_BLOCK_SKEW)


def _ffn_call(x, conv_prev, gpre, wup, wc, bc, wdown, gpost, *, sub, n_sub):
    b, t, d = x.shape
    dff = wdown.shape[0]
    tm = sub * n_sub
    xspec = pl.BlockSpec((None, tm, d), lambda bi, ti: (bi, ti, 0))
    cspec = pl.BlockSpec((None, CONV_WIDTH - 1, 2 * dff), lambda bi, ti: (bi, 0, 0))
    weight_bytes = (wup.size + wdown.size) * 2
    block_bytes = 2 * 2 * tm * d * 4
    scratch_bytes = tm * d * 4 + tm * d * 2 + SUBLANES * 2 * dff * 4 + tm * dff * 2 + tm * d * 4
    return pl.pallas_call(
        functools.partial(_ffn_kernel, sub=sub, n_sub=n_sub),
        grid=(b, t // tm),
        in_specs=[xspec, cspec, _full(gpre.shape), _full(wup.shape), _full(wc.shape), _full(bc.shape),
                  _full(wdown.shape), _full(gpost.shape)],
        out_specs=[xspec, cspec],
        out_shape=[jax.ShapeDtypeStruct((b, t, d), F32),
                   jax.ShapeDtypeStruct((b, CONV_WIDTH - 1, 2 * dff), F32)],
        scratch_shapes=[pltpu.VMEM((n_sub, d // LANES, sub, LANES), F32),
                        pltpu.VMEM((n_sub, sub, d), BF16),
                        pltpu.VMEM((SUBLANES, 2 * dff), F32),
                        pltpu.VMEM((n_sub, sub, dff), BF16),
                        pltpu.VMEM((n_sub, sub, d), F32)],
        compiler_params=_params(2, weight_bytes + block_bytes + SCRATCH_AND_TEMPORARIES * scratch_bytes),
        name="ffn",
    )(x, conv_prev, gpre, wup, wc, bc, wdown, gpost)


def _layer(x, past_k, past_v, conv_prev, w):
    b, t, d = x.shape
    gm = GM_GROUPS * GM_GROUP_DIM
    sbw = SB_HEADS * SB_HEAD_DIM
    chunk = min(t, GM_CHUNK)
    n_tok = b * t
    decode = past_k is not None
    tm = min(SEQ_TILE, t)
    n_sub = min(SUB_BLOCKS_PER_STEP, t // tm)
    assert t % (tm * n_sub) == 0 and tm % chunk == 0, (t, tm, n_sub, chunk)

    tri = jnp.tril(jnp.ones((chunk, chunk), dtype=bool))
    ws = jnp.where(tri[None], w["w_s"][:, :chunk, :chunk], 0.0).astype(BF16)
    bias = jnp.repeat(w["b_s"][:, :chunk].T, GM_GROUP_DIM, axis=1)
    w_in = w["w_in"]
    w_uvq = w_in[:, :2 * gm + sbw]
    w_k = w_in[:, 2 * gm + sbw:2 * gm + 2 * sbw]
    w_v = w_in[:, 2 * gm + 2 * sbw:]
    if not decode:
        w_k, w_v = w_k.T, w_v.T

    flat_n_sub = min(SUB_BLOCKS_PER_STEP, n_tok // tm)
    flat_sub = max(tm, min(SEQ_TILE, n_tok // flat_n_sub))
    assert n_tok % (flat_sub * flat_n_sub) == 0 and flat_sub % chunk == 0, (n_tok, flat_sub, flat_n_sub)
    outs = _proj_call(
        x.reshape(n_tok, d), w["g_pre_mix"], w_uvq, w_k, w_v, w["ln_v_g"], w["ln_v_b"], ws, bias,
        w["w_gate"], w["b_gate"], sub=flat_sub if decode else tm, n_sub=flat_n_sub if decode else n_sub,
        chunk=chunk, seq_len=t, transposed_kv=not decode, emit_va=decode)
    ya, q, k_out, v_out, kb, vb, gates = outs[:7]
    q3 = q.reshape(b, t, sbw)

    if decode:
        p = past_k.shape[1]
        kt_past = jnp.transpose(past_k, (0, 2, 3, 1)).reshape(b, sbw, p)
        vt_past = jnp.transpose(past_v, (0, 2, 3, 1)).reshape(b, sbw, p)
        yb = _sb_sample_call(q3, kb.reshape(b, t, sbw), vb.reshape(b, t, sbw), kt_past, vt_past,
                             tk=min(p, SEQ_TILE))
        k4 = k_out.reshape(b, t, SB_HEADS, SB_HEAD_DIM)
        v4 = v_out.reshape(b, t, SB_HEADS, SB_HEAD_DIM)
        va = outs[7].reshape(b, t, gm)
    else:
        yb = _sb_prompt_call(q3, kb, vb, tq=tm, heads=SB_PROMPT_HEADS_PER_STEP)
        k4 = jnp.transpose(k_out.reshape(b, SB_HEADS, SB_HEAD_DIM, t), (0, 3, 1, 2))
        v4 = jnp.transpose(v_out.reshape(b, SB_HEADS, SB_HEAD_DIM, t), (0, 3, 1, 2))
        va = None

    wide = min(2 * tm, t // n_sub)
    flat_wide = max(wide, min(2 * flat_sub, n_tok // flat_n_sub))
    x1 = _merge_call(x.reshape(n_tok, d), ya, yb.reshape(n_tok, sbw), gates,
                     w["w_branch_a"], w["w_branch_b"], w["w_o"], w["g_post_mix"], sub=flat_wide, n_sub=flat_n_sub)
    y, conv_new = _ffn_call(x1.reshape(b, t, d), conv_prev, w["g_pre_ffn"], w["w_up"], w["w_conv"],
                            w["b_conv"], w["w_down"], w["g_post_ffn"], sub=wide, n_sub=n_sub)
    return y, k4, v4, conv_new, va


def kernel(x_prompt, x_sample, cache_sb_k, cache_sb_v, state_ffn_conv, g_pre_mix, w_in, ln_v_g, ln_v_b, w_s, b_s,
           w_gate, b_gate, w_branch_a, w_branch_b, w_o, g_post_mix, g_pre_ffn, w_up, w_conv, b_conv, w_down,
           g_post_ffn):
    depth = w_in.shape[0]
    y_p, y_s = x_prompt, x_sample
    outs = [[] for _ in range(7)]
    for l in range(depth):
        row = lambda a: a[l][None, :]
        w = dict(
            g_pre_mix=row(g_pre_mix), w_in=w_in[l].astype(BF16), ln_v_g=row(ln_v_g), ln_v_b=row(ln_v_b),
            w_s=w_s[l], b_s=b_s[l], w_gate=w_gate[l].astype(BF16), b_gate=row(b_gate),
            w_branch_a=w_branch_a[l].astype(BF16), w_branch_b=w_branch_b[l].astype(BF16),
            w_o=w_o[l].astype(BF16), g_post_mix=row(g_post_mix), g_pre_ffn=row(g_pre_ffn),
            w_up=w_up[l].astype(BF16), w_conv=w_conv[l], b_conv=row(b_conv),
            w_down=w_down[l].astype(BF16), g_post_ffn=row(g_post_ffn))
        zero_conv = jnp.zeros((y_p.shape[0], CONV_WIDTH - 1, w_up.shape[2]), y_p.dtype)
        y_p, kp, vp, cp, _ = _layer(y_p, None, None, zero_conv, w)
        y_s, ks, vs, cs, gvs = _layer(y_s, cache_sb_k[l], cache_sb_v[l], state_ffn_conv[l], w)
        for lst, a in zip(outs, (kp, vp, cp, ks, vs, cs, gvs)):
            lst.append(a)
    return (y_p, y_s) + tuple(jnp.stack(o) for o in outs)
```

```python
import functools
import math
from typing import Callable, NamedTuple, Optional

import jax
import jax.numpy as jnp
from jax import lax
from jax.experimental import pallas as pl
from jax.experimental.pallas import tpu as pltpu

F32 = jnp.float32
BF16 = jnp.bfloat16

EPS = 1e-6
SB_HEADS = 16
SB_HEAD_DIM = 64
GM_GROUPS = 8
GM_GROUP_DIM = 128
GM_CHUNK = 128
CONV_WIDTH = 3

LANES = 128
SUBLANES = 8
MXU_COLS = 256
HEADS_PER_LANE_TILE = LANES // SB_HEAD_DIM
HEADS_PER_MXU_TILE = MXU_COLS // SB_HEAD_DIM
SEQ_TILE = MXU_COLS
SUB_BLOCKS_PER_STEP = 2
SUB_BLOCK_SKEW = 1
SCRATCH_AND_TEMPORARIES = 2
SB_PROMPT_HEADS_PER_STEP = 16
SB_TILE_VALUES_IN_FLIGHT = 12
V7X_SCOPED_VMEM_BYTES = 60000 * 1024

LOG2E = math.log2(math.e)
F32_EXP2_UNDERFLOW = 150.0
SOFTPLUS2_LINEAR_FROM = 64.0

_ARB = "arbitrary"
_NT = (((1,), (1,)), ((), ()))


def _rms(x, g):
    return x * lax.rsqrt(jnp.mean(x * x, axis=-1, keepdims=True) + EPS) * g


def _dot(a, b):
    return jnp.dot(a, b, preferred_element_type=F32)


def _dot_nt(a, b):
    return lax.dot_general(a, b, _NT, preferred_element_type=F32)


def _params(n_grid, vmem_bytes):
    return pltpu.CompilerParams(
        dimension_semantics=(_ARB,) * n_grid,
        vmem_limit_bytes=int(min(vmem_bytes, V7X_SCOPED_VMEM_BYTES)))


def _full(shape):
    nd = len(shape)
    return pl.BlockSpec(shape, lambda *_: (0,) * nd, pipeline_mode=pl.Buffered(1))


def _proj_kernel(x_ref, gpre_ref, win_ref, wk_ref, wv_ref, lng_ref, lnb_ref, ws_ref, bias_ref, wg_ref, bg_ref,
                 ya_ref, q_ref, k_ref, v_ref, kb_ref, vb_ref, gate_ref, *rest,
                 sub, n_sub, chunk, transposed_kv, emit_va):
    if emit_va:
        va_ref, hb_s, u_s, gv_s, vab_s = rest
    else:
        hb_s, u_s, gv_s, vab_s = rest
    gm = GM_GROUPS * GM_GROUP_DIM
    sb = SB_HEADS * SB_HEAD_DIM
    qscale = SB_HEAD_DIM ** -0.5 * LOG2E
    nch = sub // chunk

    def stages_of(i):
        base = i * sub
        tok = slice(base, base + sub)

        def norm():
            hb_s[i] = _rms(x_ref[tok, :], gpre_ref[...]).astype(BF16)

        def proj(col0, n):
            cols = slice(col0 + n * MXU_COLS, col0 + (n + 1) * MXU_COLS)
            return _dot(hb_s[i], win_ref[:, cols])

        def gating_inputs(n):
            oc = slice(n * MXU_COLS, (n + 1) * MXU_COLS)
            u_s[i, :, oc] = jax.nn.gelu(proj(0, n))
            gv_s[i, :, oc] = jax.nn.gelu(proj(gm, n))

        def layer_norm():
            gv = gv_s[i]
            mu = jnp.mean(gv, axis=-1, keepdims=True)
            dev = gv - mu
            var = jnp.mean(dev * dev, axis=-1, keepdims=True)
            va = dev * lax.rsqrt(var + EPS) * lng_ref[...] + lnb_ref[...]
            if emit_va:
                va_ref[tok, :] = va
            vab_s[i] = va.astype(BF16)

        def spatial_gating(g):
            cols = slice(g * GM_GROUP_DIM, (g + 1) * GM_GROUP_DIM)
            pieces = [vab_s[i, c * chunk:(c + 1) * chunk, cols] for c in range(nch)]
            mixed = _dot(ws_ref[g], pieces[0] if nch == 1 else jnp.concatenate(pieces, axis=1))
            for c in range(nch):
                rows = slice(c * chunk, (c + 1) * chunk)
                m = mixed[:, c * GM_GROUP_DIM:(c + 1) * GM_GROUP_DIM] + bias_ref[:, cols]
                ya_ref[base + c * chunk:base + (c + 1) * chunk, cols] = (u_s[i, rows, cols] * m).astype(BF16)

        def queries(n):
            q_ref[tok, n * MXU_COLS:(n + 1) * MXU_COLS] = (proj(2 * gm, n) * qscale).astype(BF16)

        def keys_or_values(w_ref, f32_out, bf16_out, n):
            oc = slice(n * MXU_COLS, (n + 1) * MXU_COLS)
            if transposed_kv:
                pt = _dot_nt(w_ref[oc, :], hb_s[i])
                f32_out[oc, tok] = pt
                bf16_out[oc, tok] = pt.astype(BF16)
            else:
                p = _dot(hb_s[i], w_ref[:, oc])
                bf16_out[tok, oc] = p.astype(BF16)
                for hs in range(HEADS_PER_MXU_TILE):
                    h = n * HEADS_PER_MXU_TILE + hs
                    f32_out[pl.ds(base * SB_HEADS + h, sub, stride=SB_HEADS), :] = (
                        p[:, hs * SB_HEAD_DIM:(hs + 1) * SB_HEAD_DIM])

        def gates(n):
            oc = slice(n * MXU_COLS, (n + 1) * MXU_COLS)
            gate_ref[tok, oc] = jax.nn.sigmoid(_dot(hb_s[i], wg_ref[:, oc]) + bg_ref[:, oc]).astype(BF16)

        part = functools.partial
        n_sb = sb // MXU_COLS
        return ([norm] + [part(gating_inputs, n) for n in range(gm // MXU_COLS)]
                + [part(queries, n) for n in range(n_sb)] + [layer_norm]
                + [part(keys_or_values, wk_ref, k_ref, kb_ref, n) for n in range(n_sb)]
                + [part(spatial_gating, g) for g in range(GM_GROUPS)]
                + [part(keys_or_values, wv_ref, v_ref, vb_ref, n) for n in range(n_sb)]
                + [part(gates, n) for n in range(gate_ref.shape[1] // MXU_COLS)])

    _skewed([stages_of(i) for i in range(n_sub)], SUB_BLOCK_SKEW)


def _proj_call(x2d, gpre, win, wk, wv, lng, lnb, ws, bias, wg, bg, *, sub, n_sub, chunk, seq_len, transposed_kv,
               emit_va):
    n_tok, d = x2d.shape
    gm = GM_GROUPS * GM_GROUP_DIM
    sb = SB_HEADS * SB_HEAD_DIM
    n_gate = wg.shape[1]
    tm = sub * n_sub

    def tok(width):
        return pl.BlockSpec((tm, width), lambda i: (i, 0))

    if transposed_kv:
        nt = seq_len // tm
        kv_spec = pl.BlockSpec((None, sb, tm), lambda i: (i // nt, 0, i % nt))
        kv_f32 = jax.ShapeDtypeStruct((n_tok // seq_len, sb, seq_len), F32)
        kv_bf16 = jax.ShapeDtypeStruct((n_tok // seq_len, sb, seq_len), BF16)
        kv_specs = [kv_spec] * 4
        kv_bytes = 2 * 2 * sb * tm * (4 + 2)
    else:
        hm_spec = pl.BlockSpec((tm * SB_HEADS, SB_HEAD_DIM), lambda i: (i, 0))
        kv_f32 = jax.ShapeDtypeStruct((n_tok * SB_HEADS, SB_HEAD_DIM), F32)
        kv_bf16 = jax.ShapeDtypeStruct((n_tok, sb), BF16)
        kv_specs = [hm_spec, hm_spec, tok(sb), tok(sb)]
        kv_bytes = 2 * 2 * tm * (SB_HEADS * LANES * 4 + sb * 2)
    out_specs = [tok(gm), tok(sb)] + kv_specs + [tok(n_gate)]
    out_shape = [jax.ShapeDtypeStruct((n_tok, gm), BF16), jax.ShapeDtypeStruct((n_tok, sb), BF16),
                 kv_f32, kv_f32, kv_bf16, kv_bf16, jax.ShapeDtypeStruct((n_tok, n_gate), BF16)]
    if emit_va:
        out_specs.append(tok(gm))
        out_shape.append(jax.ShapeDtypeStruct((n_tok, gm), F32))

    weight_bytes = (win.size + wk.size + wv.size + wg.size + ws.size) * 2
    block_bytes = 2 * tm * (d * 4 + gm * 2 + sb * 2 + n_gate * 2 + gm * 4) + kv_bytes
    scratch_bytes = tm * (d * 2 + gm * 4 + gm * 4 + gm * 2)
    return pl.pallas_call(
        functools.partial(_proj_kernel, sub=sub, n_sub=n_sub, chunk=chunk, transposed_kv=transposed_kv,
                          emit_va=emit_va),
        grid=(n_tok // tm,),
        in_specs=[tok(d), _full(gpre.shape), _full(win.shape), _full(wk.shape), _full(wv.shape),
                  _full(lng.shape), _full(lnb.shape), _full(ws.shape), _full(bias.shape), _full(wg.shape),
                  _full(bg.shape)],
        out_specs=out_specs,
        out_shape=out_shape,
        scratch_shapes=[pltpu.VMEM((n_sub, sub, d), BF16), pltpu.VMEM((n_sub, sub, gm), F32),
                        pltpu.VMEM((n_sub, sub, gm), F32), pltpu.VMEM((n_sub, sub, gm), BF16)],
        compiler_params=_params(1, weight_bytes + block_bytes + SCRATCH_AND_TEMPORARIES * scratch_bytes),
        name="proj",
    )(x2d, gpre, win, wk, wv, lng, lnb, ws, bias, wg, bg)


class _Tile(NamedTuple):
    head: int
    scores: Callable
    values: Callable
    v_transposed: bool
    tri: jax.Array
    mask: Optional[jax.Array]
    first: bool


def _sb_tiles(tiles, c_ref, cmin_ref, acc_ref):
    n = len(tiles)
    z, sp, cum, weight = ([None] * n for _ in range(4))

    def scores(i):
        z[i] = tiles[i].scores()

    def softplus(i):
        capped = jnp.log2(1.0 + jnp.exp2(jnp.minimum(z[i], SOFTPLUS2_LINEAR_FROM)))
        s = jnp.where(z[i] > SOFTPLUS2_LINEAR_FROM, z[i], capped)
        sp[i] = s if tiles[i].mask is None else jnp.where(tiles[i].mask, s, 0.0)

    def cumsum(i):
        cum[i] = _dot(sp[i].astype(BF16), tiles[i].tri)

    def weights(i):
        h = tiles[i].head
        total = sp[i] + cum[i]
        if not tiles[i].first:
            total = total + c_ref[h]
        a = jnp.exp2(z[i] - total)
        if tiles[i].mask is not None:
            a = jnp.where(tiles[i].mask, a, 0.0)
        weight[i] = a.astype(BF16)
        c_ref[h] = total[:, 0:1]
        tq, width = total.shape[0], min(LANES, total.shape[1])
        cmin_ref[h, :, 0:width] = jnp.min(total[:, 0:width].reshape(tq // SUBLANES, SUBLANES, width), axis=0)
        z[i] = sp[i] = cum[i] = None

    def values(i):
        v = tiles[i].values()
        contrib = _dot_nt(weight[i], v) if tiles[i].v_transposed else _dot(weight[i], v)
        h = tiles[i].head
        lanes = _pair_lanes(h)
        if tiles[i].first and h % HEADS_PER_LANE_TILE == 0:
            acc_ref[:, lanes] = contrib
        else:
            acc_ref[:, lanes] = acc_ref[:, lanes] + contrib
        weight[i] = None

    stages = (scores, softplus, cumsum, weights, values)
    for step in range(n + len(stages) - 1):
        for s, stage in enumerate(stages):
            if 0 <= step - s < n:
                stage(step - s)


def _pair_lanes(h):
    return slice(h // HEADS_PER_LANE_TILE * LANES, (h // HEADS_PER_LANE_TILE + 1) * LANES)


def _strict_tri(n):
    r = lax.broadcasted_iota(jnp.int32, (n, n), 0)
    s = lax.broadcasted_iota(jnp.int32, (n, n), 1)
    return jnp.where(r > s, 1.0, 0.0).astype(BF16)


def _causal_mask(n):
    r = lax.broadcasted_iota(jnp.int32, (n, n), 0)
    s = lax.broadcasted_iota(jnp.int32, (n, n), 1)
    return s < r


def _head_lane_masks():
    lane = lax.broadcasted_iota(jnp.int32, (1, LANES), 1)
    return [(lane >= hh * SB_HEAD_DIM) & (lane < (hh + 1) * SB_HEAD_DIM) for hh in range(HEADS_PER_LANE_TILE)]


def _head_row_masks():
    row = lax.broadcasted_iota(jnp.int32, (LANES, 1), 0)
    return [(row >= hh * SB_HEAD_DIM) & (row < (hh + 1) * SB_HEAD_DIM) for hh in range(HEADS_PER_LANE_TILE)]


def _masked_queries(q_ref, rows, qm_s, heads):
    lane_masks = _head_lane_masks()
    for h in range(heads):
        lt = h // HEADS_PER_LANE_TILE
        qt = q_ref[rows, lt * LANES:(lt + 1) * LANES]
        qm_s[h] = jnp.where(lane_masks[h % HEADS_PER_LANE_TILE], qt, jnp.zeros_like(qt))


def _sweep_is_live(cmin_ref):
    return (jnp.min(cmin_ref[:, :, 0:1]) < F32_EXP2_UNDERFLOW).astype(jnp.int32)


def _sb_prompt_kernel(q_ref, kt_ref, vt_ref, o_ref, acc_s, c_s, cmin_s, u_s, qm_s, *, tq, heads):
    n_blocks = q_ref.shape[0] // tq
    u_s[...] = _strict_tri(tq)
    row_masks = _head_row_masks()

    def tiles(j, diagonal):
        keys = pl.ds(pl.multiple_of(j * tq, tq), tq)
        mask = _causal_mask(tq) if diagonal else None

        def scores(h):
            return _dot(qm_s[h], kt_ref[_pair_lanes(h), keys])

        def values(h):
            vt = vt_ref[_pair_lanes(h), keys]
            return jnp.where(row_masks[h % HEADS_PER_LANE_TILE], vt, jnp.zeros_like(vt))

        return [_Tile(h, functools.partial(scores, h), functools.partial(values, h), True, u_s[...], mask,
                      first=diagonal) for h in range(heads)]

    def query_block(i, has_previous):
        rows = pl.ds(pl.multiple_of(i * tq, tq), tq)
        _masked_queries(q_ref, rows, qm_s, heads)
        if has_previous:
            _sb_tiles(tiles(i, True) + tiles(i - 1, False), c_s, cmin_s, acc_s)

            def cond(carry):
                n, go = carry
                return jnp.logical_and(n < i - 1, go > 0)

            def body(carry):
                n, _ = carry
                _sb_tiles(tiles(i - 2 - n, False), c_s, cmin_s, acc_s)
                return n + 1, _sweep_is_live(cmin_s)

            lax.while_loop(cond, body, (jnp.int32(0), _sweep_is_live(cmin_s)))
        else:
            _sb_tiles(tiles(i, True), c_s, cmin_s, acc_s)
        o_ref[rows, :] = acc_s[...].astype(BF16)

    query_block(0, False)

    def later_blocks(i, carry):
        query_block(i, True)
        return carry

    lax.fori_loop(1, n_blocks, later_blocks, 0)


def _sb_prompt_call(q, kt, vt, *, tq, heads):
    b, t, w = q.shape
    width = heads * SB_HEAD_DIM
    qspec = pl.BlockSpec((None, t, width), lambda bi, hg: (bi, 0, hg))
    kvspec = pl.BlockSpec((None, width, t), lambda bi, hg: (bi, hg, 0))
    vmem = (2 * 4 * t * width * 2 + tq * width * 4 + heads * tq * LANES * 4
            + tq * tq * 2 + heads * tq * LANES * 2 + heads * SB_TILE_VALUES_IN_FLIGHT * tq * tq * 4)
    return pl.pallas_call(
        functools.partial(_sb_prompt_kernel, tq=tq, heads=heads),
        grid=(b, w // width),
        in_specs=[qspec, kvspec, kvspec],
        out_specs=qspec,
        out_shape=jax.ShapeDtypeStruct((b, t, w), BF16),
        scratch_shapes=[pltpu.VMEM((tq, width), F32),
                        pltpu.VMEM((heads, tq, 1), F32),
                        pltpu.VMEM((heads, SUBLANES, LANES), F32),
                        pltpu.VMEM((tq, tq), BF16),
                        pltpu.VMEM((heads, tq, LANES), BF16)],
        compiler_params=_params(2, vmem),
        name="sb_prompt",
    )(q, kt, vt)


def _sb_sample_kernel(q_ref, kn_ref, vn_ref, kp_hbm, vp_hbm, o_ref, kbuf, vbuf, sem, acc_s, c_s, cmin_s, u_s, ud_s,
                      qm_s, *, tq, tk, n_past):
    b = pl.program_id(0)

    def copies(j, slot):
        keys = pl.ds(pl.multiple_of(j * tk, tk), tk)
        return (pltpu.make_async_copy(kp_hbm.at[b, :, keys], kbuf.at[slot], sem.at[0, slot]),
                pltpu.make_async_copy(vp_hbm.at[b, :, keys], vbuf.at[slot], sem.at[1, slot]))

    def start(j, slot):
        for cp in copies(j, slot):
            cp.start()

    def wait(j, slot):
        for cp in copies(j, slot):
            cp.wait()

    start(n_past - 1, 0)
    u_s[...] = _strict_tri(tk)
    ud_s[...] = _strict_tri(tq)
    _masked_queries(q_ref, slice(None), qm_s, SB_HEADS)
    lane_masks = _head_lane_masks()
    row_masks = _head_row_masks()

    def z_new(h):
        return _dot_nt(qm_s[h], kn_ref[:, _pair_lanes(h)])

    def v_new(h):
        vn = vn_ref[:, _pair_lanes(h)]
        return jnp.where(lane_masks[h % HEADS_PER_LANE_TILE], vn, jnp.zeros_like(vn))

    mask = _causal_mask(tq)
    _sb_tiles([_Tile(h, functools.partial(z_new, h), functools.partial(v_new, h), False, ud_s[...], mask,
                     first=True) for h in range(SB_HEADS)], c_s, cmin_s, acc_s)

    def cond(carry):
        n, go = carry
        return jnp.logical_and(n < n_past, go > 0)

    def body(carry):
        n, _ = carry
        j = n_past - 1 - n
        slot = lax.rem(n, 2)
        wait(j, slot)

        @pl.when(n + 1 < n_past)
        def _():
            start(j - 1, 1 - slot)

        def z_past(h):
            return _dot(qm_s[h], kbuf[slot, _pair_lanes(h), :].astype(BF16))

        def v_past(h):
            vt = vbuf[slot, _pair_lanes(h), :].astype(BF16)
            return jnp.where(row_masks[h % HEADS_PER_LANE_TILE], vt, jnp.zeros_like(vt))

        _sb_tiles([_Tile(h, functools.partial(z_past, h), functools.partial(v_past, h), True, u_s[...], None,
                         first=False) for h in range(SB_HEADS)], c_s, cmin_s, acc_s)
        return n + 1, _sweep_is_live(cmin_s)

    n_done, _ = lax.while_loop(cond, body, (jnp.int32(0), _sweep_is_live(cmin_s)))

    @pl.when(n_done < n_past)
    def _():
        wait(n_past - 1 - n_done, lax.rem(n_done, 2))

    o_ref[...] = acc_s[...].astype(BF16)


def _sb_sample_call(q, k_new, v_new, kt_past, vt_past, *, tk):
    b, tq, w = q.shape
    p = kt_past.shape[2]
    newspec = pl.BlockSpec((None, tq, w), lambda bi: (bi, 0, 0))
    anyspec = pl.BlockSpec(memory_space=pl.ANY)
    vmem = (2 * 2 * w * tk * 4 + 2 * 4 * tq * w * 2 + tq * w * 4 + SB_HEADS * tq * LANES * 4
            + 2 * tk * tk * 2 + SB_HEADS * tq * LANES * 2 + SB_HEADS * SB_TILE_VALUES_IN_FLIGHT * tq * tk * 4)
    return pl.pallas_call(
        functools.partial(_sb_sample_kernel, tq=tq, tk=tk, n_past=p // tk),
        grid=(b,),
        in_specs=[newspec, newspec, newspec, anyspec, anyspec],
        out_specs=newspec,
        out_shape=jax.ShapeDtypeStruct((b, tq, w), BF16),
        scratch_shapes=[pltpu.VMEM((2, w, tk), F32),
                        pltpu.VMEM((2, w, tk), F32),
                        pltpu.SemaphoreType.DMA((2, 2)),
                        pltpu.VMEM((tq, w), F32),
                        pltpu.VMEM((SB_HEADS, tq, 1), F32),
                        pltpu.VMEM((SB_HEADS, SUBLANES, LANES), F32),
                        pltpu.VMEM((tk, tk), BF16),
                        pltpu.VMEM((tq, tq), BF16),
                        pltpu.VMEM((SB_HEADS, tq, LANES), BF16)],
        compiler_params=_params(1, vmem),
        name="sb_sample",
    )(q, k_new, v_new, kt_past, vt_past)


def _merge_kernel(x_ref, ya_ref, yb_ref, gate_ref, wa_ref, wb_ref, wo_ref, gpost_ref, o_ref, m_s, p_s,
                  *, sub, n_sub):
    d = x_ref.shape[1]

    def stages_of(i):
        tok = slice(i * sub, (i + 1) * sub)

        def merge(n):
            cols = slice(n * MXU_COLS, (n + 1) * MXU_COLS)
            gcols = slice(d + n * MXU_COLS, d + (n + 1) * MXU_COLS)
            merged = (gate_ref[tok, cols].astype(F32) * _dot(ya_ref[tok, :], wa_ref[:, cols])
                      + gate_ref[tok, gcols].astype(F32) * _dot(yb_ref[tok, :], wb_ref[:, cols]))
            m_s[i, :, cols] = merged.astype(BF16)

        def project(n):
            cols = slice(n * MXU_COLS, (n + 1) * MXU_COLS)
            p_s[i, :, cols] = _dot(m_s[i], wo_ref[:, cols])

        def norm_and_store():
            o_ref[tok, :] = x_ref[tok, :] + _rms(p_s[i], gpost_ref[...])

        n_chunks = d // MXU_COLS
        return ([functools.partial(merge, n) for n in range(n_chunks)]
                + [functools.partial(project, n) for n in range(n_chunks)] + [norm_and_store])

    _skewed([stages_of(i) for i in range(n_sub)], SUB_BLOCK_SKEW)


def _merge_call(x2d, ya, yb, gates, wa, wb, wo, gpost, *, sub, n_sub):
    n_tok, d = x2d.shape
    tm = sub * n_sub

    def tok(width):
        return pl.BlockSpec((tm, width), lambda i: (i, 0))

    weight_bytes = (wa.size + wb.size + wo.size) * 2
    block_bytes = 2 * tm * (d * 4 + ya.shape[1] * 2 + yb.shape[1] * 2 + gates.shape[1] * 2 + d * 4)
    scratch_bytes = tm * d * 6
    return pl.pallas_call(
        functools.partial(_merge_kernel, sub=sub, n_sub=n_sub),
        grid=(n_tok // tm,),
        in_specs=[tok(d), tok(ya.shape[1]), tok(yb.shape[1]), tok(gates.shape[1]),
                  _full(wa.shape), _full(wb.shape), _full(wo.shape), _full(gpost.shape)],
        out_specs=tok(d),
        out_shape=jax.ShapeDtypeStruct((n_tok, d), F32),
        scratch_shapes=[pltpu.VMEM((n_sub, sub, d), BF16), pltpu.VMEM((n_sub, sub, d), F32)],
        compiler_params=_params(1, weight_bytes + block_bytes + SCRATCH_AND_TEMPORARIES * scratch_bytes),
        name="merge",
    )(x2d, ya, yb, gates, wa, wb, wo, gpost)


def _skewed(stage_lists, skew):
    n_stages = len(stage_lists[0])
    for step in range(n_stages + skew * (len(stage_lists) - 1)):
        for i, stages in enumerate(stage_lists):
            if 0 <= step - i * skew < n_stages:
                stages[step - i * skew]()


def _ffn_kernel(x_ref, prev_ref, gpre_ref, wup_ref, wc_ref, bc_ref, wdown_ref, gpost_ref,
                y_ref, newconv_ref, xp_s, hn_s, carry_s, act_s, f_s, *, sub, n_sub):
    assert CONV_WIDTH == 3
    dff = wdown_ref.shape[0]
    d = x_ref.shape[1]
    ng = sub // SUBLANES
    assert ng >= CONV_WIDTH - 1 and ng % SUBLANES == 0
    n_slabs = d // LANES
    t = pl.program_id(1)

    @pl.when(t == 0)
    def _():
        carry_s[0:CONV_WIDTH - 1, :] = prev_ref[...]

    top = lax.broadcasted_iota(jnp.int32, (SUBLANES, MXU_COLS), 0) == 0

    def permuted_rows(r):
        return pl.ds((SUBLANES * r % ng) * SUBLANES + SUBLANES * r // ng, SUBLANES, stride=SUBLANES)

    def stages_of(i):
        base = i * sub

        def load_and_norm():
            for r in range(ng):
                rows = slice(base + r * SUBLANES, base + (r + 1) * SUBLANES)
                for c in range(n_slabs):
                    xp_s[i, c, permuted_rows(r), :] = x_ref[rows, c * LANES:(c + 1) * LANES]
            xp = jnp.concatenate([xp_s[i, c] for c in range(n_slabs)], axis=1)
            hn_s[i] = _rms(xp, gpre_ref[...]).astype(BF16)

        def up_conv_act(n):
            halves = []
            for off in (0, dff):
                cols = slice(off + n * MXU_COLS, off + (n + 1) * MXU_COLS)
                up = _dot(hn_s[i], wup_ref[:, cols])
                back1 = jnp.where(top, carry_s[1:2, cols], pltpu.roll(up[sub - SUBLANES:sub], 1, axis=0))
                back2 = jnp.where(top, carry_s[0:1, cols],
                                  pltpu.roll(up[sub - 2 * SUBLANES:sub - SUBLANES], 1, axis=0))
                carry_s[0:1, cols] = up[sub - SUBLANES - 1:sub - SUBLANES]
                carry_s[1:2, cols] = up[sub - 1:sub]
                prev1 = jnp.concatenate([back1, up[:sub - SUBLANES]], axis=0)
                prev2 = jnp.concatenate([back2, back1, up[:sub - 2 * SUBLANES]], axis=0)
                halves.append(bc_ref[:, cols] + wc_ref[0:1, cols] * prev2 + wc_ref[1:2, cols] * prev1
                              + wc_ref[2:3, cols] * up)
            act_s[i, :, n * MXU_COLS:(n + 1) * MXU_COLS] = (jax.nn.gelu(halves[0]) * halves[1]).astype(BF16)

        def down(n):
            cols = slice(n * MXU_COLS, (n + 1) * MXU_COLS)
            f_s[i, :, cols] = _dot(act_s[i], wdown_ref[:, cols])

        def norm_and_store():
            normed = _rms(f_s[i], gpost_ref[...])
            for c in range(n_slabs):
                xp_s[i, c] = xp_s[i, c] + normed[:, c * LANES:(c + 1) * LANES]
            for r in range(ng):
                rows = slice(base + r * SUBLANES, base + (r + 1) * SUBLANES)
                for c in range(n_slabs):
                    y_ref[rows, c * LANES:(c + 1) * LANES] = xp_s[i, c, permuted_rows(r), :]
            if i == n_sub - 1:
                newconv_ref[...] = carry_s[0:CONV_WIDTH - 1, :]

        return ([load_and_norm] + [functools.partial(up_conv_act, n) for n in range(dff // MXU_COLS)]
                + [functools.partial(down, n) for n in range(d // MXU_COLS)] + [norm_and_store])

    _skewed([stages_of(i) for i in range(n_sub)], SUB_BLOCK_SKEW)


def _ffn_call(x, conv_prev, gpre, wup, wc, bc, wdown, gpost, *, sub, n_sub):
    b, t, d = x.shape
    dff = wdown.shape[0]
    tm = sub * n_sub
    xspec = pl.BlockSpec((None, tm, d), lambda bi, ti: (bi, ti, 0))
    cspec = pl.BlockSpec((None, CONV_WIDTH - 1, 2 * dff), lambda bi, ti: (bi, 0, 0))
    weight_bytes = (wup.size + wdown.size) * 2
    block_bytes = 2 * 2 * tm * d * 4
    scratch_bytes = tm * d * 4 + tm * d * 2 + SUBLANES * 2 * dff * 4 + tm * dff * 2 + tm * d * 4
    return pl.pallas_call(
        functools.partial(_ffn_kernel, sub=sub, n_sub=n_sub),
        grid=(b, t // tm),
        in_specs=[xspec, cspec, _full(gpre.shape), _full(wup.shape), _full(wc.shape), _full(bc.shape),
                  _full(wdown.shape), _full(gpost.shape)],
        out_specs=[xspec, cspec],
        out_shape=[jax.ShapeDtypeStruct((b, t, d), F32),
                   jax.ShapeDtypeStruct((b, CONV_WIDTH - 1, 2 * dff), F32)],
        scratch_shapes=[pltpu.VMEM((n_sub, d // LANES, sub, LANES), F32),
                        pltpu.VMEM((n_sub, sub, d), BF16),
                        pltpu.VMEM((SUBLANES, 2 * dff), F32),
                        pltpu.VMEM((n_sub, sub, dff), BF16),
                        pltpu.VMEM((n_sub, sub, d), F32)],
        compiler_params=_params(2, weight_bytes + block_bytes + SCRATCH_AND_TEMPORARIES * scratch_bytes),
        name="ffn",
    )(x, conv_prev, gpre, wup, wc, bc, wdown, gpost)


def _layer(x, past_k, past_v, conv_prev, w):
    b, t, d = x.shape
    gm = GM_GROUPS * GM_GROUP_DIM
    sbw = SB_HEADS * SB_HEAD_DIM
    chunk = min(t, GM_CHUNK)
    n_tok = b * t
    decode = past_k is not None
    tm = min(SEQ_TILE, t)
    n_sub = min(SUB_BLOCKS_PER_STEP, t // tm)
    assert t % (tm * n_sub) == 0 and tm % chunk == 0, (t, tm, n_sub, chunk)

    tri = jnp.tril(jnp.ones((chunk, chunk), dtype=bool))
    ws = jnp.where(tri[None], w["w_s"][:, :chunk, :chunk], 0.0).astype(BF16)
    bias = jnp.repeat(w["b_s"][:, :chunk].T, GM_GROUP_DIM, axis=1)
    w_in = w["w_in"]
    w_uvq = w_in[:, :2 * gm + sbw]
    w_k = w_in[:, 2 * gm + sbw:2 * gm + 2 * sbw]
    w_v = w_in[:, 2 * gm + 2 * sbw:]
    if not decode:
        w_k, w_v = w_k.T, w_v.T

    flat_n_sub = min(SUB_BLOCKS_PER_STEP, n_tok // tm)
    flat_sub = max(tm, min(SEQ_TILE, n_tok // flat_n_sub))
    assert n_tok % (flat_sub * flat_n_sub) == 0 and flat_sub % chunk == 0, (n_tok, flat_sub, flat_n_sub)
    outs = _proj_call(
        x.reshape(n_tok, d), w["g_pre_mix"], w_uvq, w_k, w_v, w["ln_v_g"], w["ln_v_b"], ws, bias,
        w["w_gate"], w["b_gate"], sub=flat_sub if decode else tm, n_sub=flat_n_sub if decode else n_sub,
        chunk=chunk, seq_len=t, transposed_kv=not decode, emit_va=decode)
    ya, q, k_out, v_out, kb, vb, gates = outs[:7]
    q3 = q.reshape(b, t, sbw)

    if decode:
        p = past_k.shape[1]
        kt_past = jnp.transpose(past_k, (0, 2, 3, 1)).reshape(b, sbw, p)
        vt_past = jnp.transpose(past_v, (0, 2, 3, 1)).reshape(b, sbw, p)
        yb = _sb_sample_call(q3, kb.reshape(b, t, sbw), vb.reshape(b, t, sbw), kt_past, vt_past,
                             tk=min(p, SEQ_TILE))
        k4 = k_out.reshape(b, t, SB_HEADS, SB_HEAD_DIM)
        v4 = v_out.reshape(b, t, SB_HEADS, SB_HEAD_DIM)
        va = outs[7].reshape(b, t, gm)
    else:
        yb = _sb_prompt_call(q3, kb, vb, tq=tm, heads=SB_PROMPT_HEADS_PER_STEP)
        k4 = jnp.transpose(k_out.reshape(b, SB_HEADS, SB_HEAD_DIM, t), (0, 3, 1, 2))
        v4 = jnp.transpose(v_out.reshape(b, SB_HEADS, SB_HEAD_DIM, t), (0, 3, 1, 2))
        va = None

    wide = min(2 * tm, t // n_sub)
    flat_wide = max(wide, min(2 * flat_sub, n_tok // flat_n_sub))
    x1 = _merge_call(x.reshape(n_tok, d), ya, yb.reshape(n_tok, sbw), gates,
                     w["w_branch_a"], w["w_branch_b"], w["w_o"], w["g_post_mix"], sub=flat_wide, n_sub=flat_n_sub)
    y, conv_new = _ffn_call(x1.reshape(b, t, d), conv_prev, w["g_pre_ffn"], w["w_up"], w["w_conv"],
                            w["b_conv"], w["w_down"], w["g_post_ffn"], sub=wide, n_sub=n_sub)
    return y, k4, v4, conv_new, va


def kernel(x_prompt, x_sample, cache_sb_k, cache_sb_v, state_ffn_conv, g_pre_mix, w_in, ln_v_g, ln_v_b, w_s, b_s,
           w_gate, b_gate, w_branch_a, w_branch_b, w_o, g_post_mix, g_pre_ffn, w_up, w_conv, b_conv, w_down,
           g_post_ffn):
    depth = w_in.shape[0]
    y_p, y_s = x_prompt, x_sample
    outs = [[] for _ in range(7)]
    for l in range(depth):
        row = lambda a: a[l][None, :]
        w = dict(
            g_pre_mix=row(g_pre_mix), w_in=w_in[l].astype(BF16), ln_v_g=row(ln_v_g), ln_v_b=row(ln_v_b),
            w_s=w_s[l], b_s=b_s[l], w_gate=w_gate[l].astype(BF16), b_gate=row(b_gate),
            w_branch_a=w_branch_a[l].astype(BF16), w_branch_b=w_branch_b[l].astype(BF16),
            w_o=w_o[l].astype(BF16), g_post_mix=row(g_post_mix), g_pre_ffn=row(g_pre_ffn),
            w_up=w_up[l].astype(BF16), w_conv=w_conv[l], b_conv=row(b_conv),
            w_down=w_down[l].astype(BF16), g_post_ffn=row(g_post_ffn))
        zero_conv = jnp.zeros((y_p.shape[0], CONV_WIDTH - 1, w_up.shape[2]), y_p.dtype)
        y_p, kp, vp, cp, _ = _layer(y_p, None, None, zero_conv, w)
        y_s, ks, vs, cs, gvs = _layer(y_s, cache_sb_k[l], cache_sb_v[l], state_ffn_conv[l], w)
        for lst, a in zip(outs, (kp, vp, cp, ks, vs, cs, gvs)):
            lst.append(a)
    return (y_p, y_s) + tuple(jnp.stack(o) for o in outs)
```
